```python
import jax
import jax.numpy as jnp
from jax import lax
import numpy as np

D_MODEL = 4096
BATCH = 8
SEQ = 2048
DEPTH = 2

CTX_LEN = 256
GRID_W = 64
N_GROUPS = 4
GROUP_W = D_MODEL // N_GROUPS
N_HEADS = 8
HEAD_DIM = GROUP_W // N_HEADS
N_IN_SPLITS = 14
N_MOD = 6
LRU_CONV_W = 4
LRU_C = 8.0
NA_WIN_ROWS = 8
NA_WIN_COLS = 16
NA_QBLOCK_W = NA_WIN_COLS
NA_KBAND_W = 2 * NA_WIN_COLS
RET_CHUNK = 128
HGRN_CHUNK = 32
PEER_HEADS = 8
PEER_NKEYS = 128
PEER_N_EXPERTS = PEER_NKEYS * PEER_NKEYS
PEER_QDIM = 256
PEER_TOPK = 16
PEER_TOKEN_BLOCK = 64
ROPE_BASE = 10000.0
EPS = 1e-6
NEG_INF = -1e30

kernel_name = 'hybrid_dit_rglru_natten_retnet_hgrn2_peer'


def rmsnorm(z, g):
    zf = z.astype(jnp.float32)
    zf = zf * lax.rsqrt(jnp.mean(zf * zf, axis=-1, keepdims=True) + EPS)
    return (zf * g.astype(jnp.float32)).astype(z.dtype)


def split_apply(z, n_ctx, f_ctx, f_lat):
    if n_ctx == 0:
        return f_lat(z)
    return jnp.concatenate([f_ctx(z[:, :n_ctx]), f_lat(z[:, n_ctx:])], axis=1)


def seg_flip(z, axis):
    n = z.shape[axis]
    head = lax.slice_in_dim(z, 0, CTX_LEN, axis=axis)
    tail = lax.slice_in_dim(z, CTX_LEN, n, axis=axis)
    return jnp.concatenate([jnp.flip(head, axis), jnp.flip(tail, axis)], axis=axis)


def to_heads(z):
    b, l, _ = z.shape
    return z.reshape(b, l, N_HEADS, -1).transpose(0, 2, 1, 3)


def merge_heads(z):
    b, h, l, d = z.shape
    return z.transpose(0, 2, 1, 3).reshape(b, l, h * d)


def head_norm(o, g, center):
    if center:
        o = o - jnp.mean(o, axis=-1, keepdims=True)
    o = o * lax.rsqrt(jnp.mean(o * o, axis=-1, keepdims=True) + EPS)
    return merge_heads(o) * g.astype(jnp.float32)


def rope_2d(z):
    n, dh = z.shape[2], z.shape[3]
    quarter = dh // 4
    t = jnp.arange(n)
    inv_freq = ROPE_BASE ** (-jnp.arange(quarter, dtype=jnp.float32) / quarter)

    def rotate(za, pos):
        ang = pos.astype(jnp.float32)[:, None] * inv_freq[None, :]
        cos, sin = jnp.cos(ang), jnp.sin(ang)
        z1, z2 = za[..., :quarter], za[..., quarter:]
        return jnp.concatenate([z1 * cos - z2 * sin, z1 * sin + z2 * cos], axis=-1)

    zf = z.astype(jnp.float32)
    out = jnp.concatenate([rotate(zf[..., :2 * quarter], t // GRID_W),
                           rotate(zf[..., 2 * quarter:], t % GRID_W)], axis=-1)
    return out.astype(z.dtype)


def chunk_decay_attn(q, k, v, log_f, chunk):
    bsz, h, l, dk = q.shape
    dv = v.shape[-1]
    n = l // chunk
    f32 = jnp.float32
    qc = q.astype(f32).reshape(bsz, h, n, chunk, dk)
    kc = k.astype(f32).reshape(bsz, h, n, chunk, dk)
    vc = v.astype(f32).reshape(bsz, h, n, chunk, dv)
    lf = jnp.broadcast_to(log_f.astype(f32), (bsz, h, l, dk)).reshape(bsz, h, n, chunk, dk)
    b = jnp.cumsum(lf, axis=3)
    b_last = b[:, :, :, -1:, :]
    q_dec = qc * jnp.exp(b)
    scores = jnp.einsum('bhncd,bhnsd->bhncs', q_dec, kc * jnp.exp(-b))
    lower = jnp.tril(jnp.ones((chunk, chunk), dtype=bool))
    o_intra = jnp.einsum('bhncs,bhnsv->bhncv', jnp.where(lower, scores, 0.0), vc)
    k_state = kc * jnp.exp(b_last - b)
    decay = jnp.exp(b_last[:, :, :, 0, :])

    def step(state, xs):
        qd, ks, vs, dec = xs
        out = jnp.einsum('bhcd,bhdv->bhcv', qd, state)
        state = dec[..., None] * state + jnp.einsum('bhcd,bhcv->bhdv', ks, vs)
        return state, out

    xs = (jnp.moveaxis(q_dec, 2, 0), jnp.moveaxis(k_state, 2, 0),
          jnp.moveaxis(vc, 2, 0), jnp.moveaxis(decay, 2, 0))
    _, o_inter = lax.scan(step, jnp.zeros((bsz, h, dk, dv), f32), xs)
    return (o_intra + jnp.moveaxis(o_inter, 0, 2)).reshape(bsz, h, l, dv)


def rglru_group(xb, gb, conv_w, conv_b, wa, ba, wx, bx, lam):
    f32 = jnp.float32

    def centred_conv(z):
        n = z.shape[1]
        left = LRU_CONV_W // 2
        zp = jnp.pad(z.astype(f32), ((0, 0), (left, LRU_CONV_W - 1 - left), (0, 0)))
        out = conv_b.astype(f32) + zp[:, 0:n] * conv_w[0].astype(f32)
        for tap in range(1, LRU_CONV_W):
            out = out + zp[:, tap:tap + n] * conv_w[tap].astype(f32)
        return out

    xc = split_apply(xb, CTX_LEN, centred_conv, centred_conv)
    bsz, l, w = xc.shape
    xr = xc.reshape(bsz, l, N_HEADS, w // N_HEADS)

    def coeffs(d):
        r = jax.nn.sigmoid(jnp.einsum('blhi,hij->blhj', xr, wa[d].astype(f32)).reshape(bsz, l, w)
                           + ba[d].astype(f32))
        i = jax.nn.sigmoid(jnp.einsum('blhi,hij->blhj', xr, wx[d].astype(f32)).reshape(bsz, l, w)
                           + bx[d].astype(f32))
        log_a = -LRU_C * r * jax.nn.softplus(-lam[d].astype(f32))
        return jnp.exp(log_a), jnp.sqrt(-jnp.expm1(2.0 * log_a)) * (i * xc)

    def linear_scan(a, u):
        def combine(left, right):
            return left[0] * right[0], right[0] * left[1] + right[1]
        return lax.associative_scan(combine, (a, u), axis=1)[1]

    a_f, u_f = coeffs(0)
    a_b, u_b = coeffs(1)
    h = linear_scan(a_f, u_f) + seg_flip(linear_scan(seg_flip(a_b, 1), seg_flip(u_b, 1)), 1)
    return (jax.nn.gelu(gb.astype(f32)) * h).astype(xb.dtype)


def na_group(q, k, v, qn_g, kn_g, rpb):
    bsz, h, l, dh = q.shape
    f32 = jnp.float32
    scale = dh ** -0.5
    q = rmsnorm(q, qn_g)
    k = rmsnorm(k, kn_g)
    q_c, k_c, v_c = q[:, :, :CTX_LEN], k[:, :, :CTX_LEN], v[:, :, :CTX_LEN]
    q_l, k_l, v_l = q[:, :, CTX_LEN:], k[:, :, CTX_LEN:], v[:, :, CTX_LEN:]
    s_c = jnp.einsum('bhqd,bhkd->bhqk', q_c, k_c).astype(f32) * scale
    o_c = jnp.einsum('bhqk,bhkd->bhqd', jax.nn.softmax(s_c, axis=-1).astype(v.dtype), v_c)

    n_lat = l - CTX_LEN
    rows = n_lat // GRID_W
    kr = min(NA_WIN_ROWS, rows)
    n_cb = GRID_W // NA_QBLOCK_W
    q_g = q_l.reshape(bsz, h, rows, GRID_W, dh)
    k_g = k_l.reshape(bsz, h, rows, GRID_W, dh)
    v_g = v_l.reshape(bsz, h, rows, GRID_W, dh)
    q_col = jnp.arange(GRID_W).reshape(n_cb, NA_QBLOCK_W)
    band0 = jnp.clip(jnp.arange(n_cb) * NA_QBLOCK_W - NA_WIN_COLS // 2, 0, GRID_W - NA_KBAND_W)
    k_col = band0[:, None] + jnp.arange(NA_KBAND_W)[None, :]
    win0 = jnp.clip(q_col - NA_WIN_COLS // 2, 0, GRID_W - NA_WIN_COLS)
    col_ok = ((k_col[:, None, :] >= win0[:, :, None])
              & (k_col[:, None, :] < win0[:, :, None] + NA_WIN_COLS))
    rel_c = jnp.clip(k_col[:, None, :] - q_col[:, :, None] + NA_WIN_COLS - 1, 0, 2 * NA_WIN_COLS - 2)
    rpb_col = rpb.astype(f32)[:, :, rel_c]

    def row_block(r):
        r0 = jnp.clip(r - kr // 2, 0, rows - kr)
        k_band = lax.dynamic_slice_in_dim(k_g, r0, kr, axis=2)[:, :, :, k_col]
        v_band = lax.dynamic_slice_in_dim(v_g, r0, kr, axis=2)[:, :, :, k_col]
        q_row = lax.dynamic_index_in_dim(q_g, r, axis=2, keepdims=False)
        q_row = q_row.reshape(bsz, h, n_cb, NA_QBLOCK_W, dh)
        s_loc = jnp.einsum('bhjqd,bhrjkd->bhjqrk', q_row, k_band).astype(f32) * scale
        bias = jnp.take(rpb_col, r0 + jnp.arange(kr) - r + NA_WIN_ROWS - 1, axis=1)
        s_loc = jnp.where(col_ok[:, :, None, :], s_loc + bias.transpose(0, 2, 3, 1, 4), NEG_INF)
        s_ctx = jnp.einsum('bhjqd,bhcd->bhjqc', q_row, k_c).astype(f32) * scale
        n_loc = kr * NA_KBAND_W
        s_all = jnp.concatenate([s_loc.reshape(bsz, h, n_cb, NA_QBLOCK_W, n_loc), s_ctx], axis=-1)
        p = jax.nn.softmax(s_all, axis=-1).astype(v.dtype)
        p_loc = p[..., :n_loc].reshape(bsz, h, n_cb, NA_QBLOCK_W, kr, NA_KBAND_W)
        o = (jnp.einsum('bhjqrk,bhrjkd->bhjqd', p_loc, v_band)
             + jnp.einsum('bhjqc,bhcd->bhjqd', p[..., n_loc:], v_c))
        return o.reshape(bsz, h, GRID_W, dh)

    o_l = lax.map(row_block, jnp.arange(rows))
    o_l = jnp.moveaxis(o_l, 0, 2).reshape(bsz, h, n_lat, dh)
    return jnp.concatenate([o_c, o_l], axis=2)


def retention_group(q, k, v, g, gn_g):
    f32 = jnp.float32
    dh = q.shape[-1]
    q = jnp.concatenate([q[:, :, :CTX_LEN], rope_2d(q[:, :, CTX_LEN:])], axis=2)
    k = jnp.concatenate([k[:, :, :CTX_LEN], rope_2d(k[:, :, CTX_LEN:])], axis=2) * (dh ** -0.5)
    log_gamma = jnp.log(1.0 - 2.0 ** (-5.0 - jnp.arange(N_HEADS, dtype=f32)))
    lg_fwd = log_gamma[None, :, None, None]
    lg_bwd = log_gamma[::-1][None, :, None, None]
    o = (chunk_decay_attn(q, k, v, lg_fwd, RET_CHUNK)
         + seg_flip(chunk_decay_attn(seg_flip(q, 2), seg_flip(k, 2), seg_flip(v, 2), lg_bwd, RET_CHUNK), 2))
    return (jax.nn.silu(g.astype(f32)) * head_norm(o, gn_g, True)).astype(g.dtype)


def hgrn2_group(f_fwd, f_bwd, i, q, g, lb, on_g):
    f32 = jnp.float32

    def gate(fz, lbd):
        lbh = lbd.astype(f32).reshape(N_HEADS, -1)[None, :, None, :]
        log_f = jnp.logaddexp(jnp.log(lbh), jnp.log1p(-lbh) + jax.nn.log_sigmoid(fz.astype(f32)))
        return log_f, -jnp.expm1(log_f)

    lf_f, k_f = gate(f_fwd, lb[0])
    lf_b, k_b = gate(f_bwd, lb[1])
    o_f = chunk_decay_attn(q, k_f, i, lf_f, HGRN_CHUNK)
    o_b = seg_flip(chunk_decay_attn(seg_flip(q, 2), seg_flip(k_b, 2), seg_flip(i, 2),
                                    seg_flip(lf_b, 2), HGRN_CHUNK), 2)
    return (jax.nn.silu(g.astype(f32)) * head_norm(o_f + o_b, on_g, False)).astype(g.dtype)


def hybrid_mixer(u, w_in, conv_w, conv_b, wa, ba, wx, bx, lam, qn_g, kn_g, rpb, gn_g, lb, on_g):
    p = u @ w_in
    (lx, lg, nq, nk, nv, rq, rk, rv, rg, hff, hfb, hi, hq, hg) = jnp.split(p, N_IN_SPLITS, axis=-1)
    y_a = rglru_group(lx, lg, conv_w, conv_b, wa, ba, wx, bx, lam)
    y_b = merge_heads(na_group(to_heads(nq), to_heads(nk), to_heads(nv), qn_g, kn_g, rpb))
    y_c = retention_group(to_heads(rq), to_heads(rk), to_heads(rv), rg, gn_g)
    y_d = hgrn2_group(to_heads(hff), to_heads(hfb), to_heads(hi), to_heads(hq), hg, lb, on_g)
    return jnp.concatenate([y_a, y_b, y_c, y_d], axis=-1)


def peer_ffn(z, w_q, sub_keys, u_emb, v_emb):
    bsz, l, d = z.shape
    t = bsz * l
    zt = z.reshape(t, d)
    qry = (zt @ w_q).reshape(t, PEER_HEADS, 2, PEER_QDIM // 2)
    s = jnp.einsum('thpd,hpkd->thpk', qry, sub_keys).astype(jnp.float32)
    s1, i1 = lax.top_k(s[:, :, 0], PEER_TOPK)
    s2, i2 = lax.top_k(s[:, :, 1], PEER_TOPK)
    cand_s = (s1[..., :, None] + s2[..., None, :]).reshape(t, PEER_HEADS, PEER_TOPK * PEER_TOPK)
    cand_i = (i1[..., :, None] * PEER_NKEYS + i2[..., None, :]).reshape(t, PEER_HEADS, PEER_TOPK * PEER_TOPK)
    top_s, pos = lax.top_k(cand_s, PEER_TOPK)
    idx = jnp.take_along_axis(cand_i, pos, axis=-1).reshape(t, PEER_HEADS * PEER_TOPK)
    gate = jax.nn.softmax(top_s, axis=-1).reshape(t, PEER_HEADS * PEER_TOPK)
    n_blk = t // PEER_TOKEN_BLOCK

    def block(args):
        zb, ib, gb = args
        up = jnp.take(u_emb, ib, axis=0)
        down = jnp.take(v_emb, ib, axis=0)
        act = jax.nn.gelu(jnp.einsum('td,ted->te', zb, up).astype(jnp.float32))
        return jnp.einsum('te,ted->td', (gb * act).astype(z.dtype), down)

    y = lax.map(block, (zt.reshape(n_blk, PEER_TOKEN_BLOCK, d),
                        idx.reshape(n_blk, PEER_TOKEN_BLOCK, -1),
                        gate.reshape(n_blk, PEER_TOKEN_BLOCK, -1)))
    return y.reshape(bsz, l, d)


def setup_inputs(seed: int = 0) -> dict:
    key = jax.random.key(seed)
    ks = jax.random.split(key, 32)
    f32 = jnp.float32
    d = D_MODEL
    gw = GROUP_W
    bs = GROUP_W // N_HEADS

    def nrm(k, shape, s):
        return jax.random.normal(k, shape, f32) * s

    a8 = jax.random.uniform(ks[16], (DEPTH, 2, gw), f32, 0.9, 0.999)
    a = a8 ** (1.0 / LRU_C)
    return {
        'x': nrm(ks[0], (BATCH, SEQ, d), 1.0),
        'c': nrm(ks[1], (BATCH, d), 1.0),
        'ctx': nrm(ks[2], (BATCH, CTX_LEN, d), 1.0),
        'c_ctx': nrm(ks[3], (d,), 1.0),
        'norm1_g': 1.0 + nrm(ks[4], (DEPTH, d), 0.01),
        'norm2_g': 1.0 + nrm(ks[5], (DEPTH, d), 0.01),
        'ada_w': nrm(ks[6], (DEPTH, d, N_MOD * d), 0.5 * d ** -0.5),
        'ada_b': nrm(ks[7], (DEPTH, N_MOD * d), 0.01),
        'w_in': nrm(ks[8], (DEPTH, d, N_IN_SPLITS * gw), d ** -0.5),
        'w_out': nrm(ks[9], (DEPTH, d, d), d ** -0.5),
        'lru_conv_w': nrm(ks[10], (DEPTH, LRU_CONV_W, gw), LRU_CONV_W ** -0.5),
        'lru_conv_b': nrm(ks[11], (DEPTH, gw), 0.01),
        'lru_wa': nrm(ks[12], (DEPTH, 2, N_HEADS, bs, bs), bs ** -0.5),
        'lru_ba': nrm(ks[13], (DEPTH, 2, gw), 0.01),
        'lru_wx': nrm(ks[14], (DEPTH, 2, N_HEADS, bs, bs), bs ** -0.5),
        'lru_bx': nrm(ks[15], (DEPTH, 2, gw), 0.01),
        'lru_lam': jnp.log(a) - jnp.log1p(-a),
        'na_qn_g': 1.0 + nrm(ks[17], (DEPTH, HEAD_DIM), 0.01),
        'na_kn_g': 1.0 + nrm(ks[18], (DEPTH, HEAD_DIM), 0.01),
        'na_rpb': nrm(ks[19], (DEPTH, N_HEADS, 2 * NA_WIN_ROWS - 1, 2 * NA_WIN_COLS - 1), 0.02),
        'ret_gn_g': 1.0 + nrm(ks[20], (DEPTH, gw), 0.01),
        'hgrn_lb_logits': nrm(ks[21], (DEPTH, 2, gw), 0.5),
        'hgrn_on_g': 1.0 + nrm(ks[22], (DEPTH, gw), 0.01),
        'peer_wq': nrm(ks[23], (DEPTH, d, PEER_HEADS * PEER_QDIM), d ** -0.5),
        'peer_subkeys': nrm(ks[24], (DEPTH, PEER_HEADS, 2, PEER_NKEYS, PEER_QDIM // 2), (PEER_QDIM // 2) ** -0.5),
        'peer_u': nrm(ks[25], (DEPTH, PEER_N_EXPERTS, d), d ** -0.5),
        'peer_v': nrm(ks[26], (DEPTH, PEER_N_EXPERTS, d), 1.0),
    }


def reference(x, c, ctx, c_ctx, norm1_g, norm2_g, ada_w, ada_b, w_in, w_out, lru_conv_w, lru_conv_b,
              lru_wa, lru_ba, lru_wx, lru_bx, lru_lam, na_qn_g, na_kn_g, na_rpb, ret_gn_g,
              hgrn_lb_logits, hgrn_on_g, peer_wq, peer_subkeys, peer_u, peer_v):
    lb_cum = jnp.cumsum(jax.nn.softmax(hgrn_lb_logits.astype(jnp.float32), axis=0), axis=0)
    hgrn_lb = lb_cum - lb_cum[0:1]
    bsz = x.shape[0]
    h = jnp.concatenate([ctx, x], axis=1)
    c_act = jax.nn.silu(c)
    cc_act = jax.nn.silu(c_ctx)
    for layer in range(DEPTH):
        m_lat = (c_act @ ada_w[layer] + ada_b[layer]).reshape(bsz, N_MOD, D_MODEL)[:, :, None, :]
        m_ctx = (cc_act @ ada_w[layer] + ada_b[layer]).reshape(N_MOD, D_MODEL)
        n_ctx = CTX_LEN
        hn = rmsnorm(h, norm1_g[layer])
        u = split_apply(hn, n_ctx,
                        lambda z: z * (1.0 + m_ctx[1]) + m_ctx[0],
                        lambda z: z * (1.0 + m_lat[:, 1]) + m_lat[:, 0])
        y = hybrid_mixer(u, w_in[layer], lru_conv_w[layer], lru_conv_b[layer], lru_wa[layer], lru_ba[layer],
                         lru_wx[layer], lru_bx[layer], lru_lam[layer], na_qn_g[layer], na_kn_g[layer],
                         na_rpb[layer], ret_gn_g[layer], hgrn_lb[layer], hgrn_on_g[layer])
        if layer == DEPTH - 1:
            n_ctx = 0
            h = h[:, CTX_LEN:]
            y = y[:, CTX_LEN:]
        y = y @ w_out[layer]
        h = h + split_apply(y, n_ctx, lambda z: m_ctx[2] * z, lambda z: m_lat[:, 2] * z)
        hn = rmsnorm(h, norm2_g[layer])
        u = split_apply(hn, n_ctx,
                        lambda z: z * (1.0 + m_ctx[4]) + m_ctx[3],
                        lambda z: z * (1.0 + m_lat[:, 4]) + m_lat[:, 3])
        f = peer_ffn(u, peer_wq[layer], peer_subkeys[layer], peer_u[layer], peer_v[layer])
        h = h + split_apply(f, n_ctx, lambda z: m_ctx[5] * z, lambda z: m_lat[:, 5] * z)
    return h
```

```python
import functools

import jax
import jax.numpy as jnp
from jax import lax
from jax.experimental import pallas as pl
from jax.experimental.pallas import tpu as pltpu

F32 = jnp.float32
BF16 = jnp.bfloat16

N_GROUPS = 4
N_HEADS = 8
HEAD_DIM = 128
N_IN_SPLITS = 14
N_MOD = 6
CTX_LEN = 256
GRID_W = 64
LRU_CONV_W = 4
LRU_C = 8.0
NA_WIN_ROWS = 8
NA_WIN_COLS = 16
RET_CHUNK = 128
HGRN_CHUNK = 32
PEER_HEADS = 8
PEER_NKEYS = 128
PEER_TOPK = 16
ROPE_BASE = 10000.0
EPS = 1e-6
NEG_INF = -1e30

SUBLANES = 8
LANES = 128
ROW_TILE = 256
ATTN_BLOCK = 128
MOD_ROWS = 16
VMEM_LIMIT = 56 * 1024 * 1024


def _cparams(sem, vmem=None):
    return pltpu.CompilerParams(dimension_semantics=sem, vmem_limit_bytes=vmem)


def _bdot(a, b):
    return jnp.dot(a.astype(BF16), b.astype(BF16), preferred_element_type=F32)


def _bdot_nt(a, b):
    return lax.dot_general(a.astype(BF16), b.astype(BF16), (((1,), (1,)), ((), ())),
                           preferred_element_type=F32)


def _bdot_tn(a, b):
    return lax.dot_general(a.astype(BF16), b.astype(BF16), (((0,), (0,)), ((), ())),
                           preferred_element_type=F32)


def _gelu(x):
    return 0.5 * x * (1.0 + jnp.tanh(0.7978845608028654 * (x + 0.044715 * (x * x * x))))


def _sigmoid(x):
    return 1.0 / (1.0 + jnp.exp(-x))


def _silu(x):
    return x * _sigmoid(x)


def _ada_kernel(c_ref, w_ref, b_ref, o_ref):
    x = c_ref[...]
    x = _silu(x)
    hi = x.astype(BF16)
    lo = (x - hi.astype(F32)).astype(BF16)
    w = w_ref[...]
    whi = w.astype(BF16)
    wlo = (w - whi.astype(F32)).astype(BF16)
    acc = jnp.dot(hi, whi, preferred_element_type=F32)
    acc += jnp.dot(hi, wlo, preferred_element_type=F32)
    acc += jnp.dot(lo, whi, preferred_element_type=F32)
    o_ref[...] = acc + b_ref[...]


def _ada_modulation(c16, w, b):
    d, n = w.shape
    tn = 512
    return pl.pallas_call(
        _ada_kernel,
        out_shape=jax.ShapeDtypeStruct((MOD_ROWS, n), F32),
        grid=(n // tn,),
        in_specs=[pl.BlockSpec((MOD_ROWS, d), lambda j: (0, 0)),
                  pl.BlockSpec((d, tn), lambda j: (0, j)),
                  pl.BlockSpec((1, tn), lambda j: (0, j))],
        out_specs=pl.BlockSpec((MOD_ROWS, tn), lambda j: (0, j)),
        compiler_params=_cparams(("arbitrary",), VMEM_LIMIT),
        name="ada_modulation",
    )(c16, w, b.reshape(1, n))


def _resid_norm_kernel(*refs, has_f, has_norm, transpose_out):
    it = iter(refs)
    h_ref = next(it)
    if has_f:
        ft_ref = next(it)
        gate_ref = next(it)
    if has_norm:
        g_ref = next(it)
        sh_ref = next(it)
        sc_ref = next(it)
    if has_f:
        hn_ref = next(it)
    if has_norm:
        u_ref = next(it)
    x = h_ref[...]
    if has_f:
        x = x + gate_ref[0] * ft_ref[...].T
        hn_ref[...] = x
    if has_norm:
        ms = jnp.mean(x * x, axis=-1, keepdims=True)
        y = x * lax.rsqrt(ms + EPS) * g_ref[...]
        y = y * (1.0 + sc_ref[0]) + sh_ref[0]
        if transpose_out:
            u_ref[...] = y.T.astype(BF16)
        else:
            u_ref[...] = y.astype(BF16)


def _resid_norm(h, mod3, row_fn, *, ft=None, gate_mod3=None, gate_k=None, norm_g=None, shift_k=None,
                scale_k=None, transpose_out=False):
    t, d = h.shape
    nt = t // ROW_TILE
    has_f = ft is not None
    has_norm = norm_g is not None
    in_specs = [pl.BlockSpec((ROW_TILE, d), lambda i: (i, 0))]
    args = [h]
    if has_f:
        in_specs += [pl.BlockSpec((d, ROW_TILE), lambda i: (0, i)),
                     pl.BlockSpec((1, 1, d), lambda i: (row_fn(i) * N_MOD + gate_k, 0, 0))]
        args += [ft, mod3 if gate_mod3 is None else gate_mod3]
    if has_norm:
        in_specs += [pl.BlockSpec((1, d), lambda i: (0, 0)),
                     pl.BlockSpec((1, 1, d), lambda i: (row_fn(i) * N_MOD + shift_k, 0, 0)),
                     pl.BlockSpec((1, 1, d), lambda i: (row_fn(i) * N_MOD + scale_k, 0, 0))]
        args += [norm_g.reshape(1, d), mod3, mod3]
    out_shape, out_specs = [], []
    if has_f:
        out_shape.append(jax.ShapeDtypeStruct((t, d), F32))
        out_specs.append(pl.BlockSpec((ROW_TILE, d), lambda i: (i, 0)))
    if has_norm:
        if transpose_out:
            out_shape.append(jax.ShapeDtypeStruct((d, t), BF16))
            out_specs.append(pl.BlockSpec((d, ROW_TILE), lambda i: (0, i)))
        else:
            out_shape.append(jax.ShapeDtypeStruct((t, d), BF16))
            out_specs.append(pl.BlockSpec((ROW_TILE, d), lambda i: (i, 0)))
    outs = pl.pallas_call(
        functools.partial(_resid_norm_kernel, has_f=has_f, has_norm=has_norm, transpose_out=transpose_out),
        out_shape=out_shape, grid=(nt,), in_specs=in_specs, out_specs=out_specs,
        compiler_params=_cparams(("arbitrary",), VMEM_LIMIT),
        name="resid_norm",
    )(*args)
    return outs


def _mm_kernel(x_ref, w_ref, o_ref):
    o_ref[...] = jnp.dot(x_ref[...], w_ref[...], preferred_element_type=F32)


def _matmul(x, w, tm, tn):
    m, k = x.shape
    n = w.shape[1]
    return pl.pallas_call(
        _mm_kernel,
        out_shape=jax.ShapeDtypeStruct((m, n), F32),
        grid=(n // tn, m // tm),
        in_specs=[pl.BlockSpec((tm, k), lambda j, i: (i, 0)),
                  pl.BlockSpec((k, tn), lambda j, i: (0, j))],
        out_specs=pl.BlockSpec((tm, tn), lambda j, i: (i, j)),
        compiler_params=_cparams(("arbitrary", "arbitrary"), VMEM_LIMIT),
        name="in_proj",
    )(x, w)


def _wout_kernel(ya_ref, yb_ref, yc_ref, yd_ref, w_ref, h_ref, gate_ref, o_ref):
    acc = jnp.dot(ya_ref[0], w_ref[0], preferred_element_type=F32)
    acc += jnp.dot(yb_ref[0], w_ref[1], preferred_element_type=F32)
    acc += jnp.dot(yc_ref[0], w_ref[2], preferred_element_type=F32)
    acc += jnp.dot(yd_ref[0], w_ref[3], preferred_element_type=F32)
    o_ref[0] = h_ref[0] + gate_ref[0] * acc


def _out_proj(ys, w4, h3, mod3, *, row_off, row_fn, gate_k):
    b, l, gw = ys[0].shape
    d = w4.shape[2]
    nrt = l // ROW_TILE - row_off
    tn = 1024
    y_spec = pl.BlockSpec((1, ROW_TILE, gw), lambda j, bb, i: (bb, i + row_off, 0))
    return pl.pallas_call(
        _wout_kernel,
        out_shape=jax.ShapeDtypeStruct((b, nrt * ROW_TILE, d), F32),
        grid=(d // tn, b, nrt),
        in_specs=[y_spec, y_spec, y_spec, y_spec,
                  pl.BlockSpec((N_GROUPS, gw, tn), lambda j, bb, i: (0, 0, j)),
                  pl.BlockSpec((1, ROW_TILE, tn), lambda j, bb, i: (bb, i + row_off, j)),
                  pl.BlockSpec((1, 1, tn), lambda j, bb, i: (row_fn(bb, i) * N_MOD + gate_k, 0, j))],
        out_specs=pl.BlockSpec((1, ROW_TILE, tn), lambda j, bb, i: (bb, i, j)),
        compiler_params=_cparams(("arbitrary", "arbitrary", "arbitrary"), VMEM_LIMIT),
        name="out_proj",
    )(*ys, w4, h3, mod3)


def _group_cumsum(x, rowmod, c, reverse):
    n = x.shape[0]
    k = 1
    while k < c:
        if reverse:
            x = x + jnp.where(rowmod < c - k, pltpu.roll(x, n - k, 0), 0.0)
        else:
            x = x + jnp.where(rowmod >= k, pltpu.roll(x, k, 0), 0.0)
        k *= 2
    return x


def _group_linear_scan(a, u, rowmod, c, reverse):
    n = a.shape[0]
    k = 1
    while k < c:
        if reverse:
            ok = rowmod < c - k
            a_s = pltpu.roll(a, n - k, 0)
            u_s = pltpu.roll(u, n - k, 0)
        else:
            ok = rowmod >= k
            a_s = pltpu.roll(a, k, 0)
            u_s = pltpu.roll(u, k, 0)
        u = jnp.where(ok, a * u_s + u, u)
        a = jnp.where(ok, a * a_s, a)
        k *= 2
    return a, u


def _rglru_kernel(x_ref, g_ref, cw_ref, cb_ref, wa_ref, ba_ref, wx_ref, bx_ref, lam_ref, y_ref,
                  xp_ref, af_ref, uf_ref, ab_ref, ub_ref, *, seq_len):
    l = seq_len
    pad = SUBLANES
    nblk = l // ROW_TILE
    zeros_pad = jnp.zeros((pad, LANES), F32)
    xp_ref[0:pad, :] = zeros_pad
    xp_ref[pad + l:pad + l + pad, :] = zeros_pad
    xp_ref[pad:pad + l, :] = x_ref[0]

    row = lax.broadcasted_iota(jnp.int32, (ROW_TILE, LANES), 0)
    rowmod = row % SUBLANES
    cw = cw_ref[...]
    cb = cb_ref[...]
    lam = lam_ref[...]
    sp = [jnp.maximum(-lam[dd], 0.0) + jnp.log1p(jnp.exp(-jnp.abs(lam[dd]))) for dd in range(2)]

    for blk in range(nblk):
        s0 = blk * ROW_TILE
        taps = []
        for tap in range(LRU_CONV_W):
            off = tap - LRU_CONV_W // 2
            v = xp_ref[pad + s0 + off:pad + s0 + off + ROW_TILE, :]
            if blk == 0 and off > 0:
                v = jnp.where(row + off < CTX_LEN, v, 0.0)
            if s0 == CTX_LEN and off < 0:
                v = jnp.where(row + off >= 0, v, 0.0)
            taps.append(v)
        xc = cb + taps[0] * cw[0:1]
        for tap in range(1, LRU_CONV_W):
            xc = xc + taps[tap] * cw[tap:tap + 1]
        xcb = xc.astype(BF16)
        for dd, (a_ref, u_ref) in enumerate(((af_ref, uf_ref), (ab_ref, ub_ref))):
            r = _sigmoid(jnp.dot(xcb, wa_ref[dd, 0].astype(BF16), preferred_element_type=F32) + ba_ref[dd])
            i = _sigmoid(jnp.dot(xcb, wx_ref[dd, 0].astype(BF16), preferred_element_type=F32) + bx_ref[dd])
            log_a = -LRU_C * r * sp[dd]
            a = jnp.exp(log_a)
            u = jnp.sqrt(1.0 - jnp.exp(2.0 * log_a)) * (i * xc)
            a_loc, u_loc = _group_linear_scan(a, u, rowmod, SUBLANES, reverse=(dd == 1))
            a_ref[s0:s0 + ROW_TILE, :] = a_loc
            u_ref[s0:s0 + ROW_TILE, :] = u_loc

    n_chunks = l // SUBLANES
    n_ctx_chunks = CTX_LEN // SUBLANES

    def fwd_body(c, carry):
        r0 = pl.multiple_of(c * SUBLANES, SUBLANES)
        hh = af_ref[pl.ds(r0, SUBLANES), :] * carry + uf_ref[pl.ds(r0, SUBLANES), :]
        uf_ref[pl.ds(r0, SUBLANES), :] = hh
        return jnp.broadcast_to(hh[SUBLANES - 1:SUBLANES, :], (SUBLANES, LANES))

    lax.fori_loop(0, n_chunks, fwd_body, jnp.zeros((SUBLANES, LANES), F32))

    def bwd_body(j, carry):
        c = jnp.where(j < n_ctx_chunks, n_ctx_chunks - 1 - j, n_chunks - 1 - (j - n_ctx_chunks))
        r0 = pl.multiple_of(c * SUBLANES, SUBLANES)
        hh = ab_ref[pl.ds(r0, SUBLANES), :] * carry + ub_ref[pl.ds(r0, SUBLANES), :]
        ub_ref[pl.ds(r0, SUBLANES), :] = hh
        return jnp.broadcast_to(hh[0:1, :], (SUBLANES, LANES))

    lax.fori_loop(0, n_chunks, bwd_body, jnp.zeros((SUBLANES, LANES), F32))

    for blk in range(nblk):
        s0 = blk * ROW_TILE
        hsum = uf_ref[s0:s0 + ROW_TILE, :] + ub_ref[s0:s0 + ROW_TILE, :]
        y_ref[0, s0:s0 + ROW_TILE, :] = (_gelu(g_ref[0, s0:s0 + ROW_TILE, :]) * hsum).astype(BF16)


def _rglru(p3, conv_w, conv_b, wa, ba, wx, bx, lam):
    b, l, _ = p3.shape
    gw = N_HEADS * HEAD_DIM
    vec = lambda a: a.reshape(2, 1, gw)
    spec_vec = pl.BlockSpec((2, 1, HEAD_DIM), lambda bb, h: (0, 0, h))
    spec_w = pl.BlockSpec((2, 1, HEAD_DIM, HEAD_DIM), lambda bb, h: (0, h, 0, 0))
    return pl.pallas_call(
        functools.partial(_rglru_kernel, seq_len=l),
        out_shape=jax.ShapeDtypeStruct((b, l, gw), BF16),
        grid=(b, N_HEADS),
        in_specs=[pl.BlockSpec((1, l, HEAD_DIM), lambda bb, h: (bb, 0, h)),
                  pl.BlockSpec((1, l, HEAD_DIM), lambda bb, h: (bb, 0, N_HEADS + h)),
                  pl.BlockSpec((LRU_CONV_W, HEAD_DIM), lambda bb, h: (0, h)),
                  pl.BlockSpec((1, HEAD_DIM), lambda bb, h: (0, h)),
                  spec_w, spec_vec, spec_w, spec_vec, spec_vec],
        out_specs=pl.BlockSpec((1, l, HEAD_DIM), lambda bb, h: (bb, 0, h)),
        scratch_shapes=[pltpu.VMEM((l + 2 * SUBLANES, LANES), F32)] + [pltpu.VMEM((l, LANES), F32)] * 4,
        compiler_params=_cparams(("arbitrary", "arbitrary"), VMEM_LIMIT),
        name="rglru",
    )(p3, p3, conv_w, conv_b.reshape(1, gw), wa, vec(ba), wx, vec(bx), vec(lam))


def _na_kernel(q_ref, k_ref, v_ref, qg_ref, kg_ref, bias_ref, y_ref, qb_ref, kb_ref, vb_ref, *, seq_len):
    l = seq_len
    scale = HEAD_DIM ** -0.5
    rows = (l - CTX_LEN) // GRID_W
    kr = min(NA_WIN_ROWS, rows)
    n_loc = kr * GRID_W

    def headnorm(z, g):
        return z * lax.rsqrt(jnp.mean(z * z, axis=-1, keepdims=True) + EPS) * g

    for s0 in range(0, l, ROW_TILE):
        qb_ref[s0:s0 + ROW_TILE, :] = headnorm(q_ref[0, s0:s0 + ROW_TILE, :], qg_ref[...]).astype(BF16)
        kb_ref[s0:s0 + ROW_TILE, :] = headnorm(k_ref[0, s0:s0 + ROW_TILE, :], kg_ref[...]).astype(BF16)
        vb_ref[s0:s0 + ROW_TILE, :] = v_ref[0, s0:s0 + ROW_TILE, :].astype(BF16)

    k_ctx = kb_ref[0:CTX_LEN, :]
    v_ctx = vb_ref[0:CTX_LEN, :]

    s_c = _bdot_nt(qb_ref[0:CTX_LEN, :], k_ctx) * scale
    e_c = jnp.exp(s_c - jnp.max(s_c, axis=-1, keepdims=True))
    p_c = e_c / jnp.sum(e_c, axis=-1, keepdims=True)
    y_ref[0, 0:CTX_LEN, :] = _bdot(p_c, v_ctx).astype(BF16)

    def row_body(r, carry):
        r0 = jnp.clip(r - kr // 2, 0, rows - kr)
        qs = pl.multiple_of(CTX_LEN + r * GRID_W, GRID_W)
        ks = pl.multiple_of(CTX_LEN + r0 * GRID_W, GRID_W)
        q_r = qb_ref[pl.ds(qs, GRID_W), :]
        k_loc = kb_ref[pl.ds(ks, n_loc), :]
        v_loc = vb_ref[pl.ds(ks, n_loc), :]
        bias = bias_ref[0, r0 - r + NA_WIN_ROWS - 1]
        s_loc = _bdot_nt(q_r, k_loc) * scale + bias
        s_ctx = _bdot_nt(q_r, k_ctx) * scale
        m = jnp.maximum(jnp.max(s_loc, axis=-1, keepdims=True), jnp.max(s_ctx, axis=-1, keepdims=True))
        e_loc = jnp.exp(s_loc - m)
        e_ctx = jnp.exp(s_ctx - m)
        inv = 1.0 / (jnp.sum(e_loc, axis=-1, keepdims=True) + jnp.sum(e_ctx, axis=-1, keepdims=True))
        o = _bdot(e_loc * inv, v_loc) + _bdot(e_ctx * inv, v_ctx)
        y_ref[0, pl.ds(qs, GRID_W), :] = o.astype(BF16)
        return carry

    lax.fori_loop(0, rows, row_body, 0)


def _na_bias_table(rpb, rows):
    kr = min(NA_WIN_ROWS, rows)
    qc = jnp.arange(GRID_W)[:, None]
    kc = jnp.arange(GRID_W)[None, :]
    win0 = jnp.clip(qc - NA_WIN_COLS // 2, 0, GRID_W - NA_WIN_COLS)
    col_ok = (kc >= win0) & (kc < win0 + NA_WIN_COLS)
    rel_c = jnp.clip(kc - qc + NA_WIN_COLS - 1, 0, 2 * NA_WIN_COLS - 2)
    t = jnp.where(col_ok[None, None], rpb.astype(F32)[:, :, rel_c], NEG_INF)
    d = jnp.arange(NA_WIN_ROWS)[:, None] + jnp.arange(kr)[None, :]
    d = jnp.clip(d, 0, 2 * NA_WIN_ROWS - 2)
    tb = t[:, d]
    return tb.transpose(0, 1, 3, 2, 4).reshape(rpb.shape[0], NA_WIN_ROWS, GRID_W, kr * GRID_W)


def _natten(p3, qn_g, kn_g, bias_tab):
    b, l, _ = p3.shape
    gw = N_HEADS * HEAD_DIM
    n_loc = bias_tab.shape[-1]
    spec = lambda split: pl.BlockSpec((1, l, HEAD_DIM), lambda bb, h: (bb, 0, split * N_HEADS + h))
    return pl.pallas_call(
        functools.partial(_na_kernel, seq_len=l),
        out_shape=jax.ShapeDtypeStruct((b, l, gw), BF16),
        grid=(b, N_HEADS),
        in_specs=[spec(2), spec(3), spec(4),
                  pl.BlockSpec((1, HEAD_DIM), lambda bb, h: (0, 0)),
                  pl.BlockSpec((1, HEAD_DIM), lambda bb, h: (0, 0)),
                  pl.BlockSpec((1, NA_WIN_ROWS, GRID_W, n_loc), lambda bb, h: (h, 0, 0, 0))],
        out_specs=pl.BlockSpec((1, l, HEAD_DIM), lambda bb, h: (bb, 0, h)),
        scratch_shapes=[pltpu.VMEM((l, LANES), BF16)] * 3,
        compiler_params=_cparams(("arbitrary", "arbitrary"), VMEM_LIMIT),
        name="natten",
    )(p3, p3, p3, qn_g.reshape(1, HEAD_DIM), kn_g.reshape(1, HEAD_DIM), bias_tab)


def _dir_prep(q, k, lf, rowmod, c, reverse):
    inc = _group_cumsum(lf, rowmod, c, reverse)
    oth = _group_cumsum(lf, rowmod, c, not reverse) - lf
    return q * jnp.exp(inc), k * jnp.exp(-inc), k * jnp.exp(oth), jnp.exp(inc + oth)


def _decay_attn_block(s0, q, k_f, k_b, v, lf_f, lf_b, c, refs):
    (qdf_ref, ksf_ref, ktf_ref, decf_ref, qdb_ref, ksb_ref, ktb_ref, decb_ref, vb_ref, o_ref) = refs
    n = ATTN_BLOCK
    row = lax.broadcasted_iota(jnp.int32, (n, n), 0)
    col = lax.broadcasted_iota(jnp.int32, (n, n), 1)
    rowmod = row % c
    same = (row // c) == (col // c)
    vb = v.astype(BF16)
    vb_ref[pl.ds(s0, n), :] = vb
    o = jnp.zeros((n, HEAD_DIM), F32)
    for reverse, kk, lf, (qd_ref, ks_ref, kt_ref, dec_ref) in (
            (False, k_f, lf_f, (qdf_ref, ksf_ref, ktf_ref, decf_ref)),
            (True, k_b, lf_b, (qdb_ref, ksb_ref, ktb_ref, decb_ref))):
        qd, ks, kt, dec = _dir_prep(q, kk, lf, rowmod, c, reverse)
        qd = qd.astype(BF16)
        ks = ks.astype(BF16)
        qd_ref[pl.ds(s0, n), :] = qd
        ks_ref[pl.ds(s0, n), :] = ks
        kt_ref[pl.ds(s0, n), :] = kt.astype(BF16)
        dec_ref[pl.ds(s0, n), :] = dec
        s = _bdot_nt(qd, ks)
        keep = same & ((col >= row) if reverse else (col <= row))
        o = o + jnp.dot(jnp.where(keep, s, 0.0).astype(BF16), vb, preferred_element_type=F32)
    o_ref[pl.ds(s0, n), :] = o


def _decay_attn_state_pass(c, seq_len, refs):
    (qdf_ref, ksf_ref, ktf_ref, decf_ref, qdb_ref, ksb_ref, ktb_ref, decb_ref, vb_ref, o_ref) = refs
    n_chunks = seq_len // c
    n_ctx_chunks = CTX_LEN // c

    def make_body(qd_ref, kt_ref, dec_ref, order):
        def body(j, state_t):
            r0 = pl.multiple_of(order(j) * c, c)
            qd = qd_ref[pl.ds(r0, c), :]
            o_ref[pl.ds(r0, c), :] += _bdot_nt(qd, state_t)
            dec = dec_ref[pl.ds(r0, 1), :]
            upd = _bdot_tn(vb_ref[pl.ds(r0, c), :], kt_ref[pl.ds(r0, c), :])
            return state_t * dec + upd
        return body

    zero = jnp.zeros((HEAD_DIM, HEAD_DIM), F32)
    lax.fori_loop(0, n_chunks, make_body(qdf_ref, ktf_ref, decf_ref, lambda j: j), zero)
    lax.fori_loop(0, n_chunks, make_body(
        qdb_ref, ktb_ref, decb_ref,
        lambda j: jnp.where(j < n_ctx_chunks, n_ctx_chunks - 1 - j, n_chunks - 1 - (j - n_ctx_chunks))), zero)


def _decay_scratch(l):
    one_dir = [pltpu.VMEM((l, LANES), BF16)] * 3 + [pltpu.VMEM((l, LANES), F32)]
    return one_dir + one_dir + [pltpu.VMEM((l, LANES), BF16), pltpu.VMEM((l, LANES), F32)]


def _retention_kernel(q_ref, k_ref, v_ref, g_ref, cos_ref, sin_ref, lg_ref, gn_ref, y_ref, *scratch, seq_len):
    l = seq_len
    n = ATTN_BLOCK
    lane = lax.broadcasted_iota(jnp.int32, (n, HEAD_DIM), 1)
    first = (lane % (HEAD_DIM // 2)) < (HEAD_DIM // 4)
    lg = lg_ref[0]

    def rope(z, cos, sin):
        swapped = jnp.where(first, pltpu.roll(z, HEAD_DIM - HEAD_DIM // 4, 1), pltpu.roll(z, HEAD_DIM // 4, 1))
        return z * cos + swapped * sin

    def block_body(bi, carry):
        s0 = pl.multiple_of(bi * n, n)
        cos = cos_ref[pl.ds(s0, n), :]
        sin = sin_ref[pl.ds(s0, n), :]
        q = rope(q_ref[0, pl.ds(s0, n), :], cos, sin)
        k = rope(k_ref[0, pl.ds(s0, n), :], cos, sin) * (HEAD_DIM ** -0.5)
        lf_f = jnp.broadcast_to(lg[0:1], (n, HEAD_DIM))
        lf_b = jnp.broadcast_to(lg[1:2], (n, HEAD_DIM))
        _decay_attn_block(s0, q, k, k, v_ref[0, pl.ds(s0, n), :], lf_f, lf_b, RET_CHUNK, scratch)
        return carry

    lax.fori_loop(0, l // n, block_body, 0)
    _decay_attn_state_pass(RET_CHUNK, l, scratch)
    o_ref = scratch[-1]

    def out_body(bi, carry):
        s0 = pl.multiple_of(bi * n, n)
        o = o_ref[pl.ds(s0, n), :]
        o = o - jnp.mean(o, axis=-1, keepdims=True)
        o = o * lax.rsqrt(jnp.mean(o * o, axis=-1, keepdims=True) + EPS) * gn_ref[...]
        y_ref[0, pl.ds(s0, n), :] = (_silu(g_ref[0, pl.ds(s0, n), :]) * o).astype(BF16)
        return carry

    lax.fori_loop(0, l // n, out_body, 0)


def _rope_tables(l):
    n_lat = l - CTX_LEN
    quarter = HEAD_DIM // 4
    t = jnp.arange(n_lat)
    inv_freq = ROPE_BASE ** (-jnp.arange(quarter, dtype=F32) / quarter)
    ang_r = (t // GRID_W).astype(F32)[:, None] * inv_freq[None, :]
    ang_c = (t % GRID_W).astype(F32)[:, None] * inv_freq[None, :]
    cos = jnp.concatenate([jnp.cos(ang_r), jnp.cos(ang_r), jnp.cos(ang_c), jnp.cos(ang_c)], axis=-1)
    sin = jnp.concatenate([-jnp.sin(ang_r), jnp.sin(ang_r), -jnp.sin(ang_c), jnp.sin(ang_c)], axis=-1)
    cos = jnp.concatenate([jnp.ones((CTX_LEN, HEAD_DIM), F32), cos], axis=0)
    sin = jnp.concatenate([jnp.zeros((CTX_LEN, HEAD_DIM), F32), sin], axis=0)
    return cos, sin


def _retention(p3, gn_g, cos, sin):
    b, l, _ = p3.shape
    gw = N_HEADS * HEAD_DIM
    log_gamma = jnp.log(1.0 - 2.0 ** (-5.0 - jnp.arange(N_HEADS, dtype=F32)))
    lg = jnp.stack([log_gamma, log_gamma[::-1]], axis=1)
    lg = jnp.broadcast_to(lg[:, :, None], (N_HEADS, 2, HEAD_DIM))
    spec = lambda split: pl.BlockSpec((1, l, HEAD_DIM), lambda bb, h: (bb, 0, split * N_HEADS + h))
    full = pl.BlockSpec((l, HEAD_DIM), lambda bb, h: (0, 0))
    return pl.pallas_call(
        functools.partial(_retention_kernel, seq_len=l),
        out_shape=jax.ShapeDtypeStruct((b, l, gw), BF16),
        grid=(b, N_HEADS),
        in_specs=[spec(5), spec(6), spec(7), spec(8), full, full,
                  pl.BlockSpec((1, 2, HEAD_DIM), lambda bb, h: (h, 0, 0)),
                  pl.BlockSpec((1, HEAD_DIM), lambda bb, h: (0, h))],
        out_specs=pl.BlockSpec((1, l, HEAD_DIM), lambda bb, h: (bb, 0, h)),
        scratch_shapes=_decay_scratch(l),
        compiler_params=_cparams(("arbitrary", "arbitrary"), VMEM_LIMIT),
        name="retention",
    )(p3, p3, p3, p3, cos, sin, lg, gn_g.reshape(1, gw))


def _hgrn_kernel(ff_ref, fb_ref, i_ref, q_ref, g_ref, lbl_ref, on_ref, y_ref, *scratch, seq_len, layer, depth):
    l = seq_len
    n = ATTN_BLOCK
    logits = lbl_ref[...]
    log_lb, log1m_lb = [], []
    for dd in range(2):
        xs = [logits[ll * 2 + dd] for ll in range(depth)]
        mx = functools.reduce(jnp.maximum, xs)
        es = [jnp.exp(x - mx) for x in xs]
        tot = functools.reduce(lambda a, b_: a + b_, es)
        lb = jnp.zeros_like(tot)
        for ll in range(1, layer + 1):
            lb = lb + es[ll] / tot
        log_lb.append(jnp.log(lb))
        log1m_lb.append(jnp.log1p(-lb))

    def gate(fz, dd):
        log_sig = jnp.minimum(fz, 0.0) - jnp.log1p(jnp.exp(-jnp.abs(fz)))
        a = jnp.broadcast_to(log_lb[dd], fz.shape)
        bv = log1m_lb[dd] + log_sig
        log_f = jnp.maximum(a, bv) + jnp.log1p(jnp.exp(-jnp.abs(a - bv)))
        return log_f, 1.0 - jnp.exp(log_f)

    def block_body(bi, carry):
        s0 = pl.multiple_of(bi * n, n)
        lf_f, k_f = gate(ff_ref[0, pl.ds(s0, n), :], 0)
        lf_b, k_b = gate(fb_ref[0, pl.ds(s0, n), :], 1)
        _decay_attn_block(s0, q_ref[0, pl.ds(s0, n), :], k_f, k_b, i_ref[0, pl.ds(s0, n), :],
                          lf_f, lf_b, HGRN_CHUNK, scratch)
        return carry

    lax.fori_loop(0, l // n, block_body, 0)
    _decay_attn_state_pass(HGRN_CHUNK, l, scratch)
    o_ref = scratch[-1]

    def out_body(bi, carry):
        s0 = pl.multiple_of(bi * n, n)
        o = o_ref[pl.ds(s0, n), :]
        o = o * lax.rsqrt(jnp.mean(o * o, axis=-1, keepdims=True) + EPS) * on_ref[...]
        y_ref[0, pl.ds(s0, n), :] = (_silu(g_ref[0, pl.ds(s0, n), :]) * o).astype(BF16)
        return carry

    lax.fori_loop(0, l // n, out_body, 0)


def _hgrn2(p3, lb_logits, on_g, layer):
    b, l, _ = p3.shape
    gw = N_HEADS * HEAD_DIM
    depth = lb_logits.shape[0]
    spec = lambda split: pl.BlockSpec((1, l, HEAD_DIM), lambda bb, h: (bb, 0, split * N_HEADS + h))
    return pl.pallas_call(
        functools.partial(_hgrn_kernel, seq_len=l, layer=layer, depth=depth),
        out_shape=jax.ShapeDtypeStruct((b, l, gw), BF16),
        grid=(b, N_HEADS),
        in_specs=[spec(9), spec(10), spec(11), spec(12), spec(13),
                  pl.BlockSpec((depth * 2, 1, HEAD_DIM), lambda bb, h: (0, 0, h)),
                  pl.BlockSpec((1, HEAD_DIM), lambda bb, h: (0, h))],
        out_specs=pl.BlockSpec((1, l, HEAD_DIM), lambda bb, h: (bb, 0, h)),
        scratch_shapes=_decay_scratch(l),
        compiler_params=_cparams(("arbitrary", "arbitrary"), VMEM_LIMIT),
        name="hgrn2",
    )(p3, p3, p3, p3, p3, lb_logits.astype(F32).reshape(depth * 2, 1, gw), on_g.reshape(1, gw))


def _topk_rows(x, k):
    r_n = x.shape[0]
    idx = lax.broadcasted_iota(jnp.int32, x.shape, 0)
    rank = jnp.full(x.shape, k, jnp.int32)
    vals = []
    for r in range(k):
        m = jnp.max(x, axis=0, keepdims=True)
        first = jnp.min(jnp.where(x == m, idx, r_n), axis=0, keepdims=True)
        sel = idx == first
        rank = jnp.where(sel, r, rank)
        x = jnp.where(sel, -jnp.inf, x)
        vals.append(m)
    return rank, jnp.concatenate(vals, axis=0)


def _peer_route_kernel(zt_ref, wq_ref, sk_ref, rank2_ref, lim1_ref, e1_ref, e2_ref, *, tb):
    k = PEER_TOPK
    q = jnp.dot(wq_ref[...], zt_ref[...], preferred_element_type=F32)
    half = q.shape[0] // 2
    for c0 in range(0, tb, LANES):
        s1 = _bdot(sk_ref[0, 0], q[:half, c0:c0 + LANES])
        s2 = _bdot(sk_ref[0, 1], q[half:, c0:c0 + LANES])
        rank1, top1 = _topk_rows(s1, k)
        rank2, top2 = _topk_rows(s2, k)
        cand = jnp.concatenate([top1[r:r + 1] + top2 for r in range(k)], axis=0)
        crank, _ = _topk_rows(cand, k)
        sel = crank < k
        e1 = jnp.exp(s1 - top1[0:1])
        e2 = jnp.exp(s2 - top2[0:1])
        e1t = jnp.exp(top1 - top1[0:1])
        e2t = jnp.exp(top2 - top2[0:1])
        ecand = jnp.concatenate([e1t[r:r + 1] * e2t for r in range(k)], axis=0)
        z = jnp.sum(jnp.where(sel, ecand, 0.0), axis=0, keepdims=True)
        lim1 = jnp.zeros(rank1.shape, F32)
        for r in range(k):
            cnt = jnp.sum(sel[r * k:(r + 1) * k].astype(F32), axis=0, keepdims=True)
            lim1 = jnp.where(rank1 == r, cnt, lim1)
        rank2_ref[0, :, c0:c0 + LANES] = rank2.astype(F32)
        lim1_ref[0, :, c0:c0 + LANES] = lim1
        e1_ref[0, :, c0:c0 + LANES] = e1
        e2_ref[0, :, c0:c0 + LANES] = e2 / z


def _peer_route(zt, wq_t, sub_keys, tb):
    d, t = zt.shape
    qd = wq_t.shape[0] // PEER_HEADS
    tab = jax.ShapeDtypeStruct((PEER_HEADS, PEER_NKEYS, t), F32)
    tab_spec = pl.BlockSpec((1, PEER_NKEYS, tb), lambda i, h: (h, 0, i))
    return pl.pallas_call(
        functools.partial(_peer_route_kernel, tb=tb),
        out_shape=[tab] * 4,
        grid=(t // tb, PEER_HEADS),
        in_specs=[pl.BlockSpec((d, tb), lambda i, h: (0, i)),
                  pl.BlockSpec((qd, d), lambda i, h: (h, 0)),
                  pl.BlockSpec((1, 2, PEER_NKEYS, qd // 2), lambda i, h: (h, 0, 0, 0))],
        out_specs=[tab_spec] * 4,
        compiler_params=_cparams(("arbitrary", "arbitrary"), VMEM_LIMIT),
        name="peer_route",
    )(zt, wq_t, sub_keys)


def _peer_expert_kernel(zt_ref, u_ref, vt_ref, rank2_ref, lim1_ref, e1_ref, e2_ref, o_ref, *, ek):
    e = pl.program_id(1)
    act = jnp.dot(u_ref[...], zt_ref[...], preferred_element_type=F32)
    tb = act.shape[1]
    n_i1 = ek // PEER_NKEYS
    ws = []
    for j in range(n_i1):
        i1 = e * n_i1 + j
        g = jnp.zeros((PEER_NKEYS, tb), F32)
        for h in range(PEER_HEADS):
            lim = lim1_ref[h, pl.ds(i1, 1), :]
            w1 = e1_ref[h, pl.ds(i1, 1), :]
            g = g + jnp.where(rank2_ref[h] < lim, e2_ref[h], 0.0) * w1
        a = act[j * PEER_NKEYS:(j + 1) * PEER_NKEYS]
        ws.append((g * _gelu(a)).astype(BF16))
    w = jnp.concatenate(ws, axis=0)
    contrib = jnp.dot(vt_ref[...], w, preferred_element_type=F32)

    @pl.when(e == 0)
    def _():
        o_ref[...] = contrib

    @pl.when(e > 0)
    def _():
        o_ref[...] += contrib


def _peer_experts(zt, u, vt, tabs, tb, ek):
    d, t = zt.shape
    ne = u.shape[0]
    tab_spec = pl.BlockSpec((PEER_HEADS, PEER_NKEYS, tb), lambda i, e: (0, 0, i))
    return pl.pallas_call(
        functools.partial(_peer_expert_kernel, ek=ek),
        out_shape=jax.ShapeDtypeStruct((d, t), F32),
        grid=(t // tb, ne // ek),
        in_specs=[pl.BlockSpec((d, tb), lambda i, e: (0, i)),
                  pl.BlockSpec((ek, d), lambda i, e: (e, 0)),
                  pl.BlockSpec((d, ek), lambda i, e: (0, e)),
                  tab_spec, tab_spec, tab_spec, tab_spec],
        out_specs=pl.BlockSpec((d, tb), lambda i, e: (0, i)),
        compiler_params=_cparams(("arbitrary", "arbitrary"), VMEM_LIMIT),
        name="peer_experts",
    )(zt, u, vt, *tabs)


def kernel(x, c, ctx, c_ctx, norm1_g, norm2_g, ada_w, ada_b, w_in, w_out, lru_conv_w, lru_conv_b, lru_wa, lru_ba,
           lru_wx, lru_bx, lru_lam, na_qn_g, na_kn_g, na_rpb, ret_gn_g, hgrn_lb_logits, hgrn_on_g, peer_wq,
           peer_subkeys, peer_u, peer_v):
    bsz, n_lat, d = x.shape
    depth = w_in.shape[0]
    l = CTX_LEN + n_lat
    gw = d // N_GROUPS
    assert bsz < MOD_ROWS and n_lat % ROW_TILE == 0 and CTX_LEN % ROW_TILE == 0
    ctx_row = bsz

    c16 = jnp.zeros((MOD_ROWS, d), F32).at[:bsz].set(c).at[ctx_row].set(c_ctx)
    cos, sin = _rope_tables(l)
    rows = n_lat // GRID_W

    h3 = jnp.concatenate([ctx, x], axis=1)
    tiles_full = l // ROW_TILE
    tiles_lat = n_lat // ROW_TILE
    ctx_tiles = CTX_LEN // ROW_TILE

    def row_full(i):
        return jnp.where(i % tiles_full < ctx_tiles, ctx_row, i // tiles_full)

    def row_lat(i):
        return i // tiles_lat

    h = h3.reshape(bsz * l, d)
    pending = None
    for layer in range(depth):
        last = layer == depth - 1
        mod3 = _ada_modulation(c16, ada_w[layer], ada_b[layer]).reshape(MOD_ROWS * N_MOD, 1, d)
        if pending is None:
            (u,) = _resid_norm(h, mod3, row_full, norm_g=norm1_g[layer], shift_k=0, scale_k=1)
        else:
            ft, mod3_prev = pending
            h, u = _resid_norm(h, mod3, row_full, ft=ft, gate_mod3=mod3_prev, gate_k=5,
                               norm_g=norm1_g[layer], shift_k=0, scale_k=1)
        p = _matmul(u, w_in[layer].astype(BF16), 512, 1024)
        p3 = p.reshape(bsz, l, N_IN_SPLITS * gw)
        y_a = _rglru(p3, lru_conv_w[layer], lru_conv_b[layer], lru_wa[layer], lru_ba[layer],
                     lru_wx[layer], lru_bx[layer], lru_lam[layer])
        y_b = _natten(p3, na_qn_g[layer], na_kn_g[layer], _na_bias_table(na_rpb[layer], rows))
        y_c = _retention(p3, ret_gn_g[layer], cos, sin)
        y_d = _hgrn2(p3, hgrn_lb_logits, hgrn_on_g[layer], layer)
        w4 = w_out[layer].astype(BF16).reshape(N_GROUPS, gw, d)
        if last:
            hm3 = _out_proj((y_a, y_b, y_c, y_d), w4, h.reshape(bsz, l, d), mod3, row_off=ctx_tiles,
                            row_fn=lambda bb, i: bb, gate_k=2)
            row_fn = row_lat
        else:
            hm3 = _out_proj((y_a, y_b, y_c, y_d), w4, h.reshape(bsz, l, d), mod3, row_off=0,
                            row_fn=lambda bb, i: jnp.where(i < ctx_tiles, ctx_row, bb), gate_k=2)
            row_fn = row_full
        h = hm3.reshape(-1, d)
        (zt,) = _resid_norm(h, mod3, row_fn, norm_g=norm2_g[layer], shift_k=3, scale_k=4, transpose_out=True)
        tabs = _peer_route(zt, peer_wq[layer].T.astype(BF16), peer_subkeys[layer], ROW_TILE)
        ft = _peer_experts(zt, peer_u[layer].astype(BF16), peer_v[layer].T.astype(BF16), tabs, ROW_TILE, 512)
        if last:
            (h,) = _resid_norm(h, mod3, row_fn, ft=ft, gate_k=5)
        else:
            pending = (ft, mod3)
    return h.reshape(bsz, n_lat, d)
```

```python
import functools

import jax
import jax.numpy as jnp
from jax import lax
from jax.experimental import pallas as pl
from jax.experimental.pallas import tpu as pltpu

F32 = jnp.float32
BF16 = jnp.bfloat16

N_GROUPS = 4
N_HEADS = 8
HEAD_DIM = 128
N_IN_SPLITS = 14
N_MOD = 6
CTX_LEN = 256
GRID_W = 64
LRU_CONV_W = 4
LRU_C = 8.0
NA_WIN_ROWS = 8
NA_WIN_COLS = 16
RET_CHUNK = 128
HGRN_CHUNK = 32
PEER_HEADS = 8
PEER_NKEYS = 128
PEER_TOPK = 16
ROPE_BASE = 10000.0
EPS = 1e-6
NEG_INF = -1e30

SUBLANES = 8
LANES = 128
ROW_TILE = 256
ATTN_BLOCK = 128
MOD_ROWS = 16
PEER_TOKEN_BLOCK = 512
PEER_EXPERT_BLOCK = 512
PEER_LANE_CHUNK = 256
VMEM_LIMIT = 56 * 1024 * 1024


def _cparams(sem, vmem=None):
    return pltpu.CompilerParams(dimension_semantics=sem, vmem_limit_bytes=vmem)


def _bdot(a, b):
    return jnp.dot(a.astype(BF16), b.astype(BF16), preferred_element_type=F32)


def _bdot_nt(a, b):
    return lax.dot_general(a.astype(BF16), b.astype(BF16), (((1,), (1,)), ((), ())),
                           preferred_element_type=F32)


def _bdot_tn(a, b):
    return lax.dot_general(a.astype(BF16), b.astype(BF16), (((0,), (0,)), ((), ())),
                           preferred_element_type=F32)


def _gelu(x):
    return 0.5 * x * (1.0 + jnp.tanh(0.7978845608028654 * (x + 0.044715 * (x * x * x))))


def _sigmoid(x):
    return 1.0 / (1.0 + jnp.exp(-x))


def _silu(x):
    return x * _sigmoid(x)


def _ada_kernel(c_ref, w_ref, b_ref, o_ref):
    x = c_ref[...]
    x = _silu(x)
    hi = x.astype(BF16)
    lo = (x - hi.astype(F32)).astype(BF16)
    w = w_ref[0]
    whi = w.astype(BF16)
    wlo = (w - whi.astype(F32)).astype(BF16)
    acc = jnp.dot(hi, whi, preferred_element_type=F32)
    acc += jnp.dot(hi, wlo, preferred_element_type=F32)
    acc += jnp.dot(lo, whi, preferred_element_type=F32)
    o_ref[...] = acc + b_ref[0]


def _ada_modulation(c16, w_all, b_all, layer):
    depth, d, n = w_all.shape
    tn = 512
    return pl.pallas_call(
        _ada_kernel,
        out_shape=jax.ShapeDtypeStruct((MOD_ROWS, n), F32),
        grid=(n // tn,),
        in_specs=[pl.BlockSpec((MOD_ROWS, d), lambda j: (0, 0)),
                  pl.BlockSpec((1, d, tn), lambda j: (layer, 0, j)),
                  pl.BlockSpec((1, 1, tn), lambda j: (layer, 0, j))],
        out_specs=pl.BlockSpec((MOD_ROWS, tn), lambda j: (0, j)),
        compiler_params=_cparams(("arbitrary",), VMEM_LIMIT),
        name="ada_modulation",
    )(c16, w_all, b_all.reshape(depth, 1, n))


def _resid_norm_kernel(*refs, has_f, has_norm, transpose_out):
    it = iter(refs)
    h_ref = next(it)
    if has_f:
        ft_ref = next(it)
        gate_ref = next(it)
    if has_norm:
        g_ref = next(it)
        sh_ref = next(it)
        sc_ref = next(it)
    if has_f:
        hn_ref = next(it)
    if has_norm:
        u_ref = next(it)
    x = h_ref[...]
    if has_f:
        x = x + gate_ref[0] * ft_ref[...].T
        hn_ref[...] = x
    if has_norm:
        ms = jnp.mean(x * x, axis=-1, keepdims=True)
        y = x * lax.rsqrt(ms + EPS) * g_ref[...]
        y = y * (1.0 + sc_ref[0]) + sh_ref[0]
        if transpose_out:
            u_ref[...] = y.T.astype(BF16)
        else:
            u_ref[...] = y.astype(BF16)


def _resid_norm(h, mod3, row_fn, *, ft=None, gate_mod3=None, gate_k=None, norm_g=None, shift_k=None,
                scale_k=None, transpose_out=False):
    t, d = h.shape
    nt = t // ROW_TILE
    has_f = ft is not None
    has_norm = norm_g is not None
    in_specs = [pl.BlockSpec((ROW_TILE, d), lambda i: (i, 0))]
    args = [h]
    if has_f:
        in_specs += [pl.BlockSpec((d, ROW_TILE), lambda i: (0, i)),
                     pl.BlockSpec((1, 1, d), lambda i: (row_fn(i) * N_MOD + gate_k, 0, 0))]
        args += [ft, mod3 if gate_mod3 is None else gate_mod3]
    if has_norm:
        in_specs += [pl.BlockSpec((1, d), lambda i: (0, 0)),
                     pl.BlockSpec((1, 1, d), lambda i: (row_fn(i) * N_MOD + shift_k, 0, 0)),
                     pl.BlockSpec((1, 1, d), lambda i: (row_fn(i) * N_MOD + scale_k, 0, 0))]
        args += [norm_g.reshape(1, d), mod3, mod3]
    out_shape, out_specs = [], []
    if has_f:
        out_shape.append(jax.ShapeDtypeStruct((t, d), F32))
        out_specs.append(pl.BlockSpec((ROW_TILE, d), lambda i: (i, 0)))
    if has_norm:
        if transpose_out:
            out_shape.append(jax.ShapeDtypeStruct((d, t), BF16))
            out_specs.append(pl.BlockSpec((d, ROW_TILE), lambda i: (0, i)))
        else:
            out_shape.append(jax.ShapeDtypeStruct((t, d), BF16))
            out_specs.append(pl.BlockSpec((ROW_TILE, d), lambda i: (i, 0)))
    outs = pl.pallas_call(
        functools.partial(_resid_norm_kernel, has_f=has_f, has_norm=has_norm, transpose_out=transpose_out),
        out_shape=out_shape, grid=(nt,), in_specs=in_specs, out_specs=out_specs,
        compiler_params=_cparams(("arbitrary",), VMEM_LIMIT),
        name="resid_norm",
    )(*args)
    return outs


def _mm_kernel(x_ref, w_ref, o_ref):
    o_ref[...] = jnp.dot(x_ref[...], w_ref[...], preferred_element_type=F32)


def _matmul(x, w, tm, tn):
    m, k = x.shape
    n = w.shape[1]
    return pl.pallas_call(
        _mm_kernel,
        out_shape=jax.ShapeDtypeStruct((m, n), F32),
        grid=(n // tn, m // tm),
        in_specs=[pl.BlockSpec((tm, k), lambda j, i: (i, 0)),
                  pl.BlockSpec((k, tn), lambda j, i: (0, j))],
        out_specs=pl.BlockSpec((tm, tn), lambda j, i: (i, j)),
        compiler_params=_cparams(("arbitrary", "arbitrary"), VMEM_LIMIT),
        name="in_proj",
    )(x, w)


def _wout_kernel(ya_ref, yb_ref, yc_ref, yd_ref, w_ref, h_ref, gate_ref, o_ref):
    acc = jnp.dot(ya_ref[0], w_ref[0], preferred_element_type=F32)
    acc += jnp.dot(yb_ref[0], w_ref[1], preferred_element_type=F32)
    acc += jnp.dot(yc_ref[0], w_ref[2], preferred_element_type=F32)
    acc += jnp.dot(yd_ref[0], w_ref[3], preferred_element_type=F32)
    o_ref[0] = h_ref[0] + gate_ref[0] * acc


def _out_proj(ys, w4, h3, mod3, *, row_off, row_fn, gate_k):
    b, l, gw = ys[0].shape
    d = w4.shape[2]
    nrt = l // ROW_TILE - row_off
    tn = 1024
    y_spec = pl.BlockSpec((1, ROW_TILE, gw), lambda j, bb, i: (bb, i + row_off, 0))
    return pl.pallas_call(
        _wout_kernel,
        out_shape=jax.ShapeDtypeStruct((b, nrt * ROW_TILE, d), F32),
        grid=(d // tn, b, nrt),
        in_specs=[y_spec, y_spec, y_spec, y_spec,
                  pl.BlockSpec((N_GROUPS, gw, tn), lambda j, bb, i: (0, 0, j)),
                  pl.BlockSpec((1, ROW_TILE, tn), lambda j, bb, i: (bb, i + row_off, j)),
                  pl.BlockSpec((1, 1, tn), lambda j, bb, i: (row_fn(bb, i) * N_MOD + gate_k, 0, j))],
        out_specs=pl.BlockSpec((1, ROW_TILE, tn), lambda j, bb, i: (bb, i, j)),
        compiler_params=_cparams(("arbitrary", "arbitrary", "arbitrary"), VMEM_LIMIT),
        name="out_proj",
    )(*ys, w4, h3, mod3)


def _group_cumsum(x, rowmod, c, reverse):
    n = x.shape[0]
    k = 1
    while k < c:
        if reverse:
            x = x + jnp.where(rowmod < c - k, pltpu.roll(x, n - k, 0), 0.0)
        else:
            x = x + jnp.where(rowmod >= k, pltpu.roll(x, k, 0), 0.0)
        k *= 2
    return x


def _group_linear_scan(a, u, rowmod, c, reverse):
    n = a.shape[0]
    k = 1
    while k < c:
        if reverse:
            ok = rowmod < c - k
            a_s = pltpu.roll(a, n - k, 0)
            u_s = pltpu.roll(u, n - k, 0)
        else:
            ok = rowmod >= k
            a_s = pltpu.roll(a, k, 0)
            u_s = pltpu.roll(u, k, 0)
        u = jnp.where(ok, a * u_s + u, u)
        a = jnp.where(ok, a * a_s, a)
        k *= 2
    return a, u


def _rglru_kernel(x_ref, g_ref, cw_ref, cb_ref, wa_ref, ba_ref, wx_ref, bx_ref, lam_ref, y_ref,
                  xp_ref, af_ref, uf_ref, ab_ref, ub_ref, *, seq_len):
    l = seq_len
    pad = SUBLANES
    nblk = l // ROW_TILE
    zeros_pad = jnp.zeros((pad, LANES), F32)
    xp_ref[0:pad, :] = zeros_pad
    xp_ref[pad + l:pad + l + pad, :] = zeros_pad
    xp_ref[pad:pad + l, :] = x_ref[0]

    row = lax.broadcasted_iota(jnp.int32, (ROW_TILE, LANES), 0)
    rowmod = row % SUBLANES
    cw = cw_ref[...]
    cb = cb_ref[...]
    lam = lam_ref[...]
    sp = [jnp.maximum(-lam[dd], 0.0) + jnp.log1p(jnp.exp(-jnp.abs(lam[dd]))) for dd in range(2)]

    for blk in range(nblk):
        s0 = blk * ROW_TILE
        taps = []
        for tap in range(LRU_CONV_W):
            off = tap - LRU_CONV_W // 2
            v = xp_ref[pad + s0 + off:pad + s0 + off + ROW_TILE, :]
            if blk == 0 and off > 0:
                v = jnp.where(row + off < CTX_LEN, v, 0.0)
            if s0 == CTX_LEN and off < 0:
                v = jnp.where(row + off >= 0, v, 0.0)
            taps.append(v)
        xc = cb + taps[0] * cw[0:1]
        for tap in range(1, LRU_CONV_W):
            xc = xc + taps[tap] * cw[tap:tap + 1]
        xcb = xc.astype(BF16)
        for dd, (a_ref, u_ref) in enumerate(((af_ref, uf_ref), (ab_ref, ub_ref))):
            r = _sigmoid(jnp.dot(xcb, wa_ref[dd, 0].astype(BF16), preferred_element_type=F32) + ba_ref[dd])
            i = _sigmoid(jnp.dot(xcb, wx_ref[dd, 0].astype(BF16), preferred_element_type=F32) + bx_ref[dd])
            log_a = -LRU_C * r * sp[dd]
            a = jnp.exp(log_a)
            u = jnp.sqrt(1.0 - jnp.exp(2.0 * log_a)) * (i * xc)
            a_loc, u_loc = _group_linear_scan(a, u, rowmod, SUBLANES, reverse=(dd == 1))
            a_ref[s0:s0 + ROW_TILE, :] = a_loc
            u_ref[s0:s0 + ROW_TILE, :] = u_loc

    n_chunks = l // SUBLANES
    n_ctx_chunks = CTX_LEN // SUBLANES

    def fwd_body(c, carry):
        r0 = pl.multiple_of(c * SUBLANES, SUBLANES)
        hh = af_ref[pl.ds(r0, SUBLANES), :] * carry + uf_ref[pl.ds(r0, SUBLANES), :]
        uf_ref[pl.ds(r0, SUBLANES), :] = hh
        return jnp.broadcast_to(hh[SUBLANES - 1:SUBLANES, :], (SUBLANES, LANES))

    lax.fori_loop(0, n_chunks, fwd_body, jnp.zeros((SUBLANES, LANES), F32))

    def bwd_body(j, carry):
        c = jnp.where(j < n_ctx_chunks, n_ctx_chunks - 1 - j, n_chunks - 1 - (j - n_ctx_chunks))
        r0 = pl.multiple_of(c * SUBLANES, SUBLANES)
        hh = ab_ref[pl.ds(r0, SUBLANES), :] * carry + ub_ref[pl.ds(r0, SUBLANES), :]
        ub_ref[pl.ds(r0, SUBLANES), :] = hh
        return jnp.broadcast_to(hh[0:1, :], (SUBLANES, LANES))

    lax.fori_loop(0, n_chunks, bwd_body, jnp.zeros((SUBLANES, LANES), F32))

    for blk in range(nblk):
        s0 = blk * ROW_TILE
        hsum = uf_ref[s0:s0 + ROW_TILE, :] + ub_ref[s0:s0 + ROW_TILE, :]
        y_ref[0, s0:s0 + ROW_TILE, :] = (_gelu(g_ref[0, s0:s0 + ROW_TILE, :]) * hsum).astype(BF16)


def _rglru(p3, conv_w, conv_b, wa, ba, wx, bx, lam):
    b, l, _ = p3.shape
    gw = N_HEADS * HEAD_DIM
    vec = lambda a: a.reshape(2, 1, gw)
    spec_vec = pl.BlockSpec((2, 1, HEAD_DIM), lambda bb, h: (0, 0, h))
    spec_w = pl.BlockSpec((2, 1, HEAD_DIM, HEAD_DIM), lambda bb, h: (0, h, 0, 0))
    return pl.pallas_call(
        functools.partial(_rglru_kernel, seq_len=l),
        out_shape=jax.ShapeDtypeStruct((b, l, gw), BF16),
        grid=(b, N_HEADS),
        in_specs=[pl.BlockSpec((1, l, HEAD_DIM), lambda bb, h: (bb, 0, h)),
                  pl.BlockSpec((1, l, HEAD_DIM), lambda bb, h: (bb, 0, N_HEADS + h)),
                  pl.BlockSpec((LRU_CONV_W, HEAD_DIM), lambda bb, h: (0, h)),
                  pl.BlockSpec((1, HEAD_DIM), lambda bb, h: (0, h)),
                  spec_w, spec_vec, spec_w, spec_vec, spec_vec],
        out_specs=pl.BlockSpec((1, l, HEAD_DIM), lambda bb, h: (bb, 0, h)),
        scratch_shapes=[pltpu.VMEM((l + 2 * SUBLANES, LANES), F32)] + [pltpu.VMEM((l, LANES), F32)] * 4,
        compiler_params=_cparams(("arbitrary", "arbitrary"), VMEM_LIMIT),
        name="rglru",
    )(p3, p3, conv_w, conv_b.reshape(1, gw), wa, vec(ba), wx, vec(bx), vec(lam))


def _na_kernel(q_ref, k_ref, v_ref, qg_ref, kg_ref, bias_ref, y_ref, qb_ref, kb_ref, vb_ref, *, seq_len):
    l = seq_len
    scale = HEAD_DIM ** -0.5
    rows = (l - CTX_LEN) // GRID_W
    kr = min(NA_WIN_ROWS, rows)
    n_loc = kr * GRID_W

    def headnorm(z, g):
        return z * lax.rsqrt(jnp.mean(z * z, axis=-1, keepdims=True) + EPS) * g

    for s0 in range(0, l, ROW_TILE):
        qb_ref[s0:s0 + ROW_TILE, :] = headnorm(q_ref[0, s0:s0 + ROW_TILE, :], qg_ref[...]).astype(BF16)
        kb_ref[s0:s0 + ROW_TILE, :] = headnorm(k_ref[0, s0:s0 + ROW_TILE, :], kg_ref[...]).astype(BF16)
        vb_ref[s0:s0 + ROW_TILE, :] = v_ref[0, s0:s0 + ROW_TILE, :].astype(BF16)

    k_ctx = kb_ref[0:CTX_LEN, :]
    v_ctx = vb_ref[0:CTX_LEN, :]

    s_c = _bdot_nt(qb_ref[0:CTX_LEN, :], k_ctx) * scale
    e_c = jnp.exp(s_c - jnp.max(s_c, axis=-1, keepdims=True))
    p_c = e_c / jnp.sum(e_c, axis=-1, keepdims=True)
    y_ref[0, 0:CTX_LEN, :] = _bdot(p_c, v_ctx).astype(BF16)

    def row_body(r, carry):
        r0 = jnp.clip(r - kr // 2, 0, rows - kr)
        qs = pl.multiple_of(CTX_LEN + r * GRID_W, GRID_W)
        ks = pl.multiple_of(CTX_LEN + r0 * GRID_W, GRID_W)
        q_r = qb_ref[pl.ds(qs, GRID_W), :]
        k_loc = kb_ref[pl.ds(ks, n_loc), :]
        v_loc = vb_ref[pl.ds(ks, n_loc), :]
        bias = bias_ref[0, r0 - r + NA_WIN_ROWS - 1]
        s_loc = _bdot_nt(q_r, k_loc) * scale + bias
        s_ctx = _bdot_nt(q_r, k_ctx) * scale
        m = jnp.maximum(jnp.max(s_loc, axis=-1, keepdims=True), jnp.max(s_ctx, axis=-1, keepdims=True))
        e_loc = jnp.exp(s_loc - m)
        e_ctx = jnp.exp(s_ctx - m)
        inv = 1.0 / (jnp.sum(e_loc, axis=-1, keepdims=True) + jnp.sum(e_ctx, axis=-1, keepdims=True))
        o = _bdot(e_loc * inv, v_loc) + _bdot(e_ctx * inv, v_ctx)
        y_ref[0, pl.ds(qs, GRID_W), :] = o.astype(BF16)
        return carry

    lax.fori_loop(0, rows, row_body, 0, unroll=2)


def _na_bias_table(rpb, rows):
    kr = min(NA_WIN_ROWS, rows)
    qc = jnp.arange(GRID_W)[:, None]
    kc = jnp.arange(GRID_W)[None, :]
    win0 = jnp.clip(qc - NA_WIN_COLS // 2, 0, GRID_W - NA_WIN_COLS)
    col_ok = (kc >= win0) & (kc < win0 + NA_WIN_COLS)
    rel_c = jnp.clip(kc - qc + NA_WIN_COLS - 1, 0, 2 * NA_WIN_COLS - 2)
    t = jnp.where(col_ok[None, None], rpb.astype(F32)[:, :, rel_c], NEG_INF)
    d = jnp.arange(NA_WIN_ROWS)[:, None] + jnp.arange(kr)[None, :]
    d = jnp.clip(d, 0, 2 * NA_WIN_ROWS - 2)
    tb = t[:, d]
    return tb.transpose(0, 1, 3, 2, 4).reshape(rpb.shape[0], NA_WIN_ROWS, GRID_W, kr * GRID_W)


def _natten(p3, qn_g, kn_g, bias_tab):
    b, l, _ = p3.shape
    gw = N_HEADS * HEAD_DIM
    n_loc = bias_tab.shape[-1]
    spec = lambda split: pl.BlockSpec((1, l, HEAD_DIM), lambda bb, h: (bb, 0, split * N_HEADS + h))
    return pl.pallas_call(
        functools.partial(_na_kernel, seq_len=l),
        out_shape=jax.ShapeDtypeStruct((b, l, gw), BF16),
        grid=(b, N_HEADS),
        in_specs=[spec(2), spec(3), spec(4),
                  pl.BlockSpec((1, HEAD_DIM), lambda bb, h: (0, 0)),
                  pl.BlockSpec((1, HEAD_DIM), lambda bb, h: (0, 0)),
                  pl.BlockSpec((1, NA_WIN_ROWS, GRID_W, n_loc), lambda bb, h: (h, 0, 0, 0))],
        out_specs=pl.BlockSpec((1, l, HEAD_DIM), lambda bb, h: (bb, 0, h)),
        scratch_shapes=[pltpu.VMEM((l, LANES), BF16)] * 3,
        compiler_params=_cparams(("arbitrary", "arbitrary"), VMEM_LIMIT),
        name="natten",
    )(p3, p3, p3, qn_g.reshape(1, HEAD_DIM), kn_g.reshape(1, HEAD_DIM), bias_tab)


def _dir_prep(q, k, lf, rowmod, c, reverse):
    inc = _group_cumsum(lf, rowmod, c, reverse)
    oth = _group_cumsum(lf, rowmod, c, not reverse) - lf
    return q * jnp.exp(inc), k * jnp.exp(-inc), k * jnp.exp(oth), jnp.exp(inc + oth)


def _decay_attn_block(s0, q, k_f, k_b, v, lf_f, lf_b, c, refs):
    (qdf_ref, ksf_ref, ktf_ref, decf_ref, qdb_ref, ksb_ref, ktb_ref, decb_ref, vb_ref, o_ref) = refs
    n = ATTN_BLOCK
    row = lax.broadcasted_iota(jnp.int32, (n, n), 0)
    col = lax.broadcasted_iota(jnp.int32, (n, n), 1)
    rowmod = row % c
    same = (row // c) == (col // c)
    vb = v.astype(BF16)
    vb_ref[pl.ds(s0, n), :] = vb
    o = jnp.zeros((n, HEAD_DIM), F32)
    for reverse, kk, lf, (qd_ref, ks_ref, kt_ref, dec_ref) in (
            (False, k_f, lf_f, (qdf_ref, ksf_ref, ktf_ref, decf_ref)),
            (True, k_b, lf_b, (qdb_ref, ksb_ref, ktb_ref, decb_ref))):
        qd, ks, kt, dec = _dir_prep(q, kk, lf, rowmod, c, reverse)
        qd = qd.astype(BF16)
        ks = ks.astype(BF16)
        qd_ref[pl.ds(s0, n), :] = qd
        ks_ref[pl.ds(s0, n), :] = ks
        kt_ref[pl.ds(s0, n), :] = kt.astype(BF16)
        dec_ref[pl.ds(s0, n), :] = dec
        s = _bdot_nt(qd, ks)
        keep = same & ((col >= row) if reverse else (col <= row))
        o = o + jnp.dot(jnp.where(keep, s, 0.0).astype(BF16), vb, preferred_element_type=F32)
    o_ref[pl.ds(s0, n), :] = o


def _decay_attn_state_pass(c, seq_len, refs):
    (qdf_ref, ksf_ref, ktf_ref, decf_ref, qdb_ref, ksb_ref, ktb_ref, decb_ref, vb_ref, o_ref) = refs
    n_chunks = seq_len // c
    n_ctx_chunks = CTX_LEN // c

    def advance(qd_ref, kt_ref, dec_ref, chunk, state_t):
        r0 = pl.multiple_of(chunk * c, c)
        qd = qd_ref[pl.ds(r0, c), :]
        o_ref[pl.ds(r0, c), :] += _bdot_nt(qd, state_t)
        dec = dec_ref[pl.ds(r0, 1), :]
        upd = _bdot_tn(vb_ref[pl.ds(r0, c), :], kt_ref[pl.ds(r0, c), :])
        return state_t * dec + upd

    def body(j, carry):
        state_f, state_b = carry
        jb = jnp.where(j < n_ctx_chunks, n_ctx_chunks - 1 - j, n_chunks - 1 - (j - n_ctx_chunks))
        return (advance(qdf_ref, ktf_ref, decf_ref, j, state_f),
                advance(qdb_ref, ktb_ref, decb_ref, jb, state_b))

    zero = jnp.zeros((HEAD_DIM, HEAD_DIM), F32)
    lax.fori_loop(0, n_chunks, body, (zero, zero), unroll=4 if n_chunks % 4 == 0 else 2)


def _decay_scratch(l):
    one_dir = [pltpu.VMEM((l, LANES), BF16)] * 3 + [pltpu.VMEM((l, LANES), F32)]
    return one_dir + one_dir + [pltpu.VMEM((l, LANES), BF16), pltpu.VMEM((l, LANES), F32)]


def _retention_kernel(q_ref, k_ref, v_ref, g_ref, cos_ref, sin_ref, lg_ref, gn_ref, y_ref, *scratch, seq_len):
    l = seq_len
    n = ATTN_BLOCK
    lane = lax.broadcasted_iota(jnp.int32, (n, HEAD_DIM), 1)
    first = (lane % (HEAD_DIM // 2)) < (HEAD_DIM // 4)
    lg = lg_ref[0]

    def rope(z, cos, sin):
        swapped = jnp.where(first, pltpu.roll(z, HEAD_DIM - HEAD_DIM // 4, 1), pltpu.roll(z, HEAD_DIM // 4, 1))
        return z * cos + swapped * sin

    def block_body(bi, carry):
        s0 = pl.multiple_of(bi * n, n)
        cos = cos_ref[pl.ds(s0, n), :]
        sin = sin_ref[pl.ds(s0, n), :]
        q = rope(q_ref[0, pl.ds(s0, n), :], cos, sin)
        k = rope(k_ref[0, pl.ds(s0, n), :], cos, sin) * (HEAD_DIM ** -0.5)
        lf_f = jnp.broadcast_to(lg[0:1], (n, HEAD_DIM))
        lf_b = jnp.broadcast_to(lg[1:2], (n, HEAD_DIM))
        _decay_attn_block(s0, q, k, k, v_ref[0, pl.ds(s0, n), :], lf_f, lf_b, RET_CHUNK, scratch)
        return carry

    lax.fori_loop(0, l // n, block_body, 0)
    _decay_attn_state_pass(RET_CHUNK, l, scratch)
    o_ref = scratch[-1]

    def out_body(bi, carry):
        s0 = pl.multiple_of(bi * n, n)
        o = o_ref[pl.ds(s0, n), :]
        o = o - jnp.mean(o, axis=-1, keepdims=True)
        o = o * lax.rsqrt(jnp.mean(o * o, axis=-1, keepdims=True) + EPS) * gn_ref[...]
        y_ref[0, pl.ds(s0, n), :] = (_silu(g_ref[0, pl.ds(s0, n), :]) * o).astype(BF16)
        return carry

    lax.fori_loop(0, l // n, out_body, 0)


def _rope_tables(l):
    n_lat = l - CTX_LEN
    quarter = HEAD_DIM // 4
    t = jnp.arange(n_lat)
    inv_freq = ROPE_BASE ** (-jnp.arange(quarter, dtype=F32) / quarter)
    ang_r = (t // GRID_W).astype(F32)[:, None] * inv_freq[None, :]
    ang_c = (t % GRID_W).astype(F32)[:, None] * inv_freq[None, :]
    cos = jnp.concatenate([jnp.cos(ang_r), jnp.cos(ang_r), jnp.cos(ang_c), jnp.cos(ang_c)], axis=-1)
    sin = jnp.concatenate([-jnp.sin(ang_r), jnp.sin(ang_r), -jnp.sin(ang_c), jnp.sin(ang_c)], axis=-1)
    cos = jnp.concatenate([jnp.ones((CTX_LEN, HEAD_DIM), F32), cos], axis=0)
    sin = jnp.concatenate([jnp.zeros((CTX_LEN, HEAD_DIM), F32), sin], axis=0)
    return cos, sin


def _retention(p3, gn_g, cos, sin):
    b, l, _ = p3.shape
    gw = N_HEADS * HEAD_DIM
    log_gamma = jnp.log(1.0 - 2.0 ** (-5.0 - jnp.arange(N_HEADS, dtype=F32)))
    lg = jnp.stack([log_gamma, log_gamma[::-1]], axis=1)
    lg = jnp.broadcast_to(lg[:, :, None], (N_HEADS, 2, HEAD_DIM))
    spec = lambda split: pl.BlockSpec((1, l, HEAD_DIM), lambda bb, h: (bb, 0, split * N_HEADS + h))
    full = pl.BlockSpec((l, HEAD_DIM), lambda bb, h: (0, 0))
    return pl.pallas_call(
        functools.partial(_retention_kernel, seq_len=l),
        out_shape=jax.ShapeDtypeStruct((b, l, gw), BF16),
        grid=(b, N_HEADS),
        in_specs=[spec(5), spec(6), spec(7), spec(8), full, full,
                  pl.BlockSpec((1, 2, HEAD_DIM), lambda bb, h: (h, 0, 0)),
                  pl.BlockSpec((1, HEAD_DIM), lambda bb, h: (0, h))],
        out_specs=pl.BlockSpec((1, l, HEAD_DIM), lambda bb, h: (bb, 0, h)),
        scratch_shapes=_decay_scratch(l),
        compiler_params=_cparams(("arbitrary", "arbitrary"), VMEM_LIMIT),
        name="retention",
    )(p3, p3, p3, p3, cos, sin, lg, gn_g.reshape(1, gw))


def _hgrn_kernel(ff_ref, fb_ref, i_ref, q_ref, g_ref, lbl_ref, on_ref, y_ref, *scratch, seq_len, layer, depth):
    l = seq_len
    n = ATTN_BLOCK
    logits = lbl_ref[...]
    log_lb, log1m_lb = [], []
    for dd in range(2):
        xs = [logits[ll * 2 + dd] for ll in range(depth)]
        mx = functools.reduce(jnp.maximum, xs)
        es = [jnp.exp(x - mx) for x in xs]
        tot = functools.reduce(lambda a, b_: a + b_, es)
        lb = jnp.zeros_like(tot)
        for ll in range(1, layer + 1):
            lb = lb + es[ll] / tot
        log_lb.append(jnp.log(lb))
        log1m_lb.append(jnp.log1p(-lb))

    def gate(fz, dd):
        log_sig = jnp.minimum(fz, 0.0) - jnp.log1p(jnp.exp(-jnp.abs(fz)))
        a = jnp.broadcast_to(log_lb[dd], fz.shape)
        bv = log1m_lb[dd] + log_sig
        log_f = jnp.maximum(a, bv) + jnp.log1p(jnp.exp(-jnp.abs(a - bv)))
        return log_f, 1.0 - jnp.exp(log_f)

    def block_body(bi, carry):
        s0 = pl.multiple_of(bi * n, n)
        lf_f, k_f = gate(ff_ref[0, pl.ds(s0, n), :], 0)
        lf_b, k_b = gate(fb_ref[0, pl.ds(s0, n), :], 1)
        _decay_attn_block(s0, q_ref[0, pl.ds(s0, n), :], k_f, k_b, i_ref[0, pl.ds(s0, n), :],
                          lf_f, lf_b, HGRN_CHUNK, scratch)
        return carry

    lax.fori_loop(0, l // n, block_body, 0)
    _decay_attn_state_pass(HGRN_CHUNK, l, scratch)
    o_ref = scratch[-1]

    def out_body(bi, carry):
        s0 = pl.multiple_of(bi * n, n)
        o = o_ref[pl.ds(s0, n), :]
        o = o * lax.rsqrt(jnp.mean(o * o, axis=-1, keepdims=True) + EPS) * on_ref[...]
        y_ref[0, pl.ds(s0, n), :] = (_silu(g_ref[0, pl.ds(s0, n), :]) * o).astype(BF16)
        return carry

    lax.fori_loop(0, l // n, out_body, 0)


def _hgrn2(p3, lb_logits, on_g, layer):
    b, l, _ = p3.shape
    gw = N_HEADS * HEAD_DIM
    depth = lb_logits.shape[0]
    spec = lambda split: pl.BlockSpec((1, l, HEAD_DIM), lambda bb, h: (bb, 0, split * N_HEADS + h))
    return pl.pallas_call(
        functools.partial(_hgrn_kernel, seq_len=l, layer=layer, depth=depth),
        out_shape=jax.ShapeDtypeStruct((b, l, gw), BF16),
        grid=(b, N_HEADS),
        in_specs=[spec(9), spec(10), spec(11), spec(12), spec(13),
                  pl.BlockSpec((depth * 2, 1, HEAD_DIM), lambda bb, h: (0, 0, h)),
                  pl.BlockSpec((1, HEAD_DIM), lambda bb, h: (0, h))],
        out_specs=pl.BlockSpec((1, l, HEAD_DIM), lambda bb, h: (bb, 0, h)),
        scratch_shapes=_decay_scratch(l),
        compiler_params=_cparams(("arbitrary", "arbitrary"), VMEM_LIMIT),
        name="hgrn2",
    )(p3, p3, p3, p3, p3, lb_logits.astype(F32).reshape(depth * 2, 1, gw), on_g.reshape(1, gw))


def _topk_rows(x, k):
    r_n = x.shape[0]
    idx = lax.broadcasted_iota(jnp.int32, x.shape, 0)
    rank = jnp.full(x.shape, k, jnp.int32)
    vals = []
    for r in range(k):
        m = jnp.max(x, axis=0, keepdims=True)
        first = jnp.min(jnp.where(x == m, idx, r_n), axis=0, keepdims=True)
        sel = idx == first
        rank = jnp.where(sel, r, rank)
        x = jnp.where(sel, -jnp.inf, x)
        vals.append(m)
    return rank, jnp.concatenate(vals, axis=0)


def _peer_route_kernel(zt_ref, wq_ref, sk_ref, rank2_ref, lim1_ref, e1_ref, e2_ref, *, tb):
    k = PEER_TOPK
    q = jnp.dot(wq_ref[...], zt_ref[...], preferred_element_type=F32)
    half = q.shape[0] // 2
    for c0 in range(0, tb, LANES):
        s1 = _bdot(sk_ref[0, 0], q[:half, c0:c0 + LANES])
        s2 = _bdot(sk_ref[0, 1], q[half:, c0:c0 + LANES])
        rank1, top1 = _topk_rows(s1, k)
        rank2, top2 = _topk_rows(s2, k)
        e1 = jnp.exp(s1 - top1[0:1])
        e2 = jnp.exp(s2 - top2[0:1])
        e1t = jnp.exp(top1 - top1[0:1])
        e2t = jnp.exp(top2 - top2[0:1])
        n_multi = sum(1 for r in range(k) if k // (r + 1) > 1)
        assert k - n_multi == SUBLANES
        pieces, epieces, widths = [], [], []
        for r in range(n_multi):
            n_c = k // (r + 1)
            n_pad = -(-n_c // SUBLANES) * SUBLANES
            valid = lax.broadcasted_iota(jnp.int32, (n_pad, LANES), 0) < n_c
            pieces.append(jnp.where(valid, top1[r:r + 1] + top2[0:n_pad], -jnp.inf))
            epieces.append(e1t[r:r + 1] * e2t[0:n_pad])
            widths.append(n_pad)
        pieces.append(top1[n_multi:k] + top2[0:1])
        epieces.append(e1t[n_multi:k] * e2t[0:1])
        cand = jnp.concatenate(pieces, axis=0)
        ecand = jnp.concatenate(epieces, axis=0)
        crank, _ = _topk_rows(cand, k)
        sel = crank < k
        z = jnp.sum(jnp.where(sel, ecand, 0.0), axis=0, keepdims=True)
        lim1 = jnp.zeros(rank1.shape, F32)
        self32 = sel.astype(F32)
        off = 0
        for r in range(n_multi):
            cnt = jnp.sum(self32[off:off + widths[r]], axis=0, keepdims=True)
            lim1 = jnp.where(rank1 == r, cnt, lim1)
            off += widths[r]
        for r in range(n_multi, k):
            lim1 = jnp.where(rank1 == r, self32[off + r - n_multi:off + r - n_multi + 1], lim1)
        rank2_ref[0, :, c0:c0 + LANES] = rank2.astype(F32)
        lim1_ref[0, :, c0:c0 + LANES] = lim1
        e1_ref[0, :, c0:c0 + LANES] = e1
        e2_ref[0, :, c0:c0 + LANES] = e2 / z


def _peer_route(zt, wq_t, sub_keys, tb):
    d, t = zt.shape
    qd = wq_t.shape[0] // PEER_HEADS
    tab = jax.ShapeDtypeStruct((PEER_HEADS, PEER_NKEYS, t), F32)
    tab_spec = pl.BlockSpec((1, PEER_NKEYS, tb), lambda i, h: (h, 0, i))
    return pl.pallas_call(
        functools.partial(_peer_route_kernel, tb=tb),
        out_shape=[tab] * 4,
        grid=(t // tb, PEER_HEADS),
        in_specs=[pl.BlockSpec((d, tb), lambda i, h: (0, i)),
                  pl.BlockSpec((qd, d), lambda i, h: (h, 0)),
                  pl.BlockSpec((1, 2, PEER_NKEYS, qd // 2), lambda i, h: (h, 0, 0, 0))],
        out_specs=[tab_spec] * 4,
        compiler_params=_cparams(("arbitrary", "arbitrary"), VMEM_LIMIT),
        name="peer_route",
    )(zt, wq_t, sub_keys)


def _peer_expert_kernel(zt_ref, u_ref, vt_ref, rank2_ref, lim1_ref, e1_ref, e2_ref, o_ref,
                        gate_ref, w_ref, *, ek):
    e = pl.program_id(1)
    tb = zt_ref.shape[1]
    n_i1 = ek // PEER_NKEYS

    @pl.when(e == 0)
    def _():
        o_ref[...] = jnp.zeros(o_ref.shape, F32)

    for j in range(n_i1):
        i1 = e * n_i1 + j
        for c0 in range(0, tb, PEER_LANE_CHUNK):
            cs = slice(c0, c0 + PEER_LANE_CHUNK)
            g = jnp.zeros((PEER_NKEYS, PEER_LANE_CHUNK), F32)
            for h in range(PEER_HEADS):
                lim = lim1_ref[h, pl.ds(i1, 1), cs]
                w1 = e1_ref[h, pl.ds(i1, 1), cs]
                g = g + jnp.where(rank2_ref[h, :, cs] < lim, e2_ref[h, :, cs], 0.0) * w1
            gate_ref[j * PEER_NKEYS:(j + 1) * PEER_NKEYS, cs] = g
    act = jnp.dot(u_ref[...], zt_ref[...], preferred_element_type=F32)
    w_ref[...] = (gate_ref[...] * _gelu(act)).astype(BF16)
    o_ref[...] += jnp.dot(vt_ref[...], w_ref[...], preferred_element_type=F32)


def _peer_experts(zt, u, vt, tabs, tb, ek):
    d, t = zt.shape
    ne = u.shape[0]
    once = pl.Buffered(1)
    tab_spec = pl.BlockSpec((PEER_HEADS, PEER_NKEYS, tb), lambda i, e: (0, 0, i), pipeline_mode=once)
    return pl.pallas_call(
        functools.partial(_peer_expert_kernel, ek=ek),
        out_shape=jax.ShapeDtypeStruct((d, t), F32),
        grid=(t // tb, ne // ek),
        in_specs=[pl.BlockSpec((d, tb), lambda i, e: (0, i), pipeline_mode=once),
                  pl.BlockSpec((ek, d), lambda i, e: (e, 0)),
                  pl.BlockSpec((d, ek), lambda i, e: (0, e)),
                  tab_spec, tab_spec, tab_spec, tab_spec],
        out_specs=pl.BlockSpec((d, tb), lambda i, e: (0, i)),
        scratch_shapes=[pltpu.VMEM((ek, tb), F32), pltpu.VMEM((ek, tb), BF16)],
        compiler_params=_cparams(("arbitrary", "arbitrary"), VMEM_LIMIT),
        name="peer_experts",
    )(zt, u, vt, *tabs)


def kernel(x, c, ctx, c_ctx, norm1_g, norm2_g, ada_w, ada_b, w_in, w_out, lru_conv_w, lru_conv_b, lru_wa, lru_ba,
           lru_wx, lru_bx, lru_lam, na_qn_g, na_kn_g, na_rpb, ret_gn_g, hgrn_lb_logits, hgrn_on_g, peer_wq,
           peer_subkeys, peer_u, peer_v):
    bsz, n_lat, d = x.shape
    depth = w_in.shape[0]
    l = CTX_LEN + n_lat
    gw = d // N_GROUPS
    assert bsz < MOD_ROWS and n_lat % ROW_TILE == 0 and CTX_LEN % ROW_TILE == 0
    ctx_row = bsz

    c16 = jnp.zeros((MOD_ROWS, d), F32).at[:bsz].set(c).at[ctx_row].set(c_ctx)
    cos, sin = _rope_tables(l)
    rows = n_lat // GRID_W

    h3 = jnp.concatenate([ctx, x], axis=1)
    tiles_full = l // ROW_TILE
    tiles_lat = n_lat // ROW_TILE
    ctx_tiles = CTX_LEN // ROW_TILE

    def row_full(i):
        return jnp.where(i % tiles_full < ctx_tiles, ctx_row, i // tiles_full)

    def row_lat(i):
        return i // tiles_lat

    h = h3.reshape(bsz * l, d)
    pending = None
    for layer in range(depth):
        last = layer == depth - 1
        mod3 = _ada_modulation(c16, ada_w, ada_b, layer).reshape(MOD_ROWS * N_MOD, 1, d)
        if pending is None:
            (u,) = _resid_norm(h, mod3, row_full, norm_g=norm1_g[layer], shift_k=0, scale_k=1)
        else:
            ft, mod3_prev = pending
            h, u = _resid_norm(h, mod3, row_full, ft=ft, gate_mod3=mod3_prev, gate_k=5,
                               norm_g=norm1_g[layer], shift_k=0, scale_k=1)
        p = _matmul(u, w_in[layer].astype(BF16), 512, 1024)
        p3 = p.reshape(bsz, l, N_IN_SPLITS * gw)
        y_a = _rglru(p3, lru_conv_w[layer], lru_conv_b[layer], lru_wa[layer], lru_ba[layer],
                     lru_wx[layer], lru_bx[layer], lru_lam[layer])
        y_b = _natten(p3, na_qn_g[layer], na_kn_g[layer], _na_bias_table(na_rpb[layer], rows))
        y_c = _retention(p3, ret_gn_g[layer], cos, sin)
        y_d = _hgrn2(p3, hgrn_lb_logits, hgrn_on_g[layer], layer)
        w4 = w_out[layer].astype(BF16).reshape(N_GROUPS, gw, d)
        if last:
            hm3 = _out_proj((y_a, y_b, y_c, y_d), w4, h.reshape(bsz, l, d), mod3, row_off=ctx_tiles,
                            row_fn=lambda bb, i: bb, gate_k=2)
            row_fn = row_lat
        else:
            hm3 = _out_proj((y_a, y_b, y_c, y_d), w4, h.reshape(bsz, l, d), mod3, row_off=0,
                            row_fn=lambda bb, i: jnp.where(i < ctx_tiles, ctx_row, bb), gate_k=2)
            row_fn = row_full
        h = hm3.reshape(-1, d)
        (zt,) = _resid_norm(h, mod3, row_fn, norm_g=norm2_g[layer], shift_k=3, scale_k=4, transpose_out=True)
        tabs = _peer_route(zt, peer_wq[layer].T.astype(BF16), peer_subkeys[layer], ROW_TILE)
        ft = _peer_experts(zt, peer_u[layer].astype(BF16), peer_v[layer].T.astype(BF16), tabs,
                           PEER_TOKEN_BLOCK, PEER_EXPERT_BLOCK)
        if last:
            (h,) = _resid_norm(h, mod3, row_fn, ft=ft, gate_k=5)
        else:
            pending = (ft, mod3)
    return h.reshape(bsz, n_lat, d)
```

```python
import functools

import jax
import jax.numpy as jnp
from jax import lax
from jax.experimental import pallas as pl
from jax.experimental.pallas import tpu as pltpu

F32 = jnp.float32
BF16 = jnp.bfloat16

N_GROUPS = 4
N_HEADS = 8
HEAD_DIM = 128
N_IN_SPLITS = 14
N_MOD = 6
CTX_LEN = 256
GRID_W = 64
LRU_CONV_W = 4
LRU_C = 8.0
NA_WIN_ROWS = 8
NA_WIN_COLS = 16
RET_CHUNK = 128
HGRN_CHUNK = 32
PEER_HEADS = 8
PEER_NKEYS = 128
PEER_TOPK = 16
ROPE_BASE = 10000.0
EPS = 1e-6
NEG_INF = -1e30

SUBLANES = 8
LANES = 128
ROW_TILE = 256
ATTN_BLOCK = 128
MOD_ROWS = 16
PEER_TOKEN_BLOCK = 512
PEER_EXPERT_BLOCK = 512
PEER_LANE_CHUNK = 256
VMEM_LIMIT = 56 * 1024 * 1024


def _cparams(sem, vmem=None):
    return pltpu.CompilerParams(dimension_semantics=sem, vmem_limit_bytes=vmem)


def _bdot(a, b):
    return jnp.dot(a.astype(BF16), b.astype(BF16), preferred_element_type=F32)


def _bdot_nt(a, b):
    return lax.dot_general(a.astype(BF16), b.astype(BF16), (((1,), (1,)), ((), ())),
                           preferred_element_type=F32)


def _bdot_tn(a, b):
    return lax.dot_general(a.astype(BF16), b.astype(BF16), (((0,), (0,)), ((), ())),
                           preferred_element_type=F32)


def _gelu(x):
    return 0.5 * x * (1.0 + jnp.tanh(0.7978845608028654 * (x + 0.044715 * (x * x * x))))


def _sigmoid(x):
    return 1.0 / (1.0 + jnp.exp(-x))


def _silu(x):
    return x * _sigmoid(x)


def _ada_kernel(c_ref, w_ref, b_ref, o_ref):
    x = c_ref[...]
    x = _silu(x)
    hi = x.astype(BF16)
    lo = (x - hi.astype(F32)).astype(BF16)
    w = w_ref[0]
    whi = w.astype(BF16)
    wlo = (w - whi.astype(F32)).astype(BF16)
    acc = jnp.dot(hi, whi, preferred_element_type=F32)
    acc += jnp.dot(hi, wlo, preferred_element_type=F32)
    acc += jnp.dot(lo, whi, preferred_element_type=F32)
    o_ref[...] = acc + b_ref[0]


def _ada_modulation(c16, w_all, b_all, layer):
    depth, d, n = w_all.shape
    tn = 512
    return pl.pallas_call(
        _ada_kernel,
        out_shape=jax.ShapeDtypeStruct((MOD_ROWS, n), F32),
        grid=(n // tn,),
        in_specs=[pl.BlockSpec((MOD_ROWS, d), lambda j: (0, 0)),
                  pl.BlockSpec((1, d, tn), lambda j: (layer, 0, j)),
                  pl.BlockSpec((1, 1, tn), lambda j: (layer, 0, j))],
        out_specs=pl.BlockSpec((MOD_ROWS, tn), lambda j: (0, j)),
        compiler_params=_cparams(("arbitrary",), VMEM_LIMIT),
        name="ada_modulation",
    )(c16, w_all, b_all.reshape(depth, 1, n))


def _resid_norm_kernel(*refs, has_f, has_norm, transpose_out):
    it = iter(refs)
    h_ref = next(it)
    if has_f:
        ft_ref = next(it)
        gate_ref = next(it)
    if has_norm:
        g_ref = next(it)
        sh_ref = next(it)
        sc_ref = next(it)
    if has_f:
        hn_ref = next(it)
    if has_norm:
        u_ref = next(it)
    x = h_ref[...]
    if has_f:
        x = x + gate_ref[0] * ft_ref[...].T
        hn_ref[...] = x
    if has_norm:
        ms = jnp.mean(x * x, axis=-1, keepdims=True)
        y = x * lax.rsqrt(ms + EPS) * g_ref[...]
        y = y * (1.0 + sc_ref[0]) + sh_ref[0]
        if transpose_out:
            u_ref[...] = y.T.astype(BF16)
        else:
            u_ref[...] = y.astype(BF16)


def _resid_norm(h, mod3, row_fn, *, ft=None, gate_mod3=None, gate_k=None, norm_g=None, shift_k=None,
                scale_k=None, transpose_out=False):
    t, d = h.shape
    nt = t // ROW_TILE
    has_f = ft is not None
    has_norm = norm_g is not None
    in_specs = [pl.BlockSpec((ROW_TILE, d), lambda i: (i, 0))]
    args = [h]
    if has_f:
        in_specs += [pl.BlockSpec((d, ROW_TILE), lambda i: (0, i)),
                     pl.BlockSpec((1, 1, d), lambda i: (row_fn(i) * N_MOD + gate_k, 0, 0))]
        args += [ft, mod3 if gate_mod3 is None else gate_mod3]
    if has_norm:
        in_specs += [pl.BlockSpec((1, d), lambda i: (0, 0)),
                     pl.BlockSpec((1, 1, d), lambda i: (row_fn(i) * N_MOD + shift_k, 0, 0)),
                     pl.BlockSpec((1, 1, d), lambda i: (row_fn(i) * N_MOD + scale_k, 0, 0))]
        args += [norm_g.reshape(1, d), mod3, mod3]
    out_shape, out_specs = [], []
    if has_f:
        out_shape.append(jax.ShapeDtypeStruct((t, d), F32))
        out_specs.append(pl.BlockSpec((ROW_TILE, d), lambda i: (i, 0)))
    if has_norm:
        if transpose_out:
            out_shape.append(jax.ShapeDtypeStruct((d, t), BF16))
            out_specs.append(pl.BlockSpec((d, ROW_TILE), lambda i: (0, i)))
        else:
            out_shape.append(jax.ShapeDtypeStruct((t, d), BF16))
            out_specs.append(pl.BlockSpec((ROW_TILE, d), lambda i: (i, 0)))
    outs = pl.pallas_call(
        functools.partial(_resid_norm_kernel, has_f=has_f, has_norm=has_norm, transpose_out=transpose_out),
        out_shape=out_shape, grid=(nt,), in_specs=in_specs, out_specs=out_specs,
        compiler_params=_cparams(("arbitrary",), VMEM_LIMIT),
        name="resid_norm",
    )(*args)
    return outs


def _mm_kernel(x_ref, w_ref, o_ref):
    r = jnp.dot(x_ref[...], w_ref[...], preferred_element_type=F32)
    for hh in range(o_ref.shape[0]):
        o_ref[hh] = r[:, hh * HEAD_DIM:(hh + 1) * HEAD_DIM]


def _matmul(x, w, tm, tn):
    m, k = x.shape
    n = w.shape[1]
    return pl.pallas_call(
        _mm_kernel,
        out_shape=jax.ShapeDtypeStruct((n // HEAD_DIM, m, HEAD_DIM), F32),
        grid=(n // tn, m // tm),
        in_specs=[pl.BlockSpec((tm, k), lambda j, i: (i, 0)),
                  pl.BlockSpec((k, tn), lambda j, i: (0, j))],
        out_specs=pl.BlockSpec((tn // HEAD_DIM, tm, HEAD_DIM), lambda j, i: (j, i, 0)),
        compiler_params=_cparams(("arbitrary", "arbitrary"), VMEM_LIMIT),
        name="in_proj",
    )(x, w)


def _head_slab_spec(bsz, l, split):
    return pl.BlockSpec((1, l, HEAD_DIM), lambda bb, h: ((split * N_HEADS + h) * bsz + bb, 0, 0))


def _wout_kernel(ya_ref, yb_ref, yc_ref, yd_ref, w_ref, h_ref, gate_ref, o_ref):
    acc = jnp.dot(ya_ref[0], w_ref[0], preferred_element_type=F32)
    acc += jnp.dot(yb_ref[0], w_ref[1], preferred_element_type=F32)
    acc += jnp.dot(yc_ref[0], w_ref[2], preferred_element_type=F32)
    acc += jnp.dot(yd_ref[0], w_ref[3], preferred_element_type=F32)
    o_ref[0] = h_ref[0] + gate_ref[0] * acc


def _out_proj(ys, w4, h3, mod3, *, row_off, row_fn, gate_k):
    b, l, gw = ys[0].shape
    d = w4.shape[2]
    nrt = l // ROW_TILE - row_off
    tn = 1024
    y_spec = pl.BlockSpec((1, ROW_TILE, gw), lambda j, bb, i: (bb, i + row_off, 0))
    return pl.pallas_call(
        _wout_kernel,
        out_shape=jax.ShapeDtypeStruct((b, nrt * ROW_TILE, d), F32),
        grid=(d // tn, b, nrt),
        in_specs=[y_spec, y_spec, y_spec, y_spec,
                  pl.BlockSpec((N_GROUPS, gw, tn), lambda j, bb, i: (0, 0, j)),
                  pl.BlockSpec((1, ROW_TILE, tn), lambda j, bb, i: (bb, i + row_off, j)),
                  pl.BlockSpec((1, 1, tn), lambda j, bb, i: (row_fn(bb, i) * N_MOD + gate_k, 0, j))],
        out_specs=pl.BlockSpec((1, ROW_TILE, tn), lambda j, bb, i: (bb, i, j)),
        compiler_params=_cparams(("arbitrary", "arbitrary", "arbitrary"), VMEM_LIMIT),
        name="out_proj",
    )(*ys, w4, h3, mod3)


def _group_cumsum(x, rowmod, c, reverse):
    n = x.shape[0]
    k = 1
    while k < c:
        if reverse:
            x = x + jnp.where(rowmod < c - k, pltpu.roll(x, n - k, 0), 0.0)
        else:
            x = x + jnp.where(rowmod >= k, pltpu.roll(x, k, 0), 0.0)
        k *= 2
    return x


def _group_linear_scan(a, u, rowmod, c, reverse):
    n = a.shape[0]
    k = 1
    while k < c:
        if reverse:
            ok = rowmod < c - k
            a_s = pltpu.roll(a, n - k, 0)
            u_s = pltpu.roll(u, n - k, 0)
        else:
            ok = rowmod >= k
            a_s = pltpu.roll(a, k, 0)
            u_s = pltpu.roll(u, k, 0)
        u = jnp.where(ok, a * u_s + u, u)
        a = jnp.where(ok, a * a_s, a)
        k *= 2
    return a, u


def _rglru_kernel(x_ref, g_ref, cw_ref, cb_ref, wa_ref, ba_ref, wx_ref, bx_ref, lam_ref, y_ref,
                  xp_ref, af_ref, uf_ref, ab_ref, ub_ref, *, seq_len):
    l = seq_len
    pad = SUBLANES
    nblk = l // ROW_TILE
    zeros_pad = jnp.zeros((pad, LANES), F32)
    xp_ref[0:pad, :] = zeros_pad
    xp_ref[pad + l:pad + l + pad, :] = zeros_pad
    xp_ref[pad:pad + l, :] = x_ref[0]

    row = lax.broadcasted_iota(jnp.int32, (ROW_TILE, LANES), 0)
    rowmod = row % SUBLANES
    cw = cw_ref[...]
    cb = cb_ref[...]
    lam = lam_ref[...]
    sp = [jnp.maximum(-lam[dd], 0.0) + jnp.log1p(jnp.exp(-jnp.abs(lam[dd]))) for dd in range(2)]

    for blk in range(nblk):
        s0 = blk * ROW_TILE
        taps = []
        for tap in range(LRU_CONV_W):
            off = tap - LRU_CONV_W // 2
            v = xp_ref[pad + s0 + off:pad + s0 + off + ROW_TILE, :]
            if blk == 0 and off > 0:
                v = jnp.where(row + off < CTX_LEN, v, 0.0)
            if s0 == CTX_LEN and off < 0:
                v = jnp.where(row + off >= 0, v, 0.0)
            taps.append(v)
        xc = cb + taps[0] * cw[0:1]
        for tap in range(1, LRU_CONV_W):
            xc = xc + taps[tap] * cw[tap:tap + 1]
        xcb = xc.astype(BF16)
        for dd, (a_ref, u_ref) in enumerate(((af_ref, uf_ref), (ab_ref, ub_ref))):
            r = _sigmoid(jnp.dot(xcb, wa_ref[dd, 0].astype(BF16), preferred_element_type=F32) + ba_ref[dd])
            i = _sigmoid(jnp.dot(xcb, wx_ref[dd, 0].astype(BF16), preferred_element_type=F32) + bx_ref[dd])
            log_a = -LRU_C * r * sp[dd]
            a = jnp.exp(log_a)
            u = jnp.sqrt(1.0 - jnp.exp(2.0 * log_a)) * (i * xc)
            a_loc, u_loc = _group_linear_scan(a, u, rowmod, SUBLANES, reverse=(dd == 1))
            a_ref[s0:s0 + ROW_TILE, :] = a_loc
            u_ref[s0:s0 + ROW_TILE, :] = u_loc

    n_chunks = l // SUBLANES
    n_ctx_chunks = CTX_LEN // SUBLANES

    def fwd_body(c, carry):
        r0 = pl.multiple_of(c * SUBLANES, SUBLANES)
        hh = af_ref[pl.ds(r0, SUBLANES), :] * carry + uf_ref[pl.ds(r0, SUBLANES), :]
        uf_ref[pl.ds(r0, SUBLANES), :] = hh
        return jnp.broadcast_to(hh[SUBLANES - 1:SUBLANES, :], (SUBLANES, LANES))

    lax.fori_loop(0, n_chunks, fwd_body, jnp.zeros((SUBLANES, LANES), F32))

    def bwd_body(j, carry):
        c = jnp.where(j < n_ctx_chunks, n_ctx_chunks - 1 - j, n_chunks - 1 - (j - n_ctx_chunks))
        r0 = pl.multiple_of(c * SUBLANES, SUBLANES)
        hh = ab_ref[pl.ds(r0, SUBLANES), :] * carry + ub_ref[pl.ds(r0, SUBLANES), :]
        ub_ref[pl.ds(r0, SUBLANES), :] = hh
        return jnp.broadcast_to(hh[0:1, :], (SUBLANES, LANES))

    lax.fori_loop(0, n_chunks, bwd_body, jnp.zeros((SUBLANES, LANES), F32))

    for blk in range(nblk):
        s0 = blk * ROW_TILE
        hsum = uf_ref[s0:s0 + ROW_TILE, :] + ub_ref[s0:s0 + ROW_TILE, :]
        y_ref[0, s0:s0 + ROW_TILE, :] = (_gelu(g_ref[0, s0:s0 + ROW_TILE, :]) * hsum).astype(BF16)


def _rglru(p3, conv_w, conv_b, wa, ba, wx, bx, lam):
    l = p3.shape[1]
    b = p3.shape[0] // (N_IN_SPLITS * N_HEADS)
    gw = N_HEADS * HEAD_DIM
    vec = lambda a: a.reshape(2, 1, gw)
    spec_vec = pl.BlockSpec((2, 1, HEAD_DIM), lambda bb, h: (0, 0, h))
    spec_w = pl.BlockSpec((2, 1, HEAD_DIM, HEAD_DIM), lambda bb, h: (0, h, 0, 0))
    return pl.pallas_call(
        functools.partial(_rglru_kernel, seq_len=l),
        out_shape=jax.ShapeDtypeStruct((b, l, gw), BF16),
        grid=(b, N_HEADS),
        in_specs=[_head_slab_spec(b, l, 0), _head_slab_spec(b, l, 1),
                  pl.BlockSpec((LRU_CONV_W, HEAD_DIM), lambda bb, h: (0, h)),
                  pl.BlockSpec((1, HEAD_DIM), lambda bb, h: (0, h)),
                  spec_w, spec_vec, spec_w, spec_vec, spec_vec],
        out_specs=pl.BlockSpec((1, l, HEAD_DIM), lambda bb, h: (bb, 0, h)),
        scratch_shapes=[pltpu.VMEM((l + 2 * SUBLANES, LANES), F32)] + [pltpu.VMEM((l, LANES), F32)] * 4,
        compiler_params=_cparams(("arbitrary", "arbitrary"), VMEM_LIMIT),
        name="rglru",
    )(p3, p3, conv_w, conv_b.reshape(1, gw), wa, vec(ba), wx, vec(bx), vec(lam))


def _na_kernel(q_ref, k_ref, v_ref, qg_ref, kg_ref, bias_ref, y_ref, qb_ref, kb_ref, vb_ref, *, seq_len):
    l = seq_len
    scale = HEAD_DIM ** -0.5
    rows = (l - CTX_LEN) // GRID_W
    kr = min(NA_WIN_ROWS, rows)
    n_loc = kr * GRID_W

    def headnorm(z, g):
        return z * lax.rsqrt(jnp.mean(z * z, axis=-1, keepdims=True) + EPS) * g

    for s0 in range(0, l, ROW_TILE):
        qb_ref[s0:s0 + ROW_TILE, :] = headnorm(q_ref[0, s0:s0 + ROW_TILE, :], qg_ref[...]).astype(BF16)
        kb_ref[s0:s0 + ROW_TILE, :] = headnorm(k_ref[0, s0:s0 + ROW_TILE, :], kg_ref[...]).astype(BF16)
        vb_ref[s0:s0 + ROW_TILE, :] = v_ref[0, s0:s0 + ROW_TILE, :].astype(BF16)

    k_ctx = kb_ref[0:CTX_LEN, :]
    v_ctx = vb_ref[0:CTX_LEN, :]

    s_c = _bdot_nt(qb_ref[0:CTX_LEN, :], k_ctx) * scale
    e_c = jnp.exp(s_c - jnp.max(s_c, axis=-1, keepdims=True))
    p_c = e_c / jnp.sum(e_c, axis=-1, keepdims=True)
    y_ref[0, 0:CTX_LEN, :] = _bdot(p_c, v_ctx).astype(BF16)

    def one_row(r):
        r0 = jnp.clip(r - kr // 2, 0, rows - kr)
        qs = pl.multiple_of(CTX_LEN + r * GRID_W, GRID_W)
        ks = pl.multiple_of(CTX_LEN + r0 * GRID_W, GRID_W)
        q_r = qb_ref[pl.ds(qs, GRID_W), :]
        k_loc = kb_ref[pl.ds(ks, n_loc), :]
        v_loc = vb_ref[pl.ds(ks, n_loc), :]
        bias = bias_ref[0, r0 - r + NA_WIN_ROWS - 1]
        s_loc = _bdot_nt(q_r, k_loc) * scale + bias
        s_ctx = _bdot_nt(q_r, k_ctx) * scale
        m = jnp.maximum(jnp.max(s_loc, axis=-1, keepdims=True), jnp.max(s_ctx, axis=-1, keepdims=True))
        e_loc = jnp.exp(s_loc - m)
        e_ctx = jnp.exp(s_ctx - m)
        inv = 1.0 / (jnp.sum(e_loc, axis=-1, keepdims=True) + jnp.sum(e_ctx, axis=-1, keepdims=True))
        o = _bdot(e_loc * inv, v_loc) + _bdot(e_ctx * inv, v_ctx)
        return qs, o.astype(BF16)

    group = 4 if rows % 4 == 0 else 1

    def rows_body(gi, carry):
        outs = [one_row(gi * group + t) for t in range(group)]
        for qs, o in outs:
            y_ref[0, pl.ds(qs, GRID_W), :] = o
        return carry

    lax.fori_loop(0, rows // group, rows_body, 0)


def _na_bias_table(rpb, rows):
    kr = min(NA_WIN_ROWS, rows)
    qc = jnp.arange(GRID_W)[:, None]
    kc = jnp.arange(GRID_W)[None, :]
    win0 = jnp.clip(qc - NA_WIN_COLS // 2, 0, GRID_W - NA_WIN_COLS)
    col_ok = (kc >= win0) & (kc < win0 + NA_WIN_COLS)
    rel_c = jnp.clip(kc - qc + NA_WIN_COLS - 1, 0, 2 * NA_WIN_COLS - 2)
    t = jnp.where(col_ok[None, None], rpb.astype(F32)[:, :, rel_c], NEG_INF)
    d = jnp.arange(NA_WIN_ROWS)[:, None] + jnp.arange(kr)[None, :]
    d = jnp.clip(d, 0, 2 * NA_WIN_ROWS - 2)
    tb = t[:, d]
    return tb.transpose(0, 1, 3, 2, 4).reshape(rpb.shape[0], NA_WIN_ROWS, GRID_W, kr * GRID_W)


def _natten(p3, qn_g, kn_g, bias_tab):
    l = p3.shape[1]
    b = p3.shape[0] // (N_IN_SPLITS * N_HEADS)
    gw = N_HEADS * HEAD_DIM
    n_loc = bias_tab.shape[-1]
    spec = functools.partial(_head_slab_spec, b, l)
    return pl.pallas_call(
        functools.partial(_na_kernel, seq_len=l),
        out_shape=jax.ShapeDtypeStruct((b, l, gw), BF16),
        grid=(b, N_HEADS),
        in_specs=[spec(2), spec(3), spec(4),
                  pl.BlockSpec((1, HEAD_DIM), lambda bb, h: (0, 0)),
                  pl.BlockSpec((1, HEAD_DIM), lambda bb, h: (0, 0)),
                  pl.BlockSpec((1, NA_WIN_ROWS, GRID_W, n_loc), lambda bb, h: (h, 0, 0, 0))],
        out_specs=pl.BlockSpec((1, l, HEAD_DIM), lambda bb, h: (bb, 0, h)),
        scratch_shapes=[pltpu.VMEM((l, LANES), BF16)] * 3,
        compiler_params=_cparams(("arbitrary", "arbitrary"), VMEM_LIMIT),
        name="natten",
    )(p3, p3, p3, qn_g.reshape(1, HEAD_DIM), kn_g.reshape(1, HEAD_DIM), bias_tab)


def _dir_prep(q, k, lf, rowmod, c, reverse):
    inc = _group_cumsum(lf, rowmod, c, reverse)
    oth = _group_cumsum(lf, rowmod, c, not reverse) - lf
    return q * jnp.exp(inc), k * jnp.exp(-inc), k * jnp.exp(oth), jnp.exp(inc + oth)


def _decay_attn_block(s0, q, k_f, k_b, v, lf_f, lf_b, c, refs):
    (qdf_ref, ksf_ref, ktf_ref, decf_ref, qdb_ref, ksb_ref, ktb_ref, decb_ref, vb_ref, o_ref) = refs
    n = ATTN_BLOCK
    row = lax.broadcasted_iota(jnp.int32, (n, n), 0)
    col = lax.broadcasted_iota(jnp.int32, (n, n), 1)
    rowmod = row % c
    same = (row // c) == (col // c)
    vb = v.astype(BF16)
    vb_ref[pl.ds(s0, n), :] = vb
    o = jnp.zeros((n, HEAD_DIM), F32)
    for reverse, kk, lf, (qd_ref, ks_ref, kt_ref, dec_ref) in (
            (False, k_f, lf_f, (qdf_ref, ksf_ref, ktf_ref, decf_ref)),
            (True, k_b, lf_b, (qdb_ref, ksb_ref, ktb_ref, decb_ref))):
        qd, ks, kt, dec = _dir_prep(q, kk, lf, rowmod, c, reverse)
        qd = qd.astype(BF16)
        ks = ks.astype(BF16)
        qd_ref[pl.ds(s0, n), :] = qd
        ks_ref[pl.ds(s0, n), :] = ks
        kt_ref[pl.ds(s0, n), :] = kt.astype(BF16)
        dec_ref[pl.ds(s0, n), :] = dec
        s = _bdot_nt(qd, ks)
        keep = same & ((col >= row) if reverse else (col <= row))
        o = o + jnp.dot(jnp.where(keep, s, 0.0).astype(BF16), vb, preferred_element_type=F32)
    o_ref[pl.ds(s0, n), :] = o


def _decay_attn_state_pass(c, seq_len, refs):
    (qdf_ref, ksf_ref, ktf_ref, decf_ref, qdb_ref, ksb_ref, ktb_ref, decb_ref, vb_ref, o_ref) = refs
    n_chunks = seq_len // c
    n_ctx_chunks = CTX_LEN // c

    def advance(qd_ref, kt_ref, dec_ref, chunk, state_t):
        r0 = pl.multiple_of(chunk * c, c)
        qd = qd_ref[pl.ds(r0, c), :]
        o_ref[pl.ds(r0, c), :] += _bdot_nt(qd, state_t)
        dec = dec_ref[pl.ds(r0, 1), :]
        upd = _bdot_tn(vb_ref[pl.ds(r0, c), :], kt_ref[pl.ds(r0, c), :])
        return state_t * dec + upd

    def body(j, carry):
        state_f, state_b = carry
        jb = jnp.where(j < n_ctx_chunks, n_ctx_chunks - 1 - j, n_chunks - 1 - (j - n_ctx_chunks))
        return (advance(qdf_ref, ktf_ref, decf_ref, j, state_f),
                advance(qdb_ref, ktb_ref, decb_ref, jb, state_b))

    zero = jnp.zeros((HEAD_DIM, HEAD_DIM), F32)
    lax.fori_loop(0, n_chunks, body, (zero, zero), unroll=8 if n_chunks % 8 == 0 else 6 if n_chunks % 6 == 0 else 2)


def _decay_scratch(l):
    one_dir = [pltpu.VMEM((l, LANES), BF16)] * 3 + [pltpu.VMEM((l, LANES), F32)]
    return one_dir + one_dir + [pltpu.VMEM((l, LANES), BF16), pltpu.VMEM((l, LANES), F32)]


def _retention_kernel(q_ref, k_ref, v_ref, g_ref, cos_ref, sin_ref, lg_ref, gn_ref, y_ref, *scratch, seq_len):
    l = seq_len
    n = ATTN_BLOCK
    lane = lax.broadcasted_iota(jnp.int32, (n, HEAD_DIM), 1)
    first = (lane % (HEAD_DIM // 2)) < (HEAD_DIM // 4)
    lg = lg_ref[0]

    def rope(z, cos, sin):
        swapped = jnp.where(first, pltpu.roll(z, HEAD_DIM - HEAD_DIM // 4, 1), pltpu.roll(z, HEAD_DIM // 4, 1))
        return z * cos + swapped * sin

    def block_body(bi, carry):
        s0 = pl.multiple_of(bi * n, n)
        cos = cos_ref[pl.ds(s0, n), :]
        sin = sin_ref[pl.ds(s0, n), :]
        q = rope(q_ref[0, pl.ds(s0, n), :], cos, sin)
        k = rope(k_ref[0, pl.ds(s0, n), :], cos, sin) * (HEAD_DIM ** -0.5)
        lf_f = jnp.broadcast_to(lg[0:1], (n, HEAD_DIM))
        lf_b = jnp.broadcast_to(lg[1:2], (n, HEAD_DIM))
        _decay_attn_block(s0, q, k, k, v_ref[0, pl.ds(s0, n), :], lf_f, lf_b, RET_CHUNK, scratch)
        return carry

    lax.fori_loop(0, l // n, block_body, 0)
    _decay_attn_state_pass(RET_CHUNK, l, scratch)
    o_ref = scratch[-1]

    def out_body(bi, carry):
        s0 = pl.multiple_of(bi * n, n)
        o = o_ref[pl.ds(s0, n), :]
        o = o - jnp.mean(o, axis=-1, keepdims=True)
        o = o * lax.rsqrt(jnp.mean(o * o, axis=-1, keepdims=True) + EPS) * gn_ref[...]
        y_ref[0, pl.ds(s0, n), :] = (_silu(g_ref[0, pl.ds(s0, n), :]) * o).astype(BF16)
        return carry

    lax.fori_loop(0, l // n, out_body, 0)


def _rope_tables(l):
    n_lat = l - CTX_LEN
    quarter = HEAD_DIM // 4
    t = jnp.arange(n_lat)
    inv_freq = ROPE_BASE ** (-jnp.arange(quarter, dtype=F32) / quarter)
    ang_r = (t // GRID_W).astype(F32)[:, None] * inv_freq[None, :]
    ang_c = (t % GRID_W).astype(F32)[:, None] * inv_freq[None, :]
    cos = jnp.concatenate([jnp.cos(ang_r), jnp.cos(ang_r), jnp.cos(ang_c), jnp.cos(ang_c)], axis=-1)
    sin = jnp.concatenate([-jnp.sin(ang_r), jnp.sin(ang_r), -jnp.sin(ang_c), jnp.sin(ang_c)], axis=-1)
    cos = jnp.concatenate([jnp.ones((CTX_LEN, HEAD_DIM), F32), cos], axis=0)
    sin = jnp.concatenate([jnp.zeros((CTX_LEN, HEAD_DIM), F32), sin], axis=0)
    return cos, sin


def _retention(p3, gn_g, cos, sin):
    l = p3.shape[1]
    b = p3.shape[0] // (N_IN_SPLITS * N_HEADS)
    gw = N_HEADS * HEAD_DIM
    log_gamma = jnp.log(1.0 - 2.0 ** (-5.0 - jnp.arange(N_HEADS, dtype=F32)))
    lg = jnp.stack([log_gamma, log_gamma[::-1]], axis=1)
    lg = jnp.broadcast_to(lg[:, :, None], (N_HEADS, 2, HEAD_DIM))
    spec = functools.partial(_head_slab_spec, b, l)
    full = pl.BlockSpec((l, HEAD_DIM), lambda bb, h: (0, 0))
    return pl.pallas_call(
        functools.partial(_retention_kernel, seq_len=l),
        out_shape=jax.ShapeDtypeStruct((b, l, gw), BF16),
        grid=(b, N_HEADS),
        in_specs=[spec(5), spec(6), spec(7), spec(8), full, full,
                  pl.BlockSpec((1, 2, HEAD_DIM), lambda bb, h: (h, 0, 0)),
                  pl.BlockSpec((1, HEAD_DIM), lambda bb, h: (0, h))],
        out_specs=pl.BlockSpec((1, l, HEAD_DIM), lambda bb, h: (bb, 0, h)),
        scratch_shapes=_decay_scratch(l),
        compiler_params=_cparams(("arbitrary", "arbitrary"), VMEM_LIMIT),
        name="retention",
    )(p3, p3, p3, p3, cos, sin, lg, gn_g.reshape(1, gw))


def _hgrn_kernel(ff_ref, fb_ref, i_ref, q_ref, g_ref, lbl_ref, on_ref, y_ref, *scratch, seq_len, layer, depth):
    l = seq_len
    n = ATTN_BLOCK
    logits = lbl_ref[...]
    log_lb, log1m_lb = [], []
    for dd in range(2):
        xs = [logits[ll * 2 + dd] for ll in range(depth)]
        mx = functools.reduce(jnp.maximum, xs)
        es = [jnp.exp(x - mx) for x in xs]
        tot = functools.reduce(lambda a, b_: a + b_, es)
        lb = jnp.zeros_like(tot)
        for ll in range(1, layer + 1):
            lb = lb + es[ll] / tot
        log_lb.append(jnp.log(lb))
        log1m_lb.append(jnp.log1p(-lb))

    def gate(fz, dd):
        log_sig = jnp.minimum(fz, 0.0) - jnp.log1p(jnp.exp(-jnp.abs(fz)))
        a = jnp.broadcast_to(log_lb[dd], fz.shape)
        bv = log1m_lb[dd] + log_sig
        log_f = jnp.maximum(a, bv) + jnp.log1p(jnp.exp(-jnp.abs(a - bv)))
        return log_f, 1.0 - jnp.exp(log_f)

    def block_body(bi, carry):
        s0 = pl.multiple_of(bi * n, n)
        lf_f, k_f = gate(ff_ref[0, pl.ds(s0, n), :], 0)
        lf_b, k_b = gate(fb_ref[0, pl.ds(s0, n), :], 1)
        _decay_attn_block(s0, q_ref[0, pl.ds(s0, n), :], k_f, k_b, i_ref[0, pl.ds(s0, n), :],
                          lf_f, lf_b, HGRN_CHUNK, scratch)
        return carry

    lax.fori_loop(0, l // n, block_body, 0)
    _decay_attn_state_pass(HGRN_CHUNK, l, scratch)
    o_ref = scratch[-1]

    def out_body(bi, carry):
        s0 = pl.multiple_of(bi * n, n)
        o = o_ref[pl.ds(s0, n), :]
        o = o * lax.rsqrt(jnp.mean(o * o, axis=-1, keepdims=True) + EPS) * on_ref[...]
        y_ref[0, pl.ds(s0, n), :] = (_silu(g_ref[0, pl.ds(s0, n), :]) * o).astype(BF16)
        return carry

    lax.fori_loop(0, l // n, out_body, 0)


def _hgrn2(p3, lb_logits, on_g, layer):
    l = p3.shape[1]
    b = p3.shape[0] // (N_IN_SPLITS * N_HEADS)
    gw = N_HEADS * HEAD_DIM
    depth = lb_logits.shape[0]
    spec = functools.partial(_head_slab_spec, b, l)
    return pl.pallas_call(
        functools.partial(_hgrn_kernel, seq_len=l, layer=layer, depth=depth),
        out_shape=jax.ShapeDtypeStruct((b, l, gw), BF16),
        grid=(b, N_HEADS),
        in_specs=[spec(9), spec(10), spec(11), spec(12), spec(13),
                  pl.BlockSpec((depth * 2, 1, HEAD_DIM), lambda bb, h: (0, 0, h)),
                  pl.BlockSpec((1, HEAD_DIM), lambda bb, h: (0, h))],
        out_specs=pl.BlockSpec((1, l, HEAD_DIM), lambda bb, h: (bb, 0, h)),
        scratch_shapes=_decay_scratch(l),
        compiler_params=_cparams(("arbitrary", "arbitrary"), VMEM_LIMIT),
        name="hgrn2",
    )(p3, p3, p3, p3, p3, lb_logits.astype(F32).reshape(depth * 2, 1, gw), on_g.reshape(1, gw))


def _topk_rows(x, k):
    r_n = x.shape[0]
    idx = lax.broadcasted_iota(jnp.int32, x.shape, 0).astype(F32)
    rank = jnp.full(x.shape, float(k), F32)
    vals = []
    for r in range(k):
        m = jnp.max(x, axis=0, keepdims=True)
        first = jnp.min(jnp.where(x == m, idx, float(r_n)), axis=0, keepdims=True)
        sel = idx == first
        rank = jnp.where(sel, float(r), rank)
        x = jnp.where(sel, -jnp.inf, x)
        vals.append(m)
    return rank, jnp.concatenate(vals, axis=0)


def _bf16_pair_words(x):
    bits = pltpu.bitcast(x.astype(BF16).astype(F32), jnp.uint32)
    return bits | (bits >> 16)


def _peer_route_kernel(zt_ref, wq_ref, sk_ref, rank2_ref, lim1_ref, e1_ref, e2_ref, *, tb):
    k = PEER_TOPK
    q = jnp.dot(wq_ref[...], zt_ref[...], preferred_element_type=F32)
    half = q.shape[0] // 2
    for c0 in range(0, tb, LANES):
        s1 = _bdot(sk_ref[0, 0], q[:half, c0:c0 + LANES])
        s2 = _bdot(sk_ref[0, 1], q[half:, c0:c0 + LANES])
        rank1, top1 = _topk_rows(s1, k)
        rank2, top2 = _topk_rows(s2, k)
        e1 = jnp.exp(s1 - top1[0:1])
        e2 = jnp.exp(s2 - top2[0:1])
        e1t = jnp.exp(top1 - top1[0:1])
        e2t = jnp.exp(top2 - top2[0:1])
        n_multi = sum(1 for r in range(k) if k // (r + 1) > 1)
        assert k - n_multi == SUBLANES
        pieces, epieces, widths = [], [], []
        for r in range(n_multi):
            n_c = k // (r + 1)
            n_pad = -(-n_c // SUBLANES) * SUBLANES
            valid = lax.broadcasted_iota(jnp.int32, (n_pad, LANES), 0) < n_c
            pieces.append(jnp.where(valid, top1[r:r + 1] + top2[0:n_pad], -jnp.inf))
            epieces.append(e1t[r:r + 1] * e2t[0:n_pad])
            widths.append(n_pad)
        pieces.append(top1[n_multi:k] + top2[0:1])
        epieces.append(e1t[n_multi:k] * e2t[0:1])
        cand = jnp.concatenate(pieces, axis=0)
        ecand = jnp.concatenate(epieces, axis=0)
        crank, _ = _topk_rows(cand, k)
        sel = crank < k
        z = jnp.sum(jnp.where(sel, ecand, 0.0), axis=0, keepdims=True)
        lim1 = jnp.zeros(rank1.shape, F32)
        self32 = sel.astype(F32)
        off = 0
        for r in range(n_multi):
            cnt = jnp.sum(self32[off:off + widths[r]], axis=0, keepdims=True)
            lim1 = jnp.where(rank1 == r, cnt, lim1)
            off += widths[r]
        for r in range(n_multi, k):
            lim1 = jnp.where(rank1 == r, self32[off + r - n_multi:off + r - n_multi + 1], lim1)
        rank2_ref[0, :, c0:c0 + LANES] = rank2.astype(BF16)
        lim1_ref[0, :, c0:c0 + LANES] = _bf16_pair_words(lim1)
        e1_ref[0, :, c0:c0 + LANES] = _bf16_pair_words(e1)
        e2_ref[0, :, c0:c0 + LANES] = (e2 / z).astype(BF16)


def _peer_route(zt, wq_t, sub_keys, tb):
    d, t = zt.shape
    qd = wq_t.shape[0] // PEER_HEADS
    tabs = [jax.ShapeDtypeStruct((PEER_HEADS, PEER_NKEYS, t), dt) for dt in (BF16, jnp.uint32, jnp.uint32, BF16)]
    tab_spec = pl.BlockSpec((1, PEER_NKEYS, tb), lambda i, h: (h, 0, i))
    return pl.pallas_call(
        functools.partial(_peer_route_kernel, tb=tb),
        out_shape=tabs,
        grid=(t // tb, PEER_HEADS),
        in_specs=[pl.BlockSpec((d, tb), lambda i, h: (0, i)),
                  pl.BlockSpec((qd, d), lambda i, h: (h, 0)),
                  pl.BlockSpec((1, 2, PEER_NKEYS, qd // 2), lambda i, h: (h, 0, 0, 0))],
        out_specs=[tab_spec] * 4,
        compiler_params=_cparams(("arbitrary", "arbitrary"), VMEM_LIMIT),
        name="peer_route",
    )(zt, wq_t, sub_keys)


def _peer_expert_kernel(zt_ref, u_ref, vt_ref, rank2_ref, lim1_ref, e1_ref, e2_ref, o_ref, w_ref, *, ek):
    e = pl.program_id(1)
    tb = zt_ref.shape[1]
    n_i1 = ek // PEER_NKEYS

    @pl.when(e == 0)
    def _():
        o_ref[...] = jnp.zeros(o_ref.shape, F32)

    half = (PEER_NKEYS // 2, PEER_LANE_CHUNK)
    for j in range(n_i1):
        i1 = e * n_i1 + j
        rows = slice(j * PEER_NKEYS, (j + 1) * PEER_NKEYS)
        act = jnp.dot(u_ref[rows, :], zt_ref[...], preferred_element_type=F32)
        for c0 in range(0, tb, PEER_LANE_CHUNK):
            cs = slice(c0, c0 + PEER_LANE_CHUNK)
            g = jnp.zeros((PEER_NKEYS, PEER_LANE_CHUNK), BF16)
            for h in range(PEER_HEADS):
                lim = pltpu.bitcast(jnp.broadcast_to(lim1_ref[h, pl.ds(i1, 1), cs], half), BF16)
                w1 = pltpu.bitcast(jnp.broadcast_to(e1_ref[h, pl.ds(i1, 1), cs], half), BF16)
                g = g + jnp.where(rank2_ref[h, :, cs] < lim, e2_ref[h, :, cs], jnp.zeros_like(g)) * w1
            w_ref[rows, cs] = (g.astype(F32) * _gelu(act[:, cs])).astype(BF16)
    o_ref[...] += jnp.dot(vt_ref[...], w_ref[...], preferred_element_type=F32)


def _peer_experts(zt, u, vt, tabs, tb, ek):
    d, t = zt.shape
    ne = u.shape[0]
    once = pl.Buffered(1)
    tab_spec = pl.BlockSpec((PEER_HEADS, PEER_NKEYS, tb), lambda i, e: (0, 0, i), pipeline_mode=once)
    return pl.pallas_call(
        functools.partial(_peer_expert_kernel, ek=ek),
        out_shape=jax.ShapeDtypeStruct((d, t), F32),
        grid=(t // tb, ne // ek),
        in_specs=[pl.BlockSpec((d, tb), lambda i, e: (0, i), pipeline_mode=once),
                  pl.BlockSpec((ek, d), lambda i, e: (e, 0)),
                  pl.BlockSpec((d, ek), lambda i, e: (0, e)),
                  tab_spec, tab_spec, tab_spec, tab_spec],
        out_specs=pl.BlockSpec((d, tb), lambda i, e: (0, i)),
        scratch_shapes=[pltpu.VMEM((ek, tb), BF16)],
        compiler_params=_cparams(("arbitrary", "arbitrary"), VMEM_LIMIT),
        name="peer_experts",
    )(zt, u, vt, *tabs)


def kernel(x, c, ctx, c_ctx, norm1_g, norm2_g, ada_w, ada_b, w_in, w_out, lru_conv_w, lru_conv_b, lru_wa, lru_ba,
           lru_wx, lru_bx, lru_lam, na_qn_g, na_kn_g, na_rpb, ret_gn_g, hgrn_lb_logits, hgrn_on_g, peer_wq,
           peer_subkeys, peer_u, peer_v):
    bsz, n_lat, d = x.shape
    depth = w_in.shape[0]
    l = CTX_LEN + n_lat
    gw = d // N_GROUPS
    assert bsz < MOD_ROWS and n_lat % ROW_TILE == 0 and CTX_LEN % ROW_TILE == 0
    ctx_row = bsz

    c16 = jnp.zeros((MOD_ROWS, d), F32).at[:bsz].set(c).at[ctx_row].set(c_ctx)
    cos, sin = _rope_tables(l)
    rows = n_lat // GRID_W

    h3 = jnp.concatenate([ctx, x], axis=1)
    tiles_full = l // ROW_TILE
    tiles_lat = n_lat // ROW_TILE
    ctx_tiles = CTX_LEN // ROW_TILE

    def row_full(i):
        return jnp.where(i % tiles_full < ctx_tiles, ctx_row, i // tiles_full)

    def row_lat(i):
        return i // tiles_lat

    h = h3.reshape(bsz * l, d)
    pending = None
    for layer in range(depth):
        last = layer == depth - 1
        mod3 = _ada_modulation(c16, ada_w, ada_b, layer).reshape(MOD_ROWS * N_MOD, 1, d)
        if pending is None:
            (u,) = _resid_norm(h, mod3, row_full, norm_g=norm1_g[layer], shift_k=0, scale_k=1)
        else:
            ft, mod3_prev = pending
            h, u = _resid_norm(h, mod3, row_full, ft=ft, gate_mod3=mod3_prev, gate_k=5,
                               norm_g=norm1_g[layer], shift_k=0, scale_k=1)
        p = _matmul(u, w_in[layer].astype(BF16), 512, 1024)
        p3 = p.reshape(N_IN_SPLITS * N_HEADS * bsz, l, HEAD_DIM)
        y_a = _rglru(p3, lru_conv_w[layer], lru_conv_b[layer], lru_wa[layer], lru_ba[layer],
                     lru_wx[layer], lru_bx[layer], lru_lam[layer])
        y_b = _natten(p3, na_qn_g[layer], na_kn_g[layer], _na_bias_table(na_rpb[layer], rows))
        y_c = _retention(p3, ret_gn_g[layer], cos, sin)
        y_d = _hgrn2(p3, hgrn_lb_logits, hgrn_on_g[layer], layer)
        w4 = w_out[layer].astype(BF16).reshape(N_GROUPS, gw, d)
        if last:
            hm3 = _out_proj((y_a, y_b, y_c, y_d), w4, h.reshape(bsz, l, d), mod3, row_off=ctx_tiles,
                            row_fn=lambda bb, i: bb, gate_k=2)
            row_fn = row_lat
        else:
            hm3 = _out_proj((y_a, y_b, y_c, y_d), w4, h.reshape(bsz, l, d), mod3, row_off=0,
                            row_fn=lambda bb, i: jnp.where(i < ctx_tiles, ctx_row, bb), gate_k=2)
            row_fn = row_full
        h = hm3.reshape(-1, d)
        (zt,) = _resid_norm(h, mod3, row_fn, norm_g=norm2_g[layer], shift_k=3, scale_k=4, transpose_out=True)
        tabs = _peer_route(zt, peer_wq[layer].T.astype(BF16), peer_subkeys[layer], PEER_TOKEN_BLOCK)
        ft = _peer_experts(zt, peer_u[layer].astype(BF16), peer_v[layer].T.astype(BF16), tabs,
                           PEER_TOKEN_BLOCK, PEER_EXPERT_BLOCK)
        if last:
            (h,) = _resid_norm(h, mod3, row_fn, ft=ft, gate_k=5)
        else:
            pending = (ft, mod3)
    return h.reshape(bsz, n_lat, d)
```

```python
import functools

import jax
import jax.numpy as jnp
from jax import lax
from jax.experimental import pallas as pl
from jax.experimental.pallas import tpu as pltpu

F32 = jnp.float32
BF16 = jnp.bfloat16

N_GROUPS = 4
N_HEADS = 8
HEAD_DIM = 128
N_IN_SPLITS = 14
N_MOD = 6
CTX_LEN = 256
GRID_W = 64
LRU_CONV_W = 4
LRU_C = 8.0
NA_WIN_ROWS = 8
NA_WIN_COLS = 16
RET_CHUNK = 128
HGRN_CHUNK = 32
PEER_HEADS = 8
PEER_NKEYS = 128
PEER_TOPK = 16
ROPE_BASE = 10000.0
EPS = 1e-6
NEG_INF = -1e30

SUBLANES = 8
LANES = 128
ROW_TILE = 256
ATTN_BLOCK = 128
MOD_ROWS = 16
PEER_TOKEN_BLOCK = 512
PEER_EXPERT_BLOCK = 512
PEER_LANE_CHUNK = 256
PEER_DOWN_GROUP = 256
VMEM_LIMIT = 56 * 1024 * 1024


def _cparams(sem, vmem=None):
    return pltpu.CompilerParams(dimension_semantics=sem, vmem_limit_bytes=vmem)


def _bdot(a, b):
    return jnp.dot(a.astype(BF16), b.astype(BF16), preferred_element_type=F32)


def _bdot_nt(a, b):
    return lax.dot_general(a.astype(BF16), b.astype(BF16), (((1,), (1,)), ((), ())),
                           preferred_element_type=F32)


def _bdot_tn(a, b):
    return lax.dot_general(a.astype(BF16), b.astype(BF16), (((0,), (0,)), ((), ())),
                           preferred_element_type=F32)


def _gelu(x):
    c = 0.7978845608028654
    return x / (1.0 + jnp.exp(x * (-2.0 * c - (2.0 * c * 0.044715) * (x * x))))


def _sigmoid(x):
    return 1.0 / (1.0 + jnp.exp(-x))


def _silu(x):
    return x * _sigmoid(x)


def _ada_kernel(c_ref, w_ref, b_ref, o_ref):
    x = c_ref[...]
    x = _silu(x)
    hi = x.astype(BF16)
    lo = (x - hi.astype(F32)).astype(BF16)
    w = w_ref[0]
    whi = w.astype(BF16)
    wlo = (w - whi.astype(F32)).astype(BF16)
    acc = jnp.dot(hi, whi, preferred_element_type=F32)
    acc += jnp.dot(hi, wlo, preferred_element_type=F32)
    acc += jnp.dot(lo, whi, preferred_element_type=F32)
    o_ref[...] = acc + b_ref[0]


def _ada_modulation(c16, w_all, b_all, layer):
    depth, d, n = w_all.shape
    tn = 512
    return pl.pallas_call(
        _ada_kernel,
        out_shape=jax.ShapeDtypeStruct((MOD_ROWS, n), F32),
        grid=(n // tn,),
        in_specs=[pl.BlockSpec((MOD_ROWS, d), lambda j: (0, 0)),
                  pl.BlockSpec((1, d, tn), lambda j: (layer, 0, j)),
                  pl.BlockSpec((1, 1, tn), lambda j: (layer, 0, j))],
        out_specs=pl.BlockSpec((MOD_ROWS, tn), lambda j: (0, j)),
        compiler_params=_cparams(("arbitrary",), VMEM_LIMIT),
        name="ada_modulation",
    )(c16, w_all, b_all.reshape(depth, 1, n))


def _resid_norm_kernel(*refs, has_f, has_norm, transpose_out):
    it = iter(refs)
    h_ref = next(it)
    if has_f:
        ft_ref = next(it)
        gate_ref = next(it)
    if has_norm:
        g_ref = next(it)
        sh_ref = next(it)
        sc_ref = next(it)
    if has_f:
        hn_ref = next(it)
    if has_norm:
        u_ref = next(it)
    x = h_ref[...]
    if has_f:
        x = x + gate_ref[0] * ft_ref[...].T
        hn_ref[...] = x
    if has_norm:
        ms = jnp.mean(x * x, axis=-1, keepdims=True)
        y = x * lax.rsqrt(ms + EPS) * g_ref[...]
        y = y * (1.0 + sc_ref[0]) + sh_ref[0]
        if transpose_out:
            u_ref[...] = y.T.astype(BF16)
        else:
            u_ref[...] = y.astype(BF16)


def _resid_norm(h, mod3, row_fn, *, ft=None, gate_mod3=None, gate_k=None, norm_g=None, shift_k=None,
                scale_k=None, transpose_out=False):
    t, d = h.shape
    nt = t // ROW_TILE
    has_f = ft is not None
    has_norm = norm_g is not None
    in_specs = [pl.BlockSpec((ROW_TILE, d), lambda i: (i, 0))]
    args = [h]
    if has_f:
        in_specs += [pl.BlockSpec((d, ROW_TILE), lambda i: (0, i)),
                     pl.BlockSpec((1, 1, d), lambda i: (row_fn(i) * N_MOD + gate_k, 0, 0))]
        args += [ft, mod3 if gate_mod3 is None else gate_mod3]
    if has_norm:
        in_specs += [pl.BlockSpec((1, d), lambda i: (0, 0)),
                     pl.BlockSpec((1, 1, d), lambda i: (row_fn(i) * N_MOD + shift_k, 0, 0)),
                     pl.BlockSpec((1, 1, d), lambda i: (row_fn(i) * N_MOD + scale_k, 0, 0))]
        args += [norm_g.reshape(1, d), mod3, mod3]
    out_shape, out_specs = [], []
    if has_f:
        out_shape.append(jax.ShapeDtypeStruct((t, d), F32))
        out_specs.append(pl.BlockSpec((ROW_TILE, d), lambda i: (i, 0)))
    if has_norm:
        if transpose_out:
            out_shape.append(jax.ShapeDtypeStruct((d, t), BF16))
            out_specs.append(pl.BlockSpec((d, ROW_TILE), lambda i: (0, i)))
        else:
            out_shape.append(jax.ShapeDtypeStruct((t, d), BF16))
            out_specs.append(pl.BlockSpec((ROW_TILE, d), lambda i: (i, 0)))
    outs = pl.pallas_call(
        functools.partial(_resid_norm_kernel, has_f=has_f, has_norm=has_norm, transpose_out=transpose_out),
        out_shape=out_shape, grid=(nt,), in_specs=in_specs, out_specs=out_specs,
        compiler_params=_cparams(("arbitrary",), VMEM_LIMIT),
        name="resid_norm",
    )(*args)
    return outs


def _mm_kernel(x_ref, w_ref, o_ref, wb_ref):
    @pl.when(pl.program_id(1) == 0)
    def _():
        wb_ref[...] = w_ref[0].astype(BF16)

    r = jnp.dot(x_ref[...], wb_ref[...], preferred_element_type=F32)
    for hh in range(o_ref.shape[0]):
        o_ref[hh] = r[:, hh * HEAD_DIM:(hh + 1) * HEAD_DIM]


def _matmul(x, w_all, layer, tm, tn):
    m, k = x.shape
    n = w_all.shape[2]
    return pl.pallas_call(
        _mm_kernel,
        out_shape=jax.ShapeDtypeStruct((n // HEAD_DIM, m, HEAD_DIM), F32),
        grid=(n // tn, m // tm),
        in_specs=[pl.BlockSpec((tm, k), lambda j, i: (i, 0)),
                  pl.BlockSpec((1, k, tn), lambda j, i: (layer, 0, j), pipeline_mode=pl.Buffered(1))],
        out_specs=pl.BlockSpec((tn // HEAD_DIM, tm, HEAD_DIM), lambda j, i: (j, i, 0)),
        scratch_shapes=[pltpu.VMEM((k, tn), BF16)],
        compiler_params=_cparams(("arbitrary", "arbitrary"), VMEM_LIMIT),
        name="in_proj",
    )(x, w_all)


def _cast_kernel(w_ref, o_ref, *, transpose):
    w = w_ref[0]
    if transpose:
        o_ref[0] = w.T.astype(BF16)
    else:
        o_ref[...] = w.astype(BF16)


def _cast_weight(w_all, layer, *, transpose, rows_per_step):
    _, r, c = w_all.shape
    tr = rows_per_step
    if transpose:
        out_shape, out_spec = (r // tr, c, tr), pl.BlockSpec((1, c, tr), lambda i: (i, 0, 0))
    else:
        out_shape, out_spec = (r, c), pl.BlockSpec((tr, c), lambda i: (i, 0))
    return pl.pallas_call(
        functools.partial(_cast_kernel, transpose=transpose),
        out_shape=jax.ShapeDtypeStruct(out_shape, BF16),
        grid=(r // tr,),
        in_specs=[pl.BlockSpec((1, tr, c), lambda i: (layer, i, 0))],
        out_specs=out_spec,
        compiler_params=_cparams(("arbitrary",), VMEM_LIMIT),
        name="cast_weight",
    )(w_all)


def _head_slab_spec(bsz, l, split):
    return pl.BlockSpec((1, l, HEAD_DIM), lambda bb, h: ((split * N_HEADS + h) * bsz + bb, 0, 0))


def _wout_kernel(ya_ref, yb_ref, yc_ref, yd_ref, w_ref, h_ref, gate_ref, o_ref, wb_ref):
    @pl.when((pl.program_id(1) == 0) & (pl.program_id(2) == 0))
    def _():
        wb_ref[...] = w_ref[0].astype(BF16)

    gw = ya_ref.shape[2]
    acc = jnp.dot(ya_ref[0], wb_ref[0 * gw:1 * gw, :], preferred_element_type=F32)
    acc += jnp.dot(yb_ref[0], wb_ref[1 * gw:2 * gw, :], preferred_element_type=F32)
    acc += jnp.dot(yc_ref[0], wb_ref[2 * gw:3 * gw, :], preferred_element_type=F32)
    acc += jnp.dot(yd_ref[0], wb_ref[3 * gw:4 * gw, :], preferred_element_type=F32)
    o_ref[0] = h_ref[0] + gate_ref[0] * acc


def _out_proj(ys, w_all, layer, h3, mod3, *, row_off, row_fn, gate_k):
    b, l, gw = ys[0].shape
    d = w_all.shape[2]
    nrt = l // ROW_TILE - row_off
    tn = 1024
    y_spec = pl.BlockSpec((1, ROW_TILE, gw), lambda j, bb, i: (bb, i + row_off, 0))
    return pl.pallas_call(
        _wout_kernel,
        out_shape=jax.ShapeDtypeStruct((b, nrt * ROW_TILE, d), F32),
        grid=(d // tn, b, nrt),
        in_specs=[y_spec, y_spec, y_spec, y_spec,
                  pl.BlockSpec((1, N_GROUPS * gw, tn), lambda j, bb, i: (layer, 0, j),
                               pipeline_mode=pl.Buffered(1)),
                  pl.BlockSpec((1, ROW_TILE, tn), lambda j, bb, i: (bb, i + row_off, j)),
                  pl.BlockSpec((1, 1, tn), lambda j, bb, i: (row_fn(bb, i) * N_MOD + gate_k, 0, j))],
        out_specs=pl.BlockSpec((1, ROW_TILE, tn), lambda j, bb, i: (bb, i, j)),
        scratch_shapes=[pltpu.VMEM((N_GROUPS * gw, tn), BF16)],
        compiler_params=_cparams(("arbitrary", "arbitrary", "arbitrary"), VMEM_LIMIT),
        name="out_proj",
    )(*ys, w_all, h3, mod3)


def _group_cumsum(x, rowmod, c, reverse):
    n = x.shape[0]
    k = 1
    while k < c:
        if reverse:
            x = x + jnp.where(rowmod < c - k, pltpu.roll(x, n - k, 0), 0.0)
        else:
            x = x + jnp.where(rowmod >= k, pltpu.roll(x, k, 0), 0.0)
        k *= 2
    return x


def _group_linear_scan(a, u, rowmod, c, reverse):
    n = a.shape[0]
    k = 1
    while k < c:
        if reverse:
            ok = rowmod < c - k
            a_s = pltpu.roll(a, n - k, 0)
            u_s = pltpu.roll(u, n - k, 0)
        else:
            ok = rowmod >= k
            a_s = pltpu.roll(a, k, 0)
            u_s = pltpu.roll(u, k, 0)
        u = jnp.where(ok, a * u_s + u, u)
        a = jnp.where(ok, a * a_s, a)
        k *= 2
    return a, u


def _rglru_kernel(x_ref, g_ref, cw_ref, cb_ref, wa_ref, ba_ref, wx_ref, bx_ref, lam_ref, y_ref,
                  xp_ref, af_ref, uf_ref, ab_ref, ub_ref, *, seq_len):
    l = seq_len
    pad = SUBLANES
    nblk = l // ROW_TILE
    zeros_pad = jnp.zeros((pad, LANES), F32)
    xp_ref[0:pad, :] = zeros_pad
    xp_ref[pad + l:pad + l + pad, :] = zeros_pad
    xp_ref[pad:pad + l, :] = x_ref[0]

    row = lax.broadcasted_iota(jnp.int32, (ROW_TILE, LANES), 0)
    rowmod = row % SUBLANES
    cw = cw_ref[...]
    cb = cb_ref[...]
    lam = lam_ref[...]
    sp = [jnp.maximum(-lam[dd], 0.0) + jnp.log1p(jnp.exp(-jnp.abs(lam[dd]))) for dd in range(2)]

    for blk in range(nblk):
        s0 = blk * ROW_TILE
        taps = []
        for tap in range(LRU_CONV_W):
            off = tap - LRU_CONV_W // 2
            v = xp_ref[pad + s0 + off:pad + s0 + off + ROW_TILE, :]
            if blk == 0 and off > 0:
                v = jnp.where(row + off < CTX_LEN, v, 0.0)
            if s0 == CTX_LEN and off < 0:
                v = jnp.where(row + off >= 0, v, 0.0)
            taps.append(v)
        xc = cb + taps[0] * cw[0:1]
        for tap in range(1, LRU_CONV_W):
            xc = xc + taps[tap] * cw[tap:tap + 1]
        xcb = xc.astype(BF16)
        for dd, (a_ref, u_ref) in enumerate(((af_ref, uf_ref), (ab_ref, ub_ref))):
            r = _sigmoid(jnp.dot(xcb, wa_ref[dd, 0].astype(BF16), preferred_element_type=F32) + ba_ref[dd])
            i = _sigmoid(jnp.dot(xcb, wx_ref[dd, 0].astype(BF16), preferred_element_type=F32) + bx_ref[dd])
            log_a = -LRU_C * r * sp[dd]
            a = jnp.exp(log_a)
            u = jnp.sqrt(1.0 - jnp.exp(2.0 * log_a)) * (i * xc)
            a_loc, u_loc = _group_linear_scan(a, u, rowmod, SUBLANES, reverse=(dd == 1))
            a_ref[s0:s0 + ROW_TILE, :] = a_loc
            u_ref[s0:s0 + ROW_TILE, :] = u_loc

    n_chunks = l // SUBLANES
    n_ctx_chunks = CTX_LEN // SUBLANES

    def fwd_body(c, carry):
        r0 = pl.multiple_of(c * SUBLANES, SUBLANES)
        hh = af_ref[pl.ds(r0, SUBLANES), :] * carry + uf_ref[pl.ds(r0, SUBLANES), :]
        uf_ref[pl.ds(r0, SUBLANES), :] = hh
        return jnp.broadcast_to(hh[SUBLANES - 1:SUBLANES, :], (SUBLANES, LANES))

    lax.fori_loop(0, n_chunks, fwd_body, jnp.zeros((SUBLANES, LANES), F32))

    def bwd_body(j, carry):
        c = jnp.where(j < n_ctx_chunks, n_ctx_chunks - 1 - j, n_chunks - 1 - (j - n_ctx_chunks))
        r0 = pl.multiple_of(c * SUBLANES, SUBLANES)
        hh = ab_ref[pl.ds(r0, SUBLANES), :] * carry + ub_ref[pl.ds(r0, SUBLANES), :]
        ub_ref[pl.ds(r0, SUBLANES), :] = hh
        return jnp.broadcast_to(hh[0:1, :], (SUBLANES, LANES))

    lax.fori_loop(0, n_chunks, bwd_body, jnp.zeros((SUBLANES, LANES), F32))

    for blk in range(nblk):
        s0 = blk * ROW_TILE
        hsum = uf_ref[s0:s0 + ROW_TILE, :] + ub_ref[s0:s0 + ROW_TILE, :]
        y_ref[0, s0:s0 + ROW_TILE, :] = (_gelu(g_ref[0, s0:s0 + ROW_TILE, :]) * hsum).astype(BF16)


def _rglru(p3, conv_w, conv_b, wa, ba, wx, bx, lam):
    l = p3.shape[1]
    b = p3.shape[0] // (N_IN_SPLITS * N_HEADS)
    gw = N_HEADS * HEAD_DIM
    vec = lambda a: a.reshape(2, 1, gw)
    spec_vec = pl.BlockSpec((2, 1, HEAD_DIM), lambda bb, h: (0, 0, h))
    spec_w = pl.BlockSpec((2, 1, HEAD_DIM, HEAD_DIM), lambda bb, h: (0, h, 0, 0))
    return pl.pallas_call(
        functools.partial(_rglru_kernel, seq_len=l),
        out_shape=jax.ShapeDtypeStruct((b, l, gw), BF16),
        grid=(b, N_HEADS),
        in_specs=[_head_slab_spec(b, l, 0), _head_slab_spec(b, l, 1),
                  pl.BlockSpec((LRU_CONV_W, HEAD_DIM), lambda bb, h: (0, h)),
                  pl.BlockSpec((1, HEAD_DIM), lambda bb, h: (0, h)),
                  spec_w, spec_vec, spec_w, spec_vec, spec_vec],
        out_specs=pl.BlockSpec((1, l, HEAD_DIM), lambda bb, h: (bb, 0, h)),
        scratch_shapes=[pltpu.VMEM((l + 2 * SUBLANES, LANES), F32)] + [pltpu.VMEM((l, LANES), F32)] * 4,
        compiler_params=_cparams(("arbitrary", "arbitrary"), VMEM_LIMIT),
        name="rglru",
    )(p3, p3, conv_w, conv_b.reshape(1, gw), wa, vec(ba), wx, vec(bx), vec(lam))


def _na_kernel(*refs, seq_len):
    nh = HEADS_PER_STEP
    q_refs, k_refs, v_refs = (refs[i * nh:(i + 1) * nh] for i in range(3))
    qg_ref, kg_ref, bias_ref, y_ref = refs[3 * nh:3 * nh + 4]
    scratch = refs[3 * nh + 4:]
    qb_refs, kb_refs, vb_refs = (scratch[i * nh:(i + 1) * nh] for i in range(3))
    l = seq_len
    scale = HEAD_DIM ** -0.5
    rows = (l - CTX_LEN) // GRID_W
    kr = min(NA_WIN_ROWS, rows)
    n_loc = kr * GRID_W

    def headnorm(z, g):
        return z * lax.rsqrt(jnp.mean(z * z, axis=-1, keepdims=True) + EPS) * g

    for hh in range(nh):
        cols = slice(hh * HEAD_DIM, (hh + 1) * HEAD_DIM)
        for s0 in range(0, l, ROW_TILE):
            rs = slice(s0, s0 + ROW_TILE)
            qb_refs[hh][rs, :] = headnorm(q_refs[hh][0, rs, :], qg_ref[...]).astype(BF16)
            kb_refs[hh][rs, :] = headnorm(k_refs[hh][0, rs, :], kg_ref[...]).astype(BF16)
            vb_refs[hh][rs, :] = v_refs[hh][0, rs, :].astype(BF16)
        k_ctx = kb_refs[hh][0:CTX_LEN, :]
        s_c = _bdot_nt(qb_refs[hh][0:CTX_LEN, :], k_ctx) * scale
        e_c = jnp.exp(s_c - jnp.max(s_c, axis=-1, keepdims=True))
        p_c = e_c / jnp.sum(e_c, axis=-1, keepdims=True)
        y_ref[0, 0:CTX_LEN, cols] = _bdot(p_c, vb_refs[hh][0:CTX_LEN, :]).astype(BF16)

    def scores(r, hh):
        r0 = jnp.clip(r - kr // 2, 0, rows - kr)
        qs = pl.multiple_of(CTX_LEN + r * GRID_W, GRID_W)
        ks = pl.multiple_of(CTX_LEN + r0 * GRID_W, GRID_W)
        q_r = qb_refs[hh][pl.ds(qs, GRID_W), :]
        bias = bias_ref[hh, r0 - r + NA_WIN_ROWS - 1]
        s_loc = _bdot_nt(q_r, kb_refs[hh][pl.ds(ks, n_loc), :]) * scale + bias
        s_ctx = _bdot_nt(q_r, kb_refs[hh][0:CTX_LEN, :]) * scale
        return hh, qs, ks, s_loc, s_ctx

    def softmax(hh, qs, ks, s_loc, s_ctx):
        m = jnp.maximum(jnp.max(s_loc, axis=-1, keepdims=True), jnp.max(s_ctx, axis=-1, keepdims=True))
        e_loc = jnp.exp(s_loc - m)
        e_ctx = jnp.exp(s_ctx - m)
        inv = 1.0 / (jnp.sum(e_loc, axis=-1, keepdims=True) + jnp.sum(e_ctx, axis=-1, keepdims=True))
        return hh, qs, ks, (e_loc * inv).astype(BF16), (e_ctx * inv).astype(BF16)

    def readout(hh, qs, ks, p_loc, p_ctx):
        o = _bdot(p_loc, vb_refs[hh][pl.ds(ks, n_loc), :]) + _bdot(p_ctx, vb_refs[hh][0:CTX_LEN, :])
        y_ref[0, pl.ds(qs, GRID_W), hh * HEAD_DIM:(hh + 1) * HEAD_DIM] = o.astype(BF16)

    group = 4 if rows % 4 == 0 else 1

    def rows_body(gi, carry):
        chains = [scores(gi * group + t, hh) for t in range(group) for hh in range(nh)]
        chains = [softmax(*ch) for ch in chains]
        for ch in chains:
            readout(*ch)
        return carry

    lax.fori_loop(0, rows // group, rows_body, 0)


def _na_bias_table(rpb, rows):
    kr = min(NA_WIN_ROWS, rows)
    qc = jnp.arange(GRID_W)[:, None]
    kc = jnp.arange(GRID_W)[None, :]
    win0 = jnp.clip(qc - NA_WIN_COLS // 2, 0, GRID_W - NA_WIN_COLS)
    col_ok = (kc >= win0) & (kc < win0 + NA_WIN_COLS)
    rel_c = jnp.clip(kc - qc + NA_WIN_COLS - 1, 0, 2 * NA_WIN_COLS - 2)
    t = jnp.where(col_ok[None, None], rpb.astype(F32)[:, :, rel_c], NEG_INF)
    d = jnp.arange(NA_WIN_ROWS)[:, None] + jnp.arange(kr)[None, :]
    d = jnp.clip(d, 0, 2 * NA_WIN_ROWS - 2)
    tb = t[:, d]
    return tb.transpose(0, 1, 3, 2, 4).reshape(rpb.shape[0], NA_WIN_ROWS, GRID_W, kr * GRID_W)


def _natten(p3, qn_g, kn_g, bias_tab):
    l = p3.shape[1]
    b = p3.shape[0] // (N_IN_SPLITS * N_HEADS)
    gw = N_HEADS * HEAD_DIM
    n_loc = bias_tab.shape[-1]
    nh = HEADS_PER_STEP
    slabs = [s for split in (2, 3, 4) for s in _pair_slab_specs(b, l, split)]
    return pl.pallas_call(
        functools.partial(_na_kernel, seq_len=l),
        out_shape=jax.ShapeDtypeStruct((b, l, gw), BF16),
        grid=(b, N_HEADS // nh),
        in_specs=slabs + [pl.BlockSpec((1, HEAD_DIM), lambda bb, hp: (0, 0)),
                          pl.BlockSpec((1, HEAD_DIM), lambda bb, hp: (0, 0)),
                          pl.BlockSpec((nh, NA_WIN_ROWS, GRID_W, n_loc), lambda bb, hp: (hp, 0, 0, 0))],
        out_specs=pl.BlockSpec((1, l, nh * HEAD_DIM), lambda bb, hp: (bb, 0, hp)),
        scratch_shapes=[pltpu.VMEM((l, LANES), BF16)] * (3 * nh),
        compiler_params=_cparams(("arbitrary", "arbitrary"), VMEM_LIMIT),
        name="natten",
    )(*([p3] * (3 * nh)), qn_g.reshape(1, HEAD_DIM), kn_g.reshape(1, HEAD_DIM), bias_tab)


def _dir_prep(q, k, lf, rowmod, c, reverse):
    inc = _group_cumsum(lf, rowmod, c, reverse)
    oth = _group_cumsum(lf, rowmod, c, not reverse) - lf
    return q * jnp.exp(inc), k * jnp.exp(-inc), k * jnp.exp(oth), jnp.exp(inc + oth)


def _decay_attn_block(s0, q, k_f, k_b, v, lf_f, lf_b, c, refs):
    (qdf_ref, ksf_ref, ktf_ref, decf_ref, qdb_ref, ksb_ref, ktb_ref, decb_ref, vb_ref, o_ref) = refs
    n = ATTN_BLOCK
    row = lax.broadcasted_iota(jnp.int32, (n, n), 0)
    col = lax.broadcasted_iota(jnp.int32, (n, n), 1)
    rowmod = row % c
    same = (row // c) == (col // c)
    vb = v.astype(BF16)
    vb_ref[pl.ds(s0, n), :] = vb
    o = jnp.zeros((n, HEAD_DIM), F32)
    for reverse, kk, lf, (qd_ref, ks_ref, kt_ref, dec_ref) in (
            (False, k_f, lf_f, (qdf_ref, ksf_ref, ktf_ref, decf_ref)),
            (True, k_b, lf_b, (qdb_ref, ksb_ref, ktb_ref, decb_ref))):
        qd, ks, kt, dec = _dir_prep(q, kk, lf, rowmod, c, reverse)
        qd = qd.astype(BF16)
        ks = ks.astype(BF16)
        qd_ref[pl.ds(s0, n), :] = qd
        ks_ref[pl.ds(s0, n), :] = ks
        kt_ref[pl.ds(s0, n), :] = kt.astype(BF16)
        dec_ref[pl.ds(s0, n), :] = dec
        s = _bdot_nt(qd, ks)
        keep = same & ((col >= row) if reverse else (col <= row))
        o = o + jnp.dot(jnp.where(keep, s, 0.0).astype(BF16), vb, preferred_element_type=F32)
    o_ref[pl.ds(s0, n), :] = o


def _decay_attn_state_pass(c, seq_len, ref_sets):
    n_chunks = seq_len // c
    n_ctx_chunks = CTX_LEN // c

    steps = 4 if n_chunks % 4 == 0 else 3 if n_chunks % 3 == 0 else 1

    def body(jj, carry):
        chains = []
        for hh, refs in enumerate(ref_sets):
            (qdf_ref, _, ktf_ref, decf_ref, qdb_ref, _, ktb_ref, decb_ref, vb_ref, o_ref) = refs
            chains.append((qdf_ref, ktf_ref, decf_ref, vb_ref, o_ref, False))
            chains.append((qdb_ref, ktb_ref, decb_ref, vb_ref, o_ref, True))
        starts = []
        for t in range(steps):
            j = jj * steps + t
            jb = jnp.where(j < n_ctx_chunks, n_ctx_chunks - 1 - j, n_chunks - 1 - (j - n_ctx_chunks))
            starts.append((pl.multiple_of(j * c, c), pl.multiple_of(jb * c, c)))
        upds = [[_bdot_tn(vb_ref[pl.ds(starts[t][rev], c), :], kt_ref[pl.ds(starts[t][rev], c), :])
                 for t in range(steps)]
                for (_, kt_ref, _, vb_ref, _, rev) in chains]
        states = list(carry)
        for t in range(steps):
            for ci, (qd_ref, _, dec_ref, _, o_ref, rev) in enumerate(chains):
                r0 = starts[t][rev]
                o_ref[pl.ds(r0, c), :] += _bdot_nt(qd_ref[pl.ds(r0, c), :], states[ci])
                states[ci] = states[ci] * dec_ref[pl.ds(r0, 1), :] + upds[ci][t]
        return tuple(states)

    zero = jnp.zeros((HEAD_DIM, HEAD_DIM), F32)
    lax.fori_loop(0, n_chunks // steps, body, (zero,) * (2 * len(ref_sets)))


DECAY_SCRATCH_PER_HEAD = 10
HEADS_PER_STEP = 2


def _decay_scratch(l):
    one_dir = [pltpu.VMEM((l, LANES), BF16)] * 3 + [pltpu.VMEM((l, LANES), F32)]
    one_head = one_dir + one_dir + [pltpu.VMEM((l, LANES), BF16), pltpu.VMEM((l, LANES), F32)]
    assert len(one_head) == DECAY_SCRATCH_PER_HEAD
    return one_head * HEADS_PER_STEP


def _head_sets(scratch):
    n = DECAY_SCRATCH_PER_HEAD
    return [tuple(scratch[hh * n:(hh + 1) * n]) for hh in range(HEADS_PER_STEP)]


def _pair_slab_specs(bsz, l, split):
    return [pl.BlockSpec((1, l, HEAD_DIM),
                         lambda bb, hp, hh=hh: ((split * N_HEADS + hp * HEADS_PER_STEP + hh) * bsz + bb, 0, 0))
            for hh in range(HEADS_PER_STEP)]


def _retention_kernel(*refs, seq_len):
    nh = HEADS_PER_STEP
    q_refs, k_refs, v_refs, g_refs = (refs[i * nh:(i + 1) * nh] for i in range(4))
    cos_ref, sin_ref, lg_ref, gn_ref, y_ref = refs[4 * nh:4 * nh + 5]
    sets = _head_sets(refs[4 * nh + 5:])
    l = seq_len
    n = ATTN_BLOCK
    lane = lax.broadcasted_iota(jnp.int32, (n, HEAD_DIM), 1)
    first = (lane % (HEAD_DIM // 2)) < (HEAD_DIM // 4)

    def rope(z, cos, sin):
        swapped = jnp.where(first, pltpu.roll(z, HEAD_DIM - HEAD_DIM // 4, 1), pltpu.roll(z, HEAD_DIM // 4, 1))
        return z * cos + swapped * sin

    def block_body(bi, carry):
        s0 = pl.multiple_of(bi * n, n)
        cos = cos_ref[pl.ds(s0, n), :]
        sin = sin_ref[pl.ds(s0, n), :]
        for hh in range(nh):
            lg = lg_ref[hh]
            q = rope(q_refs[hh][0, pl.ds(s0, n), :], cos, sin)
            k = rope(k_refs[hh][0, pl.ds(s0, n), :], cos, sin) * (HEAD_DIM ** -0.5)
            lf_f = jnp.broadcast_to(lg[0:1], (n, HEAD_DIM))
            lf_b = jnp.broadcast_to(lg[1:2], (n, HEAD_DIM))
            _decay_attn_block(s0, q, k, k, v_refs[hh][0, pl.ds(s0, n), :], lf_f, lf_b, RET_CHUNK, sets[hh])
        return carry

    lax.fori_loop(0, l // n, block_body, 0)
    _decay_attn_state_pass(RET_CHUNK, l, sets)

    def out_body(bi, carry):
        s0 = pl.multiple_of(bi * n, n)
        for hh in range(nh):
            cols = slice(hh * HEAD_DIM, (hh + 1) * HEAD_DIM)
            o = sets[hh][-1][pl.ds(s0, n), :]
            o = o - jnp.mean(o, axis=-1, keepdims=True)
            o = o * lax.rsqrt(jnp.mean(o * o, axis=-1, keepdims=True) + EPS) * gn_ref[:, cols]
            y_ref[0, pl.ds(s0, n), cols] = (_silu(g_refs[hh][0, pl.ds(s0, n), :]) * o).astype(BF16)
        return carry

    lax.fori_loop(0, l // n, out_body, 0)


def _rope_tables(l):
    n_lat = l - CTX_LEN
    quarter = HEAD_DIM // 4
    t = jnp.arange(n_lat)
    inv_freq = ROPE_BASE ** (-jnp.arange(quarter, dtype=F32) / quarter)
    ang_r = (t // GRID_W).astype(F32)[:, None] * inv_freq[None, :]
    ang_c = (t % GRID_W).astype(F32)[:, None] * inv_freq[None, :]
    cos = jnp.concatenate([jnp.cos(ang_r), jnp.cos(ang_r), jnp.cos(ang_c), jnp.cos(ang_c)], axis=-1)
    sin = jnp.concatenate([-jnp.sin(ang_r), jnp.sin(ang_r), -jnp.sin(ang_c), jnp.sin(ang_c)], axis=-1)
    cos = jnp.concatenate([jnp.ones((CTX_LEN, HEAD_DIM), F32), cos], axis=0)
    sin = jnp.concatenate([jnp.zeros((CTX_LEN, HEAD_DIM), F32), sin], axis=0)
    return cos, sin


def _retention(p3, gn_g, cos, sin):
    l = p3.shape[1]
    b = p3.shape[0] // (N_IN_SPLITS * N_HEADS)
    gw = N_HEADS * HEAD_DIM
    log_gamma = jnp.log(1.0 - 2.0 ** (-5.0 - jnp.arange(N_HEADS, dtype=F32)))
    lg = jnp.stack([log_gamma, log_gamma[::-1]], axis=1)
    lg = jnp.broadcast_to(lg[:, :, None], (N_HEADS, 2, HEAD_DIM))
    nh = HEADS_PER_STEP
    full = pl.BlockSpec((l, HEAD_DIM), lambda bb, hp: (0, 0))
    slabs = [s for split in (5, 6, 7, 8) for s in _pair_slab_specs(b, l, split)]
    return pl.pallas_call(
        functools.partial(_retention_kernel, seq_len=l),
        out_shape=jax.ShapeDtypeStruct((b, l, gw), BF16),
        grid=(b, N_HEADS // nh),
        in_specs=slabs + [full, full,
                          pl.BlockSpec((nh, 2, HEAD_DIM), lambda bb, hp: (hp, 0, 0)),
                          pl.BlockSpec((1, nh * HEAD_DIM), lambda bb, hp: (0, hp))],
        out_specs=pl.BlockSpec((1, l, nh * HEAD_DIM), lambda bb, hp: (bb, 0, hp)),
        scratch_shapes=_decay_scratch(l),
        compiler_params=_cparams(("arbitrary", "arbitrary"), VMEM_LIMIT),
        name="retention",
    )(*([p3] * (4 * nh)), cos, sin, lg, gn_g.reshape(1, gw))


def _hgrn_kernel(*refs, seq_len, layer, depth):
    nh = HEADS_PER_STEP
    ff_refs, fb_refs, i_refs, q_refs, g_refs = (refs[i * nh:(i + 1) * nh] for i in range(5))
    lbl_ref, on_ref, y_ref = refs[5 * nh:5 * nh + 3]
    sets = _head_sets(refs[5 * nh + 3:])
    l = seq_len
    n = ATTN_BLOCK
    logits = lbl_ref[...]
    log_lb, log1m_lb = [], []
    for dd in range(2):
        xs = [logits[ll * 2 + dd] for ll in range(depth)]
        mx = functools.reduce(jnp.maximum, xs)
        es = [jnp.exp(x - mx) for x in xs]
        tot = functools.reduce(lambda a, b_: a + b_, es)
        lb = jnp.zeros_like(tot)
        for ll in range(1, layer + 1):
            lb = lb + es[ll] / tot
        log_lb.append(jnp.log(lb))
        log1m_lb.append(jnp.log1p(-lb))

    def gate(fz, dd, cols):
        log_sig = jnp.minimum(fz, 0.0) - jnp.log1p(jnp.exp(-jnp.abs(fz)))
        a = jnp.broadcast_to(log_lb[dd][:, cols], fz.shape)
        bv = log1m_lb[dd][:, cols] + log_sig
        log_f = jnp.maximum(a, bv) + jnp.log1p(jnp.exp(-jnp.abs(a - bv)))
        return log_f, 1.0 - jnp.exp(log_f)

    def block_body(bi, carry):
        s0 = pl.multiple_of(bi * n, n)
        for hh in range(nh):
            cols = slice(hh * HEAD_DIM, (hh + 1) * HEAD_DIM)
            lf_f, k_f = gate(ff_refs[hh][0, pl.ds(s0, n), :], 0, cols)
            lf_b, k_b = gate(fb_refs[hh][0, pl.ds(s0, n), :], 1, cols)
            _decay_attn_block(s0, q_refs[hh][0, pl.ds(s0, n), :], k_f, k_b, i_refs[hh][0, pl.ds(s0, n), :],
                              lf_f, lf_b, HGRN_CHUNK, sets[hh])
        return carry

    lax.fori_loop(0, l // n, block_body, 0)
    _decay_attn_state_pass(HGRN_CHUNK, l, sets)

    def out_body(bi, carry):
        s0 = pl.multiple_of(bi * n, n)
        for hh in range(nh):
            cols = slice(hh * HEAD_DIM, (hh + 1) * HEAD_DIM)
            o = sets[hh][-1][pl.ds(s0, n), :]
            o = o * lax.rsqrt(jnp.mean(o * o, axis=-1, keepdims=True) + EPS) * on_ref[:, cols]
            y_ref[0, pl.ds(s0, n), cols] = (_silu(g_refs[hh][0, pl.ds(s0, n), :]) * o).astype(BF16)
        return carry

    lax.fori_loop(0, l // n, out_body, 0)


def _hgrn2(p3, lb_logits, on_g, layer):
    l = p3.shape[1]
    b = p3.shape[0] // (N_IN_SPLITS * N_HEADS)
    gw = N_HEADS * HEAD_DIM
    depth = lb_logits.shape[0]
    nh = HEADS_PER_STEP
    slabs = [s for split in (9, 10, 11, 12, 13) for s in _pair_slab_specs(b, l, split)]
    return pl.pallas_call(
        functools.partial(_hgrn_kernel, seq_len=l, layer=layer, depth=depth),
        out_shape=jax.ShapeDtypeStruct((b, l, gw), BF16),
        grid=(b, N_HEADS // nh),
        in_specs=slabs + [pl.BlockSpec((depth * 2, 1, nh * HEAD_DIM), lambda bb, hp: (0, 0, hp)),
                          pl.BlockSpec((1, nh * HEAD_DIM), lambda bb, hp: (0, hp))],
        out_specs=pl.BlockSpec((1, l, nh * HEAD_DIM), lambda bb, hp: (bb, 0, hp)),
        scratch_shapes=_decay_scratch(l),
        compiler_params=_cparams(("arbitrary", "arbitrary"), VMEM_LIMIT),
        name="hgrn2",
    )(*([p3] * (5 * nh)), lb_logits.astype(F32).reshape(depth * 2, 1, gw), on_g.reshape(1, gw))


def _topk_rows(x, k):
    r_n = x.shape[0]
    idx = lax.broadcasted_iota(jnp.int32, x.shape, 0).astype(F32)
    rank = jnp.full(x.shape, float(k), F32)
    vals = []
    for r in range(k):
        m = jnp.max(x, axis=0, keepdims=True)
        first = jnp.min(jnp.where(x == m, idx, float(r_n)), axis=0, keepdims=True)
        sel = idx == first
        rank = jnp.where(sel, float(r), rank)
        x = jnp.where(sel, -jnp.inf, x)
        vals.append(m)
    return rank, jnp.concatenate(vals, axis=0)


def _bf16_pair_words(x):
    bits = pltpu.bitcast(x.astype(BF16).astype(F32), jnp.uint32)
    return bits | (bits >> 16)


def _peer_route_kernel(zt_ref, wq_ref, sk_ref, rank2_ref, lim1_ref, e1_ref, e2_ref, *, tb):
    k = PEER_TOPK
    n_kc, _, kc = wq_ref.shape
    q = jnp.dot(wq_ref[0], zt_ref[0:kc, :], preferred_element_type=F32)
    for ci in range(1, n_kc):
        q += jnp.dot(wq_ref[ci], zt_ref[ci * kc:(ci + 1) * kc, :], preferred_element_type=F32)
    half = q.shape[0] // 2
    for c0 in range(0, tb, LANES):
        s1 = _bdot(sk_ref[0, 0], q[:half, c0:c0 + LANES])
        s2 = _bdot(sk_ref[0, 1], q[half:, c0:c0 + LANES])
        rank1, top1 = _topk_rows(s1, k)
        rank2, top2 = _topk_rows(s2, k)
        e1 = jnp.exp(s1 - top1[0:1])
        e2 = jnp.exp(s2 - top2[0:1])
        e1t = jnp.exp(top1 - top1[0:1])
        e2t = jnp.exp(top2 - top2[0:1])
        n_multi = sum(1 for r in range(k) if k // (r + 1) > 1)
        assert k - n_multi == SUBLANES
        pieces, epieces, widths = [], [], []
        for r in range(n_multi):
            n_c = k // (r + 1)
            n_pad = -(-n_c // SUBLANES) * SUBLANES
            valid = lax.broadcasted_iota(jnp.int32, (n_pad, LANES), 0) < n_c
            pieces.append(jnp.where(valid, top1[r:r + 1] + top2[0:n_pad], -jnp.inf))
            epieces.append(e1t[r:r + 1] * e2t[0:n_pad])
            widths.append(n_pad)
        pieces.append(top1[n_multi:k] + top2[0:1])
        epieces.append(e1t[n_multi:k] * e2t[0:1])
        cand = jnp.concatenate(pieces, axis=0)
        ecand = jnp.concatenate(epieces, axis=0)
        crank, _ = _topk_rows(cand, k)
        sel = crank < k
        z = jnp.sum(jnp.where(sel, ecand, 0.0), axis=0, keepdims=True)
        lim1 = jnp.zeros(rank1.shape, F32)
        self32 = sel.astype(F32)
        off = 0
        for r in range(n_multi):
            cnt = jnp.sum(self32[off:off + widths[r]], axis=0, keepdims=True)
            lim1 = jnp.where(rank1 == r, cnt, lim1)
            off += widths[r]
        for r in range(n_multi, k):
            lim1 = jnp.where(rank1 == r, self32[off + r - n_multi:off + r - n_multi + 1], lim1)
        rank2_ref[0, :, c0:c0 + LANES] = rank2.astype(BF16)
        lim1_ref[0, :, c0:c0 + LANES] = _bf16_pair_words(lim1)
        e1_ref[0, :, c0:c0 + LANES] = _bf16_pair_words(e1)
        e2_ref[0, :, c0:c0 + LANES] = (e2 / z).astype(BF16)


def _peer_route(zt, wq_t, sub_keys, tb):
    d, t = zt.shape
    n_kc, _, kc = wq_t.shape
    qd = wq_t.shape[1] // PEER_HEADS
    tabs = [jax.ShapeDtypeStruct((PEER_HEADS, PEER_NKEYS, t), dt) for dt in (BF16, jnp.uint32, jnp.uint32, BF16)]
    tab_spec = pl.BlockSpec((1, PEER_NKEYS, tb), lambda i, h: (h, 0, i))
    return pl.pallas_call(
        functools.partial(_peer_route_kernel, tb=tb),
        out_shape=tabs,
        grid=(t // tb, PEER_HEADS),
        in_specs=[pl.BlockSpec((d, tb), lambda i, h: (0, i)),
                  pl.BlockSpec((n_kc, qd, kc), lambda i, h: (0, h, 0)),
                  pl.BlockSpec((1, 2, PEER_NKEYS, qd // 2), lambda i, h: (h, 0, 0, 0))],
        out_specs=[tab_spec] * 4,
        compiler_params=_cparams(("arbitrary", "arbitrary"), VMEM_LIMIT),
        name="peer_route",
    )(zt, wq_t, sub_keys)


def _peer_expert_kernel(zt_ref, u_ref, vt_ref, rank2_ref, lim1_ref, e1_ref, e2_ref, o_ref, w_ref, *, ek):
    e = pl.program_id(1)
    tb = zt_ref.shape[1]
    n_i1 = ek // PEER_NKEYS

    @pl.when(e == 0)
    def _():
        o_ref[...] = jnp.zeros(o_ref.shape, F32)

    half = (PEER_NKEYS // 2, PEER_LANE_CHUNK)

    def gated(j, act):
        i1 = e * n_i1 + j
        for c0 in range(0, tb, PEER_LANE_CHUNK):
            cs = slice(c0, c0 + PEER_LANE_CHUNK)
            g = jnp.zeros((PEER_NKEYS, PEER_LANE_CHUNK), BF16)
            for h in range(PEER_HEADS):
                lim = pltpu.bitcast(jnp.broadcast_to(lim1_ref[h, pl.ds(i1, 1), cs], half), BF16)
                w1 = pltpu.bitcast(jnp.broadcast_to(e1_ref[h, pl.ds(i1, 1), cs], half), BF16)
                g = g + jnp.where(rank2_ref[h, :, cs] < lim, e2_ref[h, :, cs], jnp.zeros_like(g)) * w1
            w_ref[j * PEER_NKEYS:(j + 1) * PEER_NKEYS, cs] = (g.astype(F32) * _gelu(act[:, cs])).astype(BF16)

    acts = [jnp.dot(u_ref[j * PEER_NKEYS:(j + 1) * PEER_NKEYS, :], zt_ref[...], preferred_element_type=F32)
            for j in range(n_i1)]
    per_group = PEER_DOWN_GROUP // PEER_NKEYS
    for g0 in range(0, n_i1, per_group):
        for j in range(g0, g0 + per_group):
            gated(j, acts[j])
        ks = slice(g0 * PEER_NKEYS, (g0 + per_group) * PEER_NKEYS)
        o_ref[...] += jnp.dot(vt_ref[0, :, ks], w_ref[ks, :], preferred_element_type=F32)


def _peer_experts(zt, u, vt, tabs, tb, ek):
    d, t = zt.shape
    ne = u.shape[0]
    once = pl.Buffered(1)
    tab_spec = pl.BlockSpec((PEER_HEADS, PEER_NKEYS, tb), lambda i, e: (0, 0, i), pipeline_mode=once)
    return pl.pallas_call(
        functools.partial(_peer_expert_kernel, ek=ek),
        out_shape=jax.ShapeDtypeStruct((d, t), F32),
        grid=(t // tb, ne // ek),
        in_specs=[pl.BlockSpec((d, tb), lambda i, e: (0, i), pipeline_mode=once),
                  pl.BlockSpec((ek, d), lambda i, e: (e, 0)),
                  pl.BlockSpec((1, d, ek), lambda i, e: (e, 0, 0)),
                  tab_spec, tab_spec, tab_spec, tab_spec],
        out_specs=pl.BlockSpec((d, tb), lambda i, e: (0, i)),
        scratch_shapes=[pltpu.VMEM((ek, tb), BF16)],
        compiler_params=_cparams(("arbitrary", "arbitrary"), VMEM_LIMIT),
        name="peer_experts",
    )(zt, u, vt, *tabs)


def kernel(x, c, ctx, c_ctx, norm1_g, norm2_g, ada_w, ada_b, w_in, w_out, lru_conv_w, lru_conv_b, lru_wa, lru_ba,
           lru_wx, lru_bx, lru_lam, na_qn_g, na_kn_g, na_rpb, ret_gn_g, hgrn_lb_logits, hgrn_on_g, peer_wq,
           peer_subkeys, peer_u, peer_v):
    bsz, n_lat, d = x.shape
    depth = w_in.shape[0]
    l = CTX_LEN + n_lat
    gw = d // N_GROUPS
    assert bsz < MOD_ROWS and n_lat % ROW_TILE == 0 and CTX_LEN % ROW_TILE == 0
    ctx_row = bsz

    c16 = jnp.zeros((MOD_ROWS, d), F32).at[:bsz].set(c).at[ctx_row].set(c_ctx)
    cos, sin = _rope_tables(l)
    rows = n_lat // GRID_W

    h3 = jnp.concatenate([ctx, x], axis=1)
    tiles_full = l // ROW_TILE
    tiles_lat = n_lat // ROW_TILE
    ctx_tiles = CTX_LEN // ROW_TILE

    def row_full(i):
        return jnp.where(i % tiles_full < ctx_tiles, ctx_row, i // tiles_full)

    def row_lat(i):
        return i // tiles_lat

    h = h3.reshape(bsz * l, d)
    pending = None
    for layer in range(depth):
        last = layer == depth - 1
        mod3 = _ada_modulation(c16, ada_w, ada_b, layer).reshape(MOD_ROWS * N_MOD, 1, d)
        if pending is None:
            (u,) = _resid_norm(h, mod3, row_full, norm_g=norm1_g[layer], shift_k=0, scale_k=1)
        else:
            ft, mod3_prev = pending
            h, u = _resid_norm(h, mod3, row_full, ft=ft, gate_mod3=mod3_prev, gate_k=5,
                               norm_g=norm1_g[layer], shift_k=0, scale_k=1)
        p = _matmul(u, w_in, layer, 512, 1024)
        p3 = p.reshape(N_IN_SPLITS * N_HEADS * bsz, l, HEAD_DIM)
        y_a = _rglru(p3, lru_conv_w[layer], lru_conv_b[layer], lru_wa[layer], lru_ba[layer],
                     lru_wx[layer], lru_bx[layer], lru_lam[layer])
        y_b = _natten(p3, na_qn_g[layer], na_kn_g[layer], _na_bias_table(na_rpb[layer], rows))
        y_c = _retention(p3, ret_gn_g[layer], cos, sin)
        y_d = _hgrn2(p3, hgrn_lb_logits, hgrn_on_g[layer], layer)
        if last:
            hm3 = _out_proj((y_a, y_b, y_c, y_d), w_out, layer, h.reshape(bsz, l, d), mod3, row_off=ctx_tiles,
                            row_fn=lambda bb, i: bb, gate_k=2)
            row_fn = row_lat
        else:
            hm3 = _out_proj((y_a, y_b, y_c, y_d), w_out, layer, h.reshape(bsz, l, d), mod3, row_off=0,
                            row_fn=lambda bb, i: jnp.where(i < ctx_tiles, ctx_row, bb), gate_k=2)
            row_fn = row_full
        h = hm3.reshape(-1, d)
        (zt,) = _resid_norm(h, mod3, row_fn, norm_g=norm2_g[layer], shift_k=3, scale_k=4, transpose_out=True)
        wq_t = _cast_weight(peer_wq, layer, transpose=True, rows_per_step=512)
        u_b = _cast_weight(peer_u, layer, transpose=False, rows_per_step=512)
        v_t = _cast_weight(peer_v, layer, transpose=True, rows_per_step=512)
        tabs = _peer_route(zt, wq_t, peer_subkeys[layer], PEER_TOKEN_BLOCK)
        ft = _peer_experts(zt, u_b, v_t, tabs, PEER_TOKEN_BLOCK, PEER_EXPERT_BLOCK)
        if last:
            (h,) = _resid_norm(h, mod3, row_fn, ft=ft, gate_k=5)
        else:
            pending = (ft, mod3)
    return h.reshape(bsz, n_lat, d)
```

```python
import functools

import jax
import jax.numpy as jnp
from jax import lax
from jax.experimental import pallas as pl
from jax.experimental.pallas import tpu as pltpu

F32 = jnp.float32
BF16 = jnp.bfloat16

N_GROUPS = 4
N_HEADS = 8
HEAD_DIM = 128
N_IN_SPLITS = 14
N_MOD = 6
CTX_LEN = 256
GRID_W = 64
LRU_CONV_W = 4
LRU_C = 8.0
NA_WIN_ROWS = 8
NA_WIN_COLS = 16
RET_CHUNK = 128
HGRN_CHUNK = 32
PEER_HEADS = 8
PEER_NKEYS = 128
PEER_TOPK = 16
ROPE_BASE = 10000.0
EPS = 1e-6
NEG_INF = -1e30

SUBLANES = 8
LANES = 128
ROW_TILE = 256
ATTN_BLOCK = 128
MOD_ROWS = 16
PEER_TOKEN_BLOCK = 512
PEER_EXPERT_BLOCK = 512
PEER_LANE_CHUNK = 256
PEER_DOWN_GROUP = 256
VMEM_LIMIT = 56 * 1024 * 1024


def _cparams(sem, vmem=None):
    return pltpu.CompilerParams(dimension_semantics=sem, vmem_limit_bytes=vmem)


def _bdot(a, b):
    return jnp.dot(a.astype(BF16), b.astype(BF16), preferred_element_type=F32)


def _bdot_nt(a, b):
    return lax.dot_general(a.astype(BF16), b.astype(BF16), (((1,), (1,)), ((), ())),
                           preferred_element_type=F32)


def _bdot_tn(a, b):
    return lax.dot_general(a.astype(BF16), b.astype(BF16), (((0,), (0,)), ((), ())),
                           preferred_element_type=F32)


def _gelu(x):
    c = 0.7978845608028654
    return x / (1.0 + jnp.exp(x * (-2.0 * c - (2.0 * c * 0.044715) * (x * x))))


def _sigmoid(x):
    return 1.0 / (1.0 + jnp.exp(-x))


def _silu(x):
    return x * _sigmoid(x)


def _ada_kernel(c_ref, w_ref, b_ref, o_ref):
    x = c_ref[...]
    x = _silu(x)
    hi = x.astype(BF16)
    lo = (x - hi.astype(F32)).astype(BF16)
    w = w_ref[0]
    whi = w.astype(BF16)
    wlo = (w - whi.astype(F32)).astype(BF16)
    acc = jnp.dot(hi, whi, preferred_element_type=F32)
    acc += jnp.dot(hi, wlo, preferred_element_type=F32)
    acc += jnp.dot(lo, whi, preferred_element_type=F32)
    o_ref[...] = acc + b_ref[0]


def _ada_modulation(c16, w_all, b_all, layer):
    depth, d, n = w_all.shape
    tn = 512
    return pl.pallas_call(
        _ada_kernel,
        out_shape=jax.ShapeDtypeStruct((MOD_ROWS, n), F32),
        grid=(n // tn,),
        in_specs=[pl.BlockSpec((MOD_ROWS, d), lambda j: (0, 0)),
                  pl.BlockSpec((1, d, tn), lambda j: (layer, 0, j)),
                  pl.BlockSpec((1, 1, tn), lambda j: (layer, 0, j))],
        out_specs=pl.BlockSpec((MOD_ROWS, tn), lambda j: (0, j)),
        compiler_params=_cparams(("arbitrary",), VMEM_LIMIT),
        name="ada_modulation",
    )(c16, w_all, b_all.reshape(depth, 1, n))


def _resid_norm_kernel(*refs, has_f, has_norm, transpose_out):
    it = iter(refs)
    h_ref = next(it)
    if has_f:
        ft_ref = next(it)
        gate_ref = next(it)
    if has_norm:
        g_ref = next(it)
        sh_ref = next(it)
        sc_ref = next(it)
    if has_f:
        hn_ref = next(it)
    if has_norm:
        u_ref = next(it)
    x = h_ref[...]
    if has_f:
        x = x + gate_ref[0] * ft_ref[...].T
        hn_ref[...] = x
    if has_norm:
        ms = jnp.mean(x * x, axis=-1, keepdims=True)
        y = x * lax.rsqrt(ms + EPS) * g_ref[...]
        y = y * (1.0 + sc_ref[0]) + sh_ref[0]
        if transpose_out:
            u_ref[...] = y.T.astype(BF16)
        else:
            u_ref[...] = y.astype(BF16)


def _resid_norm(h, mod3, row_fn, *, ft=None, gate_mod3=None, gate_k=None, norm_g=None, shift_k=None,
                scale_k=None, transpose_out=False):
    t, d = h.shape
    nt = t // ROW_TILE
    has_f = ft is not None
    has_norm = norm_g is not None
    in_specs = [pl.BlockSpec((ROW_TILE, d), lambda i: (i, 0))]
    args = [h]
    if has_f:
        in_specs += [pl.BlockSpec((d, ROW_TILE), lambda i: (0, i)),
                     pl.BlockSpec((1, 1, d), lambda i: (row_fn(i) * N_MOD + gate_k, 0, 0))]
        args += [ft, mod3 if gate_mod3 is None else gate_mod3]
    if has_norm:
        in_specs += [pl.BlockSpec((1, d), lambda i: (0, 0)),
                     pl.BlockSpec((1, 1, d), lambda i: (row_fn(i) * N_MOD + shift_k, 0, 0)),
                     pl.BlockSpec((1, 1, d), lambda i: (row_fn(i) * N_MOD + scale_k, 0, 0))]
        args += [norm_g.reshape(1, d), mod3, mod3]
    out_shape, out_specs = [], []
    if has_f:
        out_shape.append(jax.ShapeDtypeStruct((t, d), F32))
        out_specs.append(pl.BlockSpec((ROW_TILE, d), lambda i: (i, 0)))
    if has_norm:
        if transpose_out:
            out_shape.append(jax.ShapeDtypeStruct((d, t), BF16))
            out_specs.append(pl.BlockSpec((d, ROW_TILE), lambda i: (0, i)))
        else:
            out_shape.append(jax.ShapeDtypeStruct((t, d), BF16))
            out_specs.append(pl.BlockSpec((ROW_TILE, d), lambda i: (i, 0)))
    outs = pl.pallas_call(
        functools.partial(_resid_norm_kernel, has_f=has_f, has_norm=has_norm, transpose_out=transpose_out),
        out_shape=out_shape, grid=(nt,), in_specs=in_specs, out_specs=out_specs,
        compiler_params=_cparams(("arbitrary",), VMEM_LIMIT),
        name="resid_norm",
    )(*args)
    return outs


def _mm_kernel(x_ref, w_ref, o_ref, wb_ref):
    @pl.when(pl.program_id(1) == 0)
    def _():
        wb_ref[...] = w_ref[0].astype(BF16)

    r = jnp.dot(x_ref[...], wb_ref[...], preferred_element_type=F32)
    for hh in range(o_ref.shape[0]):
        o_ref[hh] = r[:, hh * HEAD_DIM:(hh + 1) * HEAD_DIM]


def _matmul(x, w_all, layer, tm, tn):
    m, k = x.shape
    n = w_all.shape[2]
    return pl.pallas_call(
        _mm_kernel,
        out_shape=jax.ShapeDtypeStruct((n // HEAD_DIM, m, HEAD_DIM), F32),
        grid=(n // tn, m // tm),
        in_specs=[pl.BlockSpec((tm, k), lambda j, i: (i, 0)),
                  pl.BlockSpec((1, k, tn), lambda j, i: (layer, 0, j), pipeline_mode=pl.Buffered(1))],
        out_specs=pl.BlockSpec((tn // HEAD_DIM, tm, HEAD_DIM), lambda j, i: (j, i, 0)),
        scratch_shapes=[pltpu.VMEM((k, tn), BF16)],
        compiler_params=_cparams(("arbitrary", "arbitrary"), VMEM_LIMIT),
        name="in_proj",
    )(x, w_all)


def _cast_kernel(w_ref, o_ref, *, transpose):
    w = w_ref[0]
    if transpose:
        o_ref[0] = w.T.astype(BF16)
    else:
        o_ref[...] = w.astype(BF16)


def _cast_weight(w_all, layer, *, transpose, rows_per_step):
    _, r, c = w_all.shape
    tr = rows_per_step
    if transpose:
        out_shape, out_spec = (r // tr, c, tr), pl.BlockSpec((1, c, tr), lambda i: (i, 0, 0))
    else:
        out_shape, out_spec = (r, c), pl.BlockSpec((tr, c), lambda i: (i, 0))
    return pl.pallas_call(
        functools.partial(_cast_kernel, transpose=transpose),
        out_shape=jax.ShapeDtypeStruct(out_shape, BF16),
        grid=(r // tr,),
        in_specs=[pl.BlockSpec((1, tr, c), lambda i: (layer, i, 0))],
        out_specs=out_spec,
        compiler_params=_cparams(("arbitrary",), VMEM_LIMIT),
        name="cast_weight",
    )(w_all)


def _head_slab_spec(bsz, l, split):
    return pl.BlockSpec((1, l, HEAD_DIM), lambda bb, h: ((split * N_HEADS + h) * bsz + bb, 0, 0))


def _wout_kernel(ya_ref, yb_ref, yc_ref, yd_ref, w_ref, h_ref, gate_ref, o_ref, wb_ref):
    @pl.when((pl.program_id(1) == 0) & (pl.program_id(2) == 0))
    def _():
        wb_ref[...] = w_ref[0].astype(BF16)

    gw = ya_ref.shape[2]
    acc = jnp.dot(ya_ref[0], wb_ref[0 * gw:1 * gw, :], preferred_element_type=F32)
    acc += jnp.dot(yb_ref[0], wb_ref[1 * gw:2 * gw, :], preferred_element_type=F32)
    acc += jnp.dot(yc_ref[0], wb_ref[2 * gw:3 * gw, :], preferred_element_type=F32)
    acc += jnp.dot(yd_ref[0], wb_ref[3 * gw:4 * gw, :], preferred_element_type=F32)
    o_ref[0] = h_ref[0] + gate_ref[0] * acc


def _out_proj(ys, w_all, layer, h3, mod3, *, row_off, row_fn, gate_k):
    b, l, gw = ys[0].shape
    d = w_all.shape[2]
    nrt = l // ROW_TILE - row_off
    tn = 1024
    y_spec = pl.BlockSpec((1, ROW_TILE, gw), lambda j, bb, i: (bb, i + row_off, 0))
    return pl.pallas_call(
        _wout_kernel,
        out_shape=jax.ShapeDtypeStruct((b, nrt * ROW_TILE, d), F32),
        grid=(d // tn, b, nrt),
        in_specs=[y_spec, y_spec, y_spec, y_spec,
                  pl.BlockSpec((1, N_GROUPS * gw, tn), lambda j, bb, i: (layer, 0, j),
                               pipeline_mode=pl.Buffered(1)),
                  pl.BlockSpec((1, ROW_TILE, tn), lambda j, bb, i: (bb, i + row_off, j)),
                  pl.BlockSpec((1, 1, tn), lambda j, bb, i: (row_fn(bb, i) * N_MOD + gate_k, 0, j))],
        out_specs=pl.BlockSpec((1, ROW_TILE, tn), lambda j, bb, i: (bb, i, j)),
        scratch_shapes=[pltpu.VMEM((N_GROUPS * gw, tn), BF16)],
        compiler_params=_cparams(("arbitrary", "arbitrary", "arbitrary"), VMEM_LIMIT),
        name="out_proj",
    )(*ys, w_all, h3, mod3)


def _group_cumsum(x, rowmod, c, reverse):
    n = x.shape[0]
    k = 1
    while k < c:
        if reverse:
            x = x + jnp.where(rowmod < c - k, pltpu.roll(x, n - k, 0), 0.0)
        else:
            x = x + jnp.where(rowmod >= k, pltpu.roll(x, k, 0), 0.0)
        k *= 2
    return x


def _group_linear_scan(a, u, rowmod, c, reverse):
    n = a.shape[0]
    k = 1
    while k < c:
        if reverse:
            ok = rowmod < c - k
            a_s = pltpu.roll(a, n - k, 0)
            u_s = pltpu.roll(u, n - k, 0)
        else:
            ok = rowmod >= k
            a_s = pltpu.roll(a, k, 0)
            u_s = pltpu.roll(u, k, 0)
        u = jnp.where(ok, a * u_s + u, u)
        a = jnp.where(ok, a * a_s, a)
        k *= 2
    return a, u


def _rglru_kernel(x_ref, g_ref, cw_ref, cb_ref, wa_ref, ba_ref, wx_ref, bx_ref, lam_ref, y_ref,
                  xp_ref, af_ref, uf_ref, ab_ref, ub_ref, *, seq_len):
    l = seq_len
    pad = SUBLANES
    nblk = l // ROW_TILE
    zeros_pad = jnp.zeros((pad, LANES), F32)
    xp_ref[0:pad, :] = zeros_pad
    xp_ref[pad + l:pad + l + pad, :] = zeros_pad
    xp_ref[pad:pad + l, :] = x_ref[0]

    row = lax.broadcasted_iota(jnp.int32, (ROW_TILE, LANES), 0)
    rowmod = row % SUBLANES
    cw = cw_ref[...]
    cb = cb_ref[...]
    lam = lam_ref[...]
    sp = [jnp.maximum(-lam[dd], 0.0) + jnp.log1p(jnp.exp(-jnp.abs(lam[dd]))) for dd in range(2)]

    for blk in range(nblk):
        s0 = blk * ROW_TILE
        taps = []
        for tap in range(LRU_CONV_W):
            off = tap - LRU_CONV_W // 2
            v = xp_ref[pad + s0 + off:pad + s0 + off + ROW_TILE, :]
            if blk == 0 and off > 0:
                v = jnp.where(row + off < CTX_LEN, v, 0.0)
            if s0 == CTX_LEN and off < 0:
                v = jnp.where(row + off >= 0, v, 0.0)
            taps.append(v)
        xc = cb + taps[0] * cw[0:1]
        for tap in range(1, LRU_CONV_W):
            xc = xc + taps[tap] * cw[tap:tap + 1]
        xcb = xc.astype(BF16)
        for dd, (a_ref, u_ref) in enumerate(((af_ref, uf_ref), (ab_ref, ub_ref))):
            r = _sigmoid(jnp.dot(xcb, wa_ref[dd, 0].astype(BF16), preferred_element_type=F32) + ba_ref[dd])
            i = _sigmoid(jnp.dot(xcb, wx_ref[dd, 0].astype(BF16), preferred_element_type=F32) + bx_ref[dd])
            log_a = -LRU_C * r * sp[dd]
            a = jnp.exp(log_a)
            u = jnp.sqrt(1.0 - jnp.exp(2.0 * log_a)) * (i * xc)
            a_loc, u_loc = _group_linear_scan(a, u, rowmod, SUBLANES, reverse=(dd == 1))
            a_ref[s0:s0 + ROW_TILE, :] = a_loc
            u_ref[s0:s0 + ROW_TILE, :] = u_loc

    n_chunks = l // SUBLANES
    n_ctx_chunks = CTX_LEN // SUBLANES

    def fwd_body(c, carry):
        r0 = pl.multiple_of(c * SUBLANES, SUBLANES)
        hh = af_ref[pl.ds(r0, SUBLANES), :] * carry + uf_ref[pl.ds(r0, SUBLANES), :]
        uf_ref[pl.ds(r0, SUBLANES), :] = hh
        return jnp.broadcast_to(hh[SUBLANES - 1:SUBLANES, :], (SUBLANES, LANES))

    lax.fori_loop(0, n_chunks, fwd_body, jnp.zeros((SUBLANES, LANES), F32))

    def bwd_body(j, carry):
        c = jnp.where(j < n_ctx_chunks, n_ctx_chunks - 1 - j, n_chunks - 1 - (j - n_ctx_chunks))
        r0 = pl.multiple_of(c * SUBLANES, SUBLANES)
        hh = ab_ref[pl.ds(r0, SUBLANES), :] * carry + ub_ref[pl.ds(r0, SUBLANES), :]
        ub_ref[pl.ds(r0, SUBLANES), :] = hh
        return jnp.broadcast_to(hh[0:1, :], (SUBLANES, LANES))

    lax.fori_loop(0, n_chunks, bwd_body, jnp.zeros((SUBLANES, LANES), F32))

    for blk in range(nblk):
        s0 = blk * ROW_TILE
        hsum = uf_ref[s0:s0 + ROW_TILE, :] + ub_ref[s0:s0 + ROW_TILE, :]
        y_ref[0, s0:s0 + ROW_TILE, :] = (_gelu(g_ref[0, s0:s0 + ROW_TILE, :]) * hsum).astype(BF16)


def _rglru(p3, conv_w, conv_b, wa, ba, wx, bx, lam):
    l = p3.shape[1]
    b = p3.shape[0] // (N_IN_SPLITS * N_HEADS)
    gw = N_HEADS * HEAD_DIM
    vec = lambda a: a.reshape(2, 1, gw)
    spec_vec = pl.BlockSpec((2, 1, HEAD_DIM), lambda bb, h: (0, 0, h))
    spec_w = pl.BlockSpec((2, 1, HEAD_DIM, HEAD_DIM), lambda bb, h: (0, h, 0, 0))
    return pl.pallas_call(
        functools.partial(_rglru_kernel, seq_len=l),
        out_shape=jax.ShapeDtypeStruct((b, l, gw), BF16),
        grid=(b, N_HEADS),
        in_specs=[_head_slab_spec(b, l, 0), _head_slab_spec(b, l, 1),
                  pl.BlockSpec((LRU_CONV_W, HEAD_DIM), lambda bb, h: (0, h)),
                  pl.BlockSpec((1, HEAD_DIM), lambda bb, h: (0, h)),
                  spec_w, spec_vec, spec_w, spec_vec, spec_vec],
        out_specs=pl.BlockSpec((1, l, HEAD_DIM), lambda bb, h: (bb, 0, h)),
        scratch_shapes=[pltpu.VMEM((l + 2 * SUBLANES, LANES), F32)] + [pltpu.VMEM((l, LANES), F32)] * 4,
        compiler_params=_cparams(("arbitrary", "arbitrary"), VMEM_LIMIT),
        name="rglru",
    )(p3, p3, conv_w, conv_b.reshape(1, gw), wa, vec(ba), wx, vec(bx), vec(lam))


def _na_kernel(*refs, seq_len):
    nh = HEADS_PER_STEP
    q_refs, k_refs, v_refs = (refs[i * nh:(i + 1) * nh] for i in range(3))
    qg_ref, kg_ref, bias_ref, y_ref = refs[3 * nh:3 * nh + 4]
    scratch = refs[3 * nh + 4:]
    qb_refs, kb_refs, vb_refs = (scratch[i * nh:(i + 1) * nh] for i in range(3))
    l = seq_len
    scale = HEAD_DIM ** -0.5
    rows = (l - CTX_LEN) // GRID_W
    kr = min(NA_WIN_ROWS, rows)
    n_loc = kr * GRID_W

    def headnorm(z, g):
        return z * lax.rsqrt(jnp.mean(z * z, axis=-1, keepdims=True) + EPS) * g

    for hh in range(nh):
        cols = slice(hh * HEAD_DIM, (hh + 1) * HEAD_DIM)
        for s0 in range(0, l, ROW_TILE):
            rs = slice(s0, s0 + ROW_TILE)
            qb_refs[hh][rs, :] = headnorm(q_refs[hh][0, rs, :], qg_ref[...]).astype(BF16)
            kb_refs[hh][rs, :] = headnorm(k_refs[hh][0, rs, :], kg_ref[...]).astype(BF16)
            vb_refs[hh][rs, :] = v_refs[hh][0, rs, :].astype(BF16)
        k_ctx = kb_refs[hh][0:CTX_LEN, :]
        s_c = _bdot_nt(qb_refs[hh][0:CTX_LEN, :], k_ctx) * scale
        e_c = jnp.exp(s_c - jnp.max(s_c, axis=-1, keepdims=True))
        p_c = e_c / jnp.sum(e_c, axis=-1, keepdims=True)
        y_ref[0, 0:CTX_LEN, cols] = _bdot(p_c, vb_refs[hh][0:CTX_LEN, :]).astype(BF16)

    def scores(r, hh):
        r0 = jnp.clip(r - kr // 2, 0, rows - kr)
        qs = pl.multiple_of(CTX_LEN + r * GRID_W, GRID_W)
        ks = pl.multiple_of(CTX_LEN + r0 * GRID_W, GRID_W)
        q_r = qb_refs[hh][pl.ds(qs, GRID_W), :]
        bias = bias_ref[hh, r0 - r + NA_WIN_ROWS - 1]
        s_loc = _bdot_nt(q_r, kb_refs[hh][pl.ds(ks, n_loc), :]) * scale + bias
        s_ctx = _bdot_nt(q_r, kb_refs[hh][0:CTX_LEN, :]) * scale
        return hh, qs, ks, s_loc, s_ctx

    def softmax(hh, qs, ks, s_loc, s_ctx):
        m = jnp.maximum(jnp.max(s_loc, axis=-1, keepdims=True), jnp.max(s_ctx, axis=-1, keepdims=True))
        e_loc = jnp.exp(s_loc - m)
        e_ctx = jnp.exp(s_ctx - m)
        inv = 1.0 / (jnp.sum(e_loc, axis=-1, keepdims=True) + jnp.sum(e_ctx, axis=-1, keepdims=True))
        return hh, qs, ks, (e_loc * inv).astype(BF16), (e_ctx * inv).astype(BF16)

    def readout(hh, qs, ks, p_loc, p_ctx):
        o = _bdot(p_loc, vb_refs[hh][pl.ds(ks, n_loc), :]) + _bdot(p_ctx, vb_refs[hh][0:CTX_LEN, :])
        y_ref[0, pl.ds(qs, GRID_W), hh * HEAD_DIM:(hh + 1) * HEAD_DIM] = o.astype(BF16)

    group = 4 if rows % 4 == 0 else 1

    def rows_body(gi, carry):
        chains = [scores(gi * group + t, hh) for t in range(group) for hh in range(nh)]
        chains = [softmax(*ch) for ch in chains]
        for ch in chains:
            readout(*ch)
        return carry

    lax.fori_loop(0, rows // group, rows_body, 0)


def _na_bias_table(rpb, rows):
    kr = min(NA_WIN_ROWS, rows)
    qc = jnp.arange(GRID_W)[:, None]
    kc = jnp.arange(GRID_W)[None, :]
    win0 = jnp.clip(qc - NA_WIN_COLS // 2, 0, GRID_W - NA_WIN_COLS)
    col_ok = (kc >= win0) & (kc < win0 + NA_WIN_COLS)
    rel_c = jnp.clip(kc - qc + NA_WIN_COLS - 1, 0, 2 * NA_WIN_COLS - 2)
    t = jnp.where(col_ok[None, None], rpb.astype(F32)[:, :, rel_c], NEG_INF)
    d = jnp.arange(NA_WIN_ROWS)[:, None] + jnp.arange(kr)[None, :]
    d = jnp.clip(d, 0, 2 * NA_WIN_ROWS - 2)
    tb = t[:, d]
    return tb.transpose(0, 1, 3, 2, 4).reshape(rpb.shape[0], NA_WIN_ROWS, GRID_W, kr * GRID_W)


def _natten(p3, qn_g, kn_g, bias_tab):
    l = p3.shape[1]
    b = p3.shape[0] // (N_IN_SPLITS * N_HEADS)
    gw = N_HEADS * HEAD_DIM
    n_loc = bias_tab.shape[-1]
    nh = HEADS_PER_STEP
    slabs = [s for split in (2, 3, 4) for s in _pair_slab_specs(b, l, split)]
    return pl.pallas_call(
        functools.partial(_na_kernel, seq_len=l),
        out_shape=jax.ShapeDtypeStruct((b, l, gw), BF16),
        grid=(b, N_HEADS // nh),
        in_specs=slabs + [pl.BlockSpec((1, HEAD_DIM), lambda bb, hp: (0, 0)),
                          pl.BlockSpec((1, HEAD_DIM), lambda bb, hp: (0, 0)),
                          pl.BlockSpec((nh, NA_WIN_ROWS, GRID_W, n_loc), lambda bb, hp: (hp, 0, 0, 0))],
        out_specs=pl.BlockSpec((1, l, nh * HEAD_DIM), lambda bb, hp: (bb, 0, hp)),
        scratch_shapes=[pltpu.VMEM((l, LANES), BF16)] * (3 * nh),
        compiler_params=_cparams(("arbitrary", "arbitrary"), VMEM_LIMIT),
        name="natten",
    )(*([p3] * (3 * nh)), qn_g.reshape(1, HEAD_DIM), kn_g.reshape(1, HEAD_DIM), bias_tab)


def _dir_cums(lf, rowmod, c, reverse):
    if lf.shape[0] == 1:
        n_inc = (c - rowmod) if reverse else (rowmod + 1)
        n_oth = rowmod if reverse else (c - 1 - rowmod)
        inc = n_inc.astype(F32) * lf
        oth = n_oth.astype(F32) * lf
    else:
        n = lf.shape[0]
        row = lax.broadcasted_iota(jnp.int32, (n, n), 0)
        col = lax.broadcasted_iota(jnp.int32, (n, n), 1)
        same = (row // c) == (col // c)
        own = same & ((col >= row) if reverse else (col <= row))
        other = same & ((col < row) if reverse else (col > row))
        tri = jnp.concatenate([own, other], axis=0).astype(F32).astype(BF16)
        hi = lf.astype(BF16)
        r1 = lf - hi.astype(F32)
        mid = r1.astype(BF16)
        lo = (r1 - mid.astype(F32)).astype(BF16)
        acc = jnp.dot(tri, hi, preferred_element_type=F32)
        acc += jnp.dot(tri, mid, preferred_element_type=F32)
        acc += jnp.dot(tri, lo, preferred_element_type=F32)
        inc, oth = acc[:n], acc[n:]
    return inc, oth


def _decay_attn_blocks(s0, heads, c, ref_sets):
    n = ATTN_BLOCK
    row = lax.broadcasted_iota(jnp.int32, (n, n), 0)
    col = lax.broadcasted_iota(jnp.int32, (n, n), 1)
    rowmod = row % c
    same = (row // c) == (col // c)
    cums = [[_dir_cums(lf, rowmod, c, reverse) for reverse, lf in ((False, lf_f), (True, lf_b))]
            for (_, _, _, _, lf_f, lf_b) in heads]
    pending = []
    for hd, ((q, k_f, k_b, v, _, _), refs) in enumerate(zip(heads, ref_sets)):
        (qdf_ref, ksf_ref, ktf_ref, decf_ref, qdb_ref, ksb_ref, ktb_ref, decb_ref, vb_ref, o_ref) = refs
        vb = v.astype(BF16)
        vb_ref[pl.ds(s0, n), :] = vb
        scores = []
        for reverse, kk, (qd_ref, ks_ref, kt_ref, dec_ref) in (
                (False, k_f, (qdf_ref, ksf_ref, ktf_ref, decf_ref)),
                (True, k_b, (qdb_ref, ksb_ref, ktb_ref, decb_ref))):
            inc, oth = cums[hd][int(reverse)]
            kt = kk * jnp.exp(oth)
            dec = jnp.exp(inc + oth)
            qd = (q * jnp.exp(inc)).astype(BF16)
            ks = (kk * jnp.exp(-inc)).astype(BF16)
            qd_ref[pl.ds(s0, n), :] = qd
            ks_ref[pl.ds(s0, n), :] = ks
            kt_ref[pl.ds(s0, n), :] = kt.astype(BF16)
            dec_ref[pl.ds(s0, n), :] = dec
            scores.append((reverse, _bdot_nt(qd, ks)))
        pending.append((scores, vb, o_ref))
    for scores, vb, o_ref in pending:
        o = jnp.zeros((n, HEAD_DIM), F32)
        for reverse, s in scores:
            keep = same & ((col >= row) if reverse else (col <= row))
            o = o + jnp.dot(jnp.where(keep, s, 0.0).astype(BF16), vb, preferred_element_type=F32)
        o_ref[pl.ds(s0, n), :] = o


def _decay_attn_state_pass(c, seq_len, ref_sets):
    n_chunks = seq_len // c
    n_ctx_chunks = CTX_LEN // c

    steps = 4 if n_chunks % 4 == 0 else 3 if n_chunks % 3 == 0 else 1

    def body(jj, carry):
        chains = []
        for hh, refs in enumerate(ref_sets):
            (qdf_ref, _, ktf_ref, decf_ref, qdb_ref, _, ktb_ref, decb_ref, vb_ref, o_ref) = refs
            chains.append((qdf_ref, ktf_ref, decf_ref, vb_ref, o_ref, False))
            chains.append((qdb_ref, ktb_ref, decb_ref, vb_ref, o_ref, True))
        starts = []
        for t in range(steps):
            j = jj * steps + t
            jb = jnp.where(j < n_ctx_chunks, n_ctx_chunks - 1 - j, n_chunks - 1 - (j - n_ctx_chunks))
            starts.append((pl.multiple_of(j * c, c), pl.multiple_of(jb * c, c)))
        upds = [[_bdot_tn(vb_ref[pl.ds(starts[t][rev], c), :], kt_ref[pl.ds(starts[t][rev], c), :])
                 for t in range(steps)]
                for (_, kt_ref, _, vb_ref, _, rev) in chains]
        states = list(carry)
        for t in range(steps):
            for ci, (qd_ref, _, dec_ref, _, o_ref, rev) in enumerate(chains):
                r0 = starts[t][rev]
                o_ref[pl.ds(r0, c), :] += _bdot_nt(qd_ref[pl.ds(r0, c), :], states[ci])
                states[ci] = states[ci] * dec_ref[pl.ds(r0, 1), :] + upds[ci][t]
        return tuple(states)

    zero = jnp.zeros((HEAD_DIM, HEAD_DIM), F32)
    lax.fori_loop(0, n_chunks // steps, body, (zero,) * (2 * len(ref_sets)))


DECAY_SCRATCH_PER_HEAD = 10
HEADS_PER_STEP = 2


def _decay_scratch(l):
    one_dir = [pltpu.VMEM((l, LANES), BF16)] * 3 + [pltpu.VMEM((l, LANES), F32)]
    one_head = one_dir + one_dir + [pltpu.VMEM((l, LANES), BF16), pltpu.VMEM((l, LANES), F32)]
    assert len(one_head) == DECAY_SCRATCH_PER_HEAD
    return one_head * HEADS_PER_STEP


def _head_sets(scratch):
    n = DECAY_SCRATCH_PER_HEAD
    return [tuple(scratch[hh * n:(hh + 1) * n]) for hh in range(HEADS_PER_STEP)]


def _pair_slab_specs(bsz, l, split):
    return [pl.BlockSpec((1, l, HEAD_DIM),
                         lambda bb, hp, hh=hh: ((split * N_HEADS + hp * HEADS_PER_STEP + hh) * bsz + bb, 0, 0))
            for hh in range(HEADS_PER_STEP)]


def _retention_kernel(*refs, seq_len):
    nh = HEADS_PER_STEP
    q_refs, k_refs, v_refs, g_refs = (refs[i * nh:(i + 1) * nh] for i in range(4))
    cos_ref, sin_ref, lg_ref, gn_ref, y_ref = refs[4 * nh:4 * nh + 5]
    sets = _head_sets(refs[4 * nh + 5:])
    l = seq_len
    n = ATTN_BLOCK
    lane = lax.broadcasted_iota(jnp.int32, (n, HEAD_DIM), 1)
    first = (lane % (HEAD_DIM // 2)) < (HEAD_DIM // 4)

    def rope(z, cos, sin):
        swapped = jnp.where(first, pltpu.roll(z, HEAD_DIM - HEAD_DIM // 4, 1), pltpu.roll(z, HEAD_DIM // 4, 1))
        return z * cos + swapped * sin

    def block_body(bi, carry):
        s0 = pl.multiple_of(bi * n, n)
        cos = cos_ref[pl.ds(s0, n), :]
        sin = sin_ref[pl.ds(s0, n), :]
        heads = []
        for hh in range(nh):
            lg = lg_ref[hh]
            q = rope(q_refs[hh][0, pl.ds(s0, n), :], cos, sin)
            k = rope(k_refs[hh][0, pl.ds(s0, n), :], cos, sin) * (HEAD_DIM ** -0.5)
            heads.append((q, k, k, v_refs[hh][0, pl.ds(s0, n), :], lg[0:1], lg[1:2]))
        _decay_attn_blocks(s0, heads, RET_CHUNK, sets)
        return carry

    lax.fori_loop(0, l // n, block_body, 0)
    _decay_attn_state_pass(RET_CHUNK, l, sets)

    def out_body(bi, carry):
        s0 = pl.multiple_of(bi * n, n)
        for hh in range(nh):
            cols = slice(hh * HEAD_DIM, (hh + 1) * HEAD_DIM)
            o = sets[hh][-1][pl.ds(s0, n), :]
            o = o - jnp.mean(o, axis=-1, keepdims=True)
            o = o * lax.rsqrt(jnp.mean(o * o, axis=-1, keepdims=True) + EPS) * gn_ref[:, cols]
            y_ref[0, pl.ds(s0, n), cols] = (_silu(g_refs[hh][0, pl.ds(s0, n), :]) * o).astype(BF16)
        return carry

    lax.fori_loop(0, l // n, out_body, 0)


def _rope_tables(l):
    n_lat = l - CTX_LEN
    quarter = HEAD_DIM // 4
    t = jnp.arange(n_lat)
    inv_freq = ROPE_BASE ** (-jnp.arange(quarter, dtype=F32) / quarter)
    ang_r = (t // GRID_W).astype(F32)[:, None] * inv_freq[None, :]
    ang_c = (t % GRID_W).astype(F32)[:, None] * inv_freq[None, :]
    cos = jnp.concatenate([jnp.cos(ang_r), jnp.cos(ang_r), jnp.cos(ang_c), jnp.cos(ang_c)], axis=-1)
    sin = jnp.concatenate([-jnp.sin(ang_r), jnp.sin(ang_r), -jnp.sin(ang_c), jnp.sin(ang_c)], axis=-1)
    cos = jnp.concatenate([jnp.ones((CTX_LEN, HEAD_DIM), F32), cos], axis=0)
    sin = jnp.concatenate([jnp.zeros((CTX_LEN, HEAD_DIM), F32), sin], axis=0)
    return cos, sin


def _retention(p3, gn_g, cos, sin):
    l = p3.shape[1]
    b = p3.shape[0] // (N_IN_SPLITS * N_HEADS)
    gw = N_HEADS * HEAD_DIM
    log_gamma = jnp.log(1.0 - 2.0 ** (-5.0 - jnp.arange(N_HEADS, dtype=F32)))
    lg = jnp.stack([log_gamma, log_gamma[::-1]], axis=1)
    lg = jnp.broadcast_to(lg[:, :, None], (N_HEADS, 2, HEAD_DIM))
    nh = HEADS_PER_STEP
    full = pl.BlockSpec((l, HEAD_DIM), lambda bb, hp: (0, 0))
    slabs = [s for split in (5, 6, 7, 8) for s in _pair_slab_specs(b, l, split)]
    return pl.pallas_call(
        functools.partial(_retention_kernel, seq_len=l),
        out_shape=jax.ShapeDtypeStruct((b, l, gw), BF16),
        grid=(b, N_HEADS // nh),
        in_specs=slabs + [full, full,
                          pl.BlockSpec((nh, 2, HEAD_DIM), lambda bb, hp: (hp, 0, 0)),
                          pl.BlockSpec((1, nh * HEAD_DIM), lambda bb, hp: (0, hp))],
        out_specs=pl.BlockSpec((1, l, nh * HEAD_DIM), lambda bb, hp: (bb, 0, hp)),
        scratch_shapes=_decay_scratch(l),
        compiler_params=_cparams(("arbitrary", "arbitrary"), VMEM_LIMIT),
        name="retention",
    )(*([p3] * (4 * nh)), cos, sin, lg, gn_g.reshape(1, gw))


def _hgrn_kernel(*refs, seq_len, layer, depth):
    nh = HEADS_PER_STEP
    ff_refs, fb_refs, i_refs, q_refs, g_refs = (refs[i * nh:(i + 1) * nh] for i in range(5))
    lbl_ref, on_ref, y_ref = refs[5 * nh:5 * nh + 3]
    sets = _head_sets(refs[5 * nh + 3:])
    l = seq_len
    n = ATTN_BLOCK
    logits = lbl_ref[...]
    log_lb, log1m_lb = [], []
    for dd in range(2):
        xs = [logits[ll * 2 + dd] for ll in range(depth)]
        mx = functools.reduce(jnp.maximum, xs)
        es = [jnp.exp(x - mx) for x in xs]
        tot = functools.reduce(lambda a, b_: a + b_, es)
        lb = jnp.zeros_like(tot)
        for ll in range(1, layer + 1):
            lb = lb + es[ll] / tot
        log_lb.append(jnp.log(lb))
        log1m_lb.append(jnp.log1p(-lb))

    def gate(fz, dd, cols):
        log_sig = jnp.minimum(fz, 0.0) - jnp.log1p(jnp.exp(-jnp.abs(fz)))
        a = jnp.broadcast_to(log_lb[dd][:, cols], fz.shape)
        bv = log1m_lb[dd][:, cols] + log_sig
        log_f = jnp.maximum(a, bv) + jnp.log1p(jnp.exp(-jnp.abs(a - bv)))
        return log_f, 1.0 - jnp.exp(log_f)

    def block_body(bi, carry):
        s0 = pl.multiple_of(bi * n, n)
        heads = []
        for hh in range(nh):
            cols = slice(hh * HEAD_DIM, (hh + 1) * HEAD_DIM)
            lf_f, k_f = gate(ff_refs[hh][0, pl.ds(s0, n), :], 0, cols)
            lf_b, k_b = gate(fb_refs[hh][0, pl.ds(s0, n), :], 1, cols)
            heads.append((q_refs[hh][0, pl.ds(s0, n), :], k_f, k_b, i_refs[hh][0, pl.ds(s0, n), :], lf_f, lf_b))
        _decay_attn_blocks(s0, heads, HGRN_CHUNK, sets)
        return carry

    lax.fori_loop(0, l // n, block_body, 0)
    _decay_attn_state_pass(HGRN_CHUNK, l, sets)

    def out_body(bi, carry):
        s0 = pl.multiple_of(bi * n, n)
        for hh in range(nh):
            cols = slice(hh * HEAD_DIM, (hh + 1) * HEAD_DIM)
            o = sets[hh][-1][pl.ds(s0, n), :]
            o = o * lax.rsqrt(jnp.mean(o * o, axis=-1, keepdims=True) + EPS) * on_ref[:, cols]
            y_ref[0, pl.ds(s0, n), cols] = (_silu(g_refs[hh][0, pl.ds(s0, n), :]) * o).astype(BF16)
        return carry

    lax.fori_loop(0, l // n, out_body, 0)


def _hgrn2(p3, lb_logits, on_g, layer):
    l = p3.shape[1]
    b = p3.shape[0] // (N_IN_SPLITS * N_HEADS)
    gw = N_HEADS * HEAD_DIM
    depth = lb_logits.shape[0]
    nh = HEADS_PER_STEP
    slabs = [s for split in (9, 10, 11, 12, 13) for s in _pair_slab_specs(b, l, split)]
    return pl.pallas_call(
        functools.partial(_hgrn_kernel, seq_len=l, layer=layer, depth=depth),
        out_shape=jax.ShapeDtypeStruct((b, l, gw), BF16),
        grid=(b, N_HEADS // nh),
        in_specs=slabs + [pl.BlockSpec((depth * 2, 1, nh * HEAD_DIM), lambda bb, hp: (0, 0, hp)),
                          pl.BlockSpec((1, nh * HEAD_DIM), lambda bb, hp: (0, hp))],
        out_specs=pl.BlockSpec((1, l, nh * HEAD_DIM), lambda bb, hp: (bb, 0, hp)),
        scratch_shapes=_decay_scratch(l),
        compiler_params=_cparams(("arbitrary", "arbitrary"), VMEM_LIMIT),
        name="hgrn2",
    )(*([p3] * (5 * nh)), lb_logits.astype(F32).reshape(depth * 2, 1, gw), on_g.reshape(1, gw))


def _topk_rows(x, k):
    r_n = x.shape[0]
    idx = lax.broadcasted_iota(jnp.int32, x.shape, 0).astype(F32)
    rank = jnp.full(x.shape, float(k), F32)
    vals = []
    for r in range(k):
        m = jnp.max(x, axis=0, keepdims=True)
        first = jnp.min(jnp.where(x == m, idx, float(r_n)), axis=0, keepdims=True)
        sel = idx == first
        rank = jnp.where(sel, float(r), rank)
        x = jnp.where(sel, -jnp.inf, x)
        vals.append(m)
    return rank, jnp.concatenate(vals, axis=0)


def _bf16_pair_words(x):
    bits = pltpu.bitcast(x.astype(BF16).astype(F32), jnp.uint32)
    return bits | (bits >> 16)


def _peer_route_kernel(zt_ref, wq_ref, sk_ref, rank2_ref, lim1_ref, e1_ref, e2_ref, *, tb):
    k = PEER_TOPK
    n_kc, _, kc = wq_ref.shape
    q = jnp.dot(wq_ref[0], zt_ref[0:kc, :], preferred_element_type=F32)
    for ci in range(1, n_kc):
        q += jnp.dot(wq_ref[ci], zt_ref[ci * kc:(ci + 1) * kc, :], preferred_element_type=F32)
    half = q.shape[0] // 2
    for c0 in range(0, tb, LANES):
        s1 = _bdot(sk_ref[0, 0], q[:half, c0:c0 + LANES])
        s2 = _bdot(sk_ref[0, 1], q[half:, c0:c0 + LANES])
        rank1, top1 = _topk_rows(s1, k)
        rank2, top2 = _topk_rows(s2, k)
        e1 = jnp.exp(s1 - top1[0:1])
        e2 = jnp.exp(s2 - top2[0:1])
        e1t = jnp.exp(top1 - top1[0:1])
        e2t = jnp.exp(top2 - top2[0:1])
        n_multi = sum(1 for r in range(k) if k // (r + 1) > 1)
        assert k - n_multi == SUBLANES
        pieces, epieces, widths = [], [], []
        for r in range(n_multi):
            n_c = k // (r + 1)
            n_pad = -(-n_c // SUBLANES) * SUBLANES
            valid = lax.broadcasted_iota(jnp.int32, (n_pad, LANES), 0) < n_c
            pieces.append(jnp.where(valid, top1[r:r + 1] + top2[0:n_pad], -jnp.inf))
            epieces.append(e1t[r:r + 1] * e2t[0:n_pad])
            widths.append(n_pad)
        pieces.append(top1[n_multi:k] + top2[0:1])
        epieces.append(e1t[n_multi:k] * e2t[0:1])
        cand = jnp.concatenate(pieces, axis=0)
        ecand = jnp.concatenate(epieces, axis=0)
        crank, _ = _topk_rows(cand, k)
        sel = crank < k
        z = jnp.sum(jnp.where(sel, ecand, 0.0), axis=0, keepdims=True)
        lim1 = jnp.zeros(rank1.shape, F32)
        self32 = sel.astype(F32)
        off = 0
        for r in range(n_multi):
            cnt = jnp.sum(self32[off:off + widths[r]], axis=0, keepdims=True)
            lim1 = jnp.where(rank1 == r, cnt, lim1)
            off += widths[r]
        for r in range(n_multi, k):
            lim1 = jnp.where(rank1 == r, self32[off + r - n_multi:off + r - n_multi + 1], lim1)
        rank2_ref[0, :, c0:c0 + LANES] = rank2.astype(BF16)
        lim1_ref[0, :, c0:c0 + LANES] = _bf16_pair_words(lim1)
        e1_ref[0, :, c0:c0 + LANES] = _bf16_pair_words(e1)
        e2_ref[0, :, c0:c0 + LANES] = (e2 / z).astype(BF16)


def _peer_route(zt, wq_t, sub_keys, tb):
    d, t = zt.shape
    n_kc, _, kc = wq_t.shape
    qd = wq_t.shape[1] // PEER_HEADS
    tabs = [jax.ShapeDtypeStruct((PEER_HEADS, PEER_NKEYS, t), dt) for dt in (BF16, jnp.uint32, jnp.uint32, BF16)]
    tab_spec = pl.BlockSpec((1, PEER_NKEYS, tb), lambda i, h: (h, 0, i))
    return pl.pallas_call(
        functools.partial(_peer_route_kernel, tb=tb),
        out_shape=tabs,
        grid=(t // tb, PEER_HEADS),
        in_specs=[pl.BlockSpec((d, tb), lambda i, h: (0, i)),
                  pl.BlockSpec((n_kc, qd, kc), lambda i, h: (0, h, 0)),
                  pl.BlockSpec((1, 2, PEER_NKEYS, qd // 2), lambda i, h: (h, 0, 0, 0))],
        out_specs=[tab_spec] * 4,
        compiler_params=_cparams(("arbitrary", "arbitrary"), VMEM_LIMIT),
        name="peer_route",
    )(zt, wq_t, sub_keys)


def _peer_expert_kernel(zt_ref, u_ref, vt_ref, rank2_ref, lim1_ref, e1_ref, e2_ref, o_ref, w_ref, *, ek):
    e = pl.program_id(1)
    tb = zt_ref.shape[1]
    n_i1 = ek // PEER_NKEYS

    @pl.when(e == 0)
    def _():
        o_ref[...] = jnp.zeros(o_ref.shape, F32)

    half = (PEER_NKEYS // 2, PEER_LANE_CHUNK)

    def gated(j, act):
        i1 = e * n_i1 + j
        for c0 in range(0, tb, PEER_LANE_CHUNK):
            cs = slice(c0, c0 + PEER_LANE_CHUNK)
            g = jnp.zeros((PEER_NKEYS, PEER_LANE_CHUNK), BF16)
            for h in range(PEER_HEADS):
                lim = pltpu.bitcast(jnp.broadcast_to(lim1_ref[h, pl.ds(i1, 1), cs], half), BF16)
                w1 = pltpu.bitcast(jnp.broadcast_to(e1_ref[h, pl.ds(i1, 1), cs], half), BF16)
                g = g + jnp.where(rank2_ref[h, :, cs] < lim, e2_ref[h, :, cs], jnp.zeros_like(g)) * w1
            w_ref[j * PEER_NKEYS:(j + 1) * PEER_NKEYS, cs] = (g.astype(F32) * _gelu(act[:, cs])).astype(BF16)

    acts = [jnp.dot(u_ref[j * PEER_NKEYS:(j + 1) * PEER_NKEYS, :], zt_ref[...], preferred_element_type=F32)
            for j in range(n_i1)]
    per_group = PEER_DOWN_GROUP // PEER_NKEYS
    for g0 in range(0, n_i1, per_group):
        for j in range(g0, g0 + per_group):
            gated(j, acts[j])
        ks = slice(g0 * PEER_NKEYS, (g0 + per_group) * PEER_NKEYS)
        o_ref[...] += jnp.dot(vt_ref[0, :, ks], w_ref[ks, :], preferred_element_type=F32)


def _peer_experts(zt, u, vt, tabs, tb, ek):
    d, t = zt.shape
    ne = u.shape[0]
    once = pl.Buffered(1)
    tab_spec = pl.BlockSpec((PEER_HEADS, PEER_NKEYS, tb), lambda i, e: (0, 0, i), pipeline_mode=once)
    return pl.pallas_call(
        functools.partial(_peer_expert_kernel, ek=ek),
        out_shape=jax.ShapeDtypeStruct((d, t), F32),
        grid=(t // tb, ne // ek),
        in_specs=[pl.BlockSpec((d, tb), lambda i, e: (0, i), pipeline_mode=once),
                  pl.BlockSpec((ek, d), lambda i, e: (e, 0)),
                  pl.BlockSpec((1, d, ek), lambda i, e: (e, 0, 0)),
                  tab_spec, tab_spec, tab_spec, tab_spec],
        out_specs=pl.BlockSpec((d, tb), lambda i, e: (0, i)),
        scratch_shapes=[pltpu.VMEM((ek, tb), BF16)],
        compiler_params=_cparams(("arbitrary", "arbitrary"), VMEM_LIMIT),
        name="peer_experts",
    )(zt, u, vt, *tabs)


def kernel(x, c, ctx, c_ctx, norm1_g, norm2_g, ada_w, ada_b, w_in, w_out, lru_conv_w, lru_conv_b, lru_wa, lru_ba,
           lru_wx, lru_bx, lru_lam, na_qn_g, na_kn_g, na_rpb, ret_gn_g, hgrn_lb_logits, hgrn_on_g, peer_wq,
           peer_subkeys, peer_u, peer_v):
    bsz, n_lat, d = x.shape
    depth = w_in.shape[0]
    l = CTX_LEN + n_lat
    gw = d // N_GROUPS
    assert bsz < MOD_ROWS and n_lat % ROW_TILE == 0 and CTX_LEN % ROW_TILE == 0
    ctx_row = bsz

    c16 = jnp.zeros((MOD_ROWS, d), F32).at[:bsz].set(c).at[ctx_row].set(c_ctx)
    cos, sin = _rope_tables(l)
    rows = n_lat // GRID_W

    h3 = jnp.concatenate([ctx, x], axis=1)
    tiles_full = l // ROW_TILE
    tiles_lat = n_lat // ROW_TILE
    ctx_tiles = CTX_LEN // ROW_TILE

    def row_full(i):
        return jnp.where(i % tiles_full < ctx_tiles, ctx_row, i // tiles_full)

    def row_lat(i):
        return i // tiles_lat

    h = h3.reshape(bsz * l, d)
    pending = None
    for layer in range(depth):
        last = layer == depth - 1
        mod3 = _ada_modulation(c16, ada_w, ada_b, layer).reshape(MOD_ROWS * N_MOD, 1, d)
        if pending is None:
            (u,) = _resid_norm(h, mod3, row_full, norm_g=norm1_g[layer], shift_k=0, scale_k=1)
        else:
            ft, mod3_prev = pending
            h, u = _resid_norm(h, mod3, row_full, ft=ft, gate_mod3=mod3_prev, gate_k=5,
                               norm_g=norm1_g[layer], shift_k=0, scale_k=1)
        p = _matmul(u, w_in, layer, 1024, 1024)
        p3 = p.reshape(N_IN_SPLITS * N_HEADS * bsz, l, HEAD_DIM)
        y_a = _rglru(p3, lru_conv_w[layer], lru_conv_b[layer], lru_wa[layer], lru_ba[layer],
                     lru_wx[layer], lru_bx[layer], lru_lam[layer])
        y_b = _natten(p3, na_qn_g[layer], na_kn_g[layer], _na_bias_table(na_rpb[layer], rows))
        y_c = _retention(p3, ret_gn_g[layer], cos, sin)
        y_d = _hgrn2(p3, hgrn_lb_logits, hgrn_on_g[layer], layer)
        if last:
            hm3 = _out_proj((y_a, y_b, y_c, y_d), w_out, layer, h.reshape(bsz, l, d), mod3, row_off=ctx_tiles,
                            row_fn=lambda bb, i: bb, gate_k=2)
            row_fn = row_lat
        else:
            hm3 = _out_proj((y_a, y_b, y_c, y_d), w_out, layer, h.reshape(bsz, l, d), mod3, row_off=0,
                            row_fn=lambda bb, i: jnp.where(i < ctx_tiles, ctx_row, bb), gate_k=2)
            row_fn = row_full
        h = hm3.reshape(-1, d)
        (zt,) = _resid_norm(h, mod3, row_fn, norm_g=norm2_g[layer], shift_k=3, scale_k=4, transpose_out=True)
        wq_t = _cast_weight(peer_wq, layer, transpose=True, rows_per_step=512)
        u_b = _cast_weight(peer_u, layer, transpose=False, rows_per_step=512)
        v_t = _cast_weight(peer_v, layer, transpose=True, rows_per_step=512)
        tabs = _peer_route(zt, wq_t, peer_subkeys[layer], PEER_TOKEN_BLOCK)
        ft = _peer_experts(zt, u_b, v_t, tabs, PEER_TOKEN_BLOCK, PEER_EXPERT_BLOCK)
        if last:
            (h,) = _resid_norm(h, mod3, row_fn, ft=ft, gate_k=5)
        else:
            pending = (ft, mod3)
    return h.reshape(bsz, n_lat, d)
```

```python
import functools

import jax
import jax.numpy as jnp
from jax import lax
from jax.experimental import pallas as pl
from jax.experimental.pallas import tpu as pltpu

F32 = jnp.float32
BF16 = jnp.bfloat16

N_GROUPS = 4
N_HEADS = 8
HEAD_DIM = 128
N_IN_SPLITS = 14
N_MOD = 6
CTX_LEN = 256
GRID_W = 64
LRU_CONV_W = 4
LRU_C = 8.0
NA_WIN_ROWS = 8
NA_WIN_COLS = 16
RET_CHUNK = 128
HGRN_CHUNK = 32
PEER_HEADS = 8
PEER_NKEYS = 128
PEER_TOPK = 16
ROPE_BASE = 10000.0
EPS = 1e-6
NEG_INF = -1e30

SUBLANES = 8
LANES = 128
ROW_TILE = 256
ATTN_BLOCK = 128
MOD_ROWS = 16
PEER_TOKEN_BLOCK = 512
PEER_EXPERT_BLOCK = 512
PEER_LANE_CHUNK = 256
PEER_DOWN_GROUP = 256
VMEM_LIMIT = 56 * 1024 * 1024


def _cparams(sem, vmem=None):
    return pltpu.CompilerParams(dimension_semantics=sem, vmem_limit_bytes=vmem)


def _bdot(a, b):
    return jnp.dot(a.astype(BF16), b.astype(BF16), preferred_element_type=F32)


def _bdot_nt(a, b):
    return lax.dot_general(a.astype(BF16), b.astype(BF16), (((1,), (1,)), ((), ())),
                           preferred_element_type=F32)


def _bdot_tn(a, b):
    return lax.dot_general(a.astype(BF16), b.astype(BF16), (((0,), (0,)), ((), ())),
                           preferred_element_type=F32)


def _gelu(x):
    c = 0.7978845608028654
    return x / (1.0 + jnp.exp(x * (-2.0 * c - (2.0 * c * 0.044715) * (x * x))))


def _sigmoid(x):
    return 1.0 / (1.0 + jnp.exp(-x))


def _silu(x):
    return x * _sigmoid(x)


def _ada_kernel(c_ref, w_ref, b_ref, o_ref):
    x = c_ref[...]
    x = _silu(x)
    hi = x.astype(BF16)
    lo = (x - hi.astype(F32)).astype(BF16)
    w = w_ref[0]
    whi = w.astype(BF16)
    wlo = (w - whi.astype(F32)).astype(BF16)
    acc = jnp.dot(hi, whi, preferred_element_type=F32)
    acc += jnp.dot(hi, wlo, preferred_element_type=F32)
    acc += jnp.dot(lo, whi, preferred_element_type=F32)
    o_ref[...] = acc + b_ref[0]


def _ada_modulation(c16, w_all, b_all, layer):
    depth, d, n = w_all.shape
    tn = 512
    return pl.pallas_call(
        _ada_kernel,
        out_shape=jax.ShapeDtypeStruct((MOD_ROWS, n), F32),
        grid=(n // tn,),
        in_specs=[pl.BlockSpec((MOD_ROWS, d), lambda j: (0, 0)),
                  pl.BlockSpec((1, d, tn), lambda j: (layer, 0, j)),
                  pl.BlockSpec((1, 1, tn), lambda j: (layer, 0, j))],
        out_specs=pl.BlockSpec((MOD_ROWS, tn), lambda j: (0, j)),
        compiler_params=_cparams(("arbitrary",), VMEM_LIMIT),
        name="ada_modulation",
    )(c16, w_all, b_all.reshape(depth, 1, n))


def _resid_norm_kernel(*refs, has_f, has_norm, transpose_out):
    it = iter(refs)
    h_ref = next(it)
    if has_f:
        ft_ref = next(it)
        gate_ref = next(it)
    if has_norm:
        g_ref = next(it)
        sh_ref = next(it)
        sc_ref = next(it)
    if has_f:
        hn_ref = next(it)
    if has_norm:
        u_ref = next(it)
    x = h_ref[...]
    if has_f:
        x = x + gate_ref[0] * ft_ref[0].T
        hn_ref[...] = x
    if has_norm:
        ms = jnp.mean(x * x, axis=-1, keepdims=True)
        y = x * lax.rsqrt(ms + EPS) * g_ref[...]
        y = y * (1.0 + sc_ref[0]) + sh_ref[0]
        if transpose_out:
            u_ref[0] = y.T.astype(BF16)
        else:
            u_ref[...] = y.astype(BF16)


def _resid_norm(h, mod3, row_fn, *, ft=None, gate_mod3=None, gate_k=None, norm_g=None, shift_k=None,
                scale_k=None, transpose_out=False):
    t, d = h.shape
    nt = t // ROW_TILE
    tiles_per_chunk = PEER_TOKEN_BLOCK // ROW_TILE
    chunk_spec = pl.BlockSpec((1, d, ROW_TILE), lambda i: (i // tiles_per_chunk, 0, i % tiles_per_chunk))
    has_f = ft is not None
    has_norm = norm_g is not None
    in_specs = [pl.BlockSpec((ROW_TILE, d), lambda i: (i, 0))]
    args = [h]
    if has_f:
        in_specs += [chunk_spec,
                     pl.BlockSpec((1, 1, d), lambda i: (row_fn(i) * N_MOD + gate_k, 0, 0))]
        args += [ft, mod3 if gate_mod3 is None else gate_mod3]
    if has_norm:
        in_specs += [pl.BlockSpec((1, d), lambda i: (0, 0)),
                     pl.BlockSpec((1, 1, d), lambda i: (row_fn(i) * N_MOD + shift_k, 0, 0)),
                     pl.BlockSpec((1, 1, d), lambda i: (row_fn(i) * N_MOD + scale_k, 0, 0))]
        args += [norm_g.reshape(1, d), mod3, mod3]
    out_shape, out_specs = [], []
    if has_f:
        out_shape.append(jax.ShapeDtypeStruct((t, d), F32))
        out_specs.append(pl.BlockSpec((ROW_TILE, d), lambda i: (i, 0)))
    if has_norm:
        if transpose_out:
            out_shape.append(jax.ShapeDtypeStruct((t // PEER_TOKEN_BLOCK, d, PEER_TOKEN_BLOCK), BF16))
            out_specs.append(chunk_spec)
        else:
            out_shape.append(jax.ShapeDtypeStruct((t, d), BF16))
            out_specs.append(pl.BlockSpec((ROW_TILE, d), lambda i: (i, 0)))
    outs = pl.pallas_call(
        functools.partial(_resid_norm_kernel, has_f=has_f, has_norm=has_norm, transpose_out=transpose_out),
        out_shape=out_shape, grid=(nt,), in_specs=in_specs, out_specs=out_specs,
        compiler_params=_cparams(("arbitrary",), VMEM_LIMIT),
        name="resid_norm",
    )(*args)
    return outs


def _mm_kernel(x_ref, w_ref, o_ref, wb_ref):
    @pl.when(pl.program_id(1) == 0)
    def _():
        wb_ref[...] = w_ref[0].astype(BF16)

    r = jnp.dot(x_ref[...], wb_ref[...], preferred_element_type=F32)
    for hh in range(o_ref.shape[0]):
        o_ref[hh] = r[:, hh * HEAD_DIM:(hh + 1) * HEAD_DIM]


def _matmul(x, w_all, layer, tm, tn):
    m, k = x.shape
    n = w_all.shape[2]
    return pl.pallas_call(
        _mm_kernel,
        out_shape=jax.ShapeDtypeStruct((n // HEAD_DIM, m, HEAD_DIM), F32),
        grid=(n // tn, m // tm),
        in_specs=[pl.BlockSpec((tm, k), lambda j, i: (i, 0)),
                  pl.BlockSpec((1, k, tn), lambda j, i: (layer, 0, j), pipeline_mode=pl.Buffered(1))],
        out_specs=pl.BlockSpec((tn // HEAD_DIM, tm, HEAD_DIM), lambda j, i: (j, i, 0)),
        scratch_shapes=[pltpu.VMEM((k, tn), BF16)],
        compiler_params=_cparams(("arbitrary", "arbitrary"), VMEM_LIMIT),
        name="in_proj",
    )(x, w_all)


def _cast_kernel(w_ref, o_ref, *, transpose):
    w = w_ref[0]
    if transpose:
        o_ref[0] = w.T.astype(BF16)
    else:
        o_ref[...] = w.astype(BF16)


def _cast_weight(w_all, layer, *, transpose, rows_per_step):
    _, r, c = w_all.shape
    tr = rows_per_step
    if transpose:
        out_shape, out_spec = (r // tr, c, tr), pl.BlockSpec((1, c, tr), lambda i: (i, 0, 0))
    else:
        out_shape, out_spec = (r, c), pl.BlockSpec((tr, c), lambda i: (i, 0))
    return pl.pallas_call(
        functools.partial(_cast_kernel, transpose=transpose),
        out_shape=jax.ShapeDtypeStruct(out_shape, BF16),
        grid=(r // tr,),
        in_specs=[pl.BlockSpec((1, tr, c), lambda i: (layer, i, 0))],
        out_specs=out_spec,
        compiler_params=_cparams(("arbitrary",), VMEM_LIMIT),
        name="cast_weight",
    )(w_all)


def _head_slab_spec(bsz, l, split):
    return pl.BlockSpec((1, l, HEAD_DIM), lambda bb, h: ((split * N_HEADS + h) * bsz + bb, 0, 0))


def _wout_kernel(ya_ref, yb_ref, yc_ref, yd_ref, w_ref, h_ref, gate_ref, o_ref, wb_ref):
    @pl.when((pl.program_id(1) == 0) & (pl.program_id(2) == 0))
    def _():
        wb_ref[...] = w_ref[0].astype(BF16)

    gw = ya_ref.shape[2]
    acc = jnp.dot(ya_ref[0], wb_ref[0 * gw:1 * gw, :], preferred_element_type=F32)
    acc += jnp.dot(yb_ref[0], wb_ref[1 * gw:2 * gw, :], preferred_element_type=F32)
    acc += jnp.dot(yc_ref[0], wb_ref[2 * gw:3 * gw, :], preferred_element_type=F32)
    acc += jnp.dot(yd_ref[0], wb_ref[3 * gw:4 * gw, :], preferred_element_type=F32)
    o_ref[0] = h_ref[0] + gate_ref[0] * acc


def _out_proj(ys, w_all, layer, h3, mod3, *, row_off, row_fn, gate_k):
    b, l, gw = ys[0].shape
    d = w_all.shape[2]
    nrt = l // ROW_TILE - row_off
    tn = 1024
    y_spec = pl.BlockSpec((1, ROW_TILE, gw), lambda j, bb, i: (bb, i + row_off, 0))
    return pl.pallas_call(
        _wout_kernel,
        out_shape=jax.ShapeDtypeStruct((b, nrt * ROW_TILE, d), F32),
        grid=(d // tn, b, nrt),
        in_specs=[y_spec, y_spec, y_spec, y_spec,
                  pl.BlockSpec((1, N_GROUPS * gw, tn), lambda j, bb, i: (layer, 0, j),
                               pipeline_mode=pl.Buffered(1)),
                  pl.BlockSpec((1, ROW_TILE, tn), lambda j, bb, i: (bb, i + row_off, j)),
                  pl.BlockSpec((1, 1, tn), lambda j, bb, i: (row_fn(bb, i) * N_MOD + gate_k, 0, j))],
        out_specs=pl.BlockSpec((1, ROW_TILE, tn), lambda j, bb, i: (bb, i, j)),
        scratch_shapes=[pltpu.VMEM((N_GROUPS * gw, tn), BF16)],
        compiler_params=_cparams(("arbitrary", "arbitrary", "arbitrary"), VMEM_LIMIT),
        name="out_proj",
    )(*ys, w_all, h3, mod3)


def _group_cumsum(x, rowmod, c, reverse):
    n = x.shape[0]
    k = 1
    while k < c:
        if reverse:
            x = x + jnp.where(rowmod < c - k, pltpu.roll(x, n - k, 0), 0.0)
        else:
            x = x + jnp.where(rowmod >= k, pltpu.roll(x, k, 0), 0.0)
        k *= 2
    return x


def _group_linear_scan(a, u, rowmod, c, reverse):
    n = a.shape[0]
    k = 1
    while k < c:
        if reverse:
            ok = rowmod < c - k
            a_s = pltpu.roll(a, n - k, 0)
            u_s = pltpu.roll(u, n - k, 0)
        else:
            ok = rowmod >= k
            a_s = pltpu.roll(a, k, 0)
            u_s = pltpu.roll(u, k, 0)
        u = jnp.where(ok, a * u_s + u, u)
        a = jnp.where(ok, a * a_s, a)
        k *= 2
    return a, u


def _rglru_kernel(x_ref, g_ref, cw_ref, cb_ref, wa_ref, ba_ref, wx_ref, bx_ref, lam_ref, y_ref,
                  xp_ref, af_ref, uf_ref, ab_ref, ub_ref, *, seq_len):
    l = seq_len
    pad = SUBLANES
    nblk = l // ROW_TILE
    zeros_pad = jnp.zeros((pad, LANES), F32)
    xp_ref[0:pad, :] = zeros_pad
    xp_ref[pad + l:pad + l + pad, :] = zeros_pad
    xp_ref[pad:pad + l, :] = x_ref[0]

    row = lax.broadcasted_iota(jnp.int32, (ROW_TILE, LANES), 0)
    rowmod = row % SUBLANES
    cw = cw_ref[...]
    cb = cb_ref[...]
    lam = lam_ref[...]
    sp = [jnp.maximum(-lam[dd], 0.0) + jnp.log1p(jnp.exp(-jnp.abs(lam[dd]))) for dd in range(2)]

    for blk in range(nblk):
        s0 = blk * ROW_TILE
        taps = []
        for tap in range(LRU_CONV_W):
            off = tap - LRU_CONV_W // 2
            v = xp_ref[pad + s0 + off:pad + s0 + off + ROW_TILE, :]
            if blk == 0 and off > 0:
                v = jnp.where(row + off < CTX_LEN, v, 0.0)
            if s0 == CTX_LEN and off < 0:
                v = jnp.where(row + off >= 0, v, 0.0)
            taps.append(v)
        xc = cb + taps[0] * cw[0:1]
        for tap in range(1, LRU_CONV_W):
            xc = xc + taps[tap] * cw[tap:tap + 1]
        xcb = xc.astype(BF16)
        for dd, (a_ref, u_ref) in enumerate(((af_ref, uf_ref), (ab_ref, ub_ref))):
            r = _sigmoid(jnp.dot(xcb, wa_ref[dd, 0].astype(BF16), preferred_element_type=F32) + ba_ref[dd])
            i = _sigmoid(jnp.dot(xcb, wx_ref[dd, 0].astype(BF16), preferred_element_type=F32) + bx_ref[dd])
            log_a = -LRU_C * r * sp[dd]
            a = jnp.exp(log_a)
            u = jnp.sqrt(1.0 - jnp.exp(2.0 * log_a)) * (i * xc)
            a_loc, u_loc = _group_linear_scan(a, u, rowmod, SUBLANES, reverse=(dd == 1))
            a_ref[s0:s0 + ROW_TILE, :] = a_loc
            u_ref[s0:s0 + ROW_TILE, :] = u_loc

    n_chunks = l // SUBLANES
    n_ctx_chunks = CTX_LEN // SUBLANES

    def fwd_body(c, carry):
        r0 = pl.multiple_of(c * SUBLANES, SUBLANES)
        hh = af_ref[pl.ds(r0, SUBLANES), :] * carry + uf_ref[pl.ds(r0, SUBLANES), :]
        uf_ref[pl.ds(r0, SUBLANES), :] = hh
        return jnp.broadcast_to(hh[SUBLANES - 1:SUBLANES, :], (SUBLANES, LANES))

    lax.fori_loop(0, n_chunks, fwd_body, jnp.zeros((SUBLANES, LANES), F32))

    def bwd_body(j, carry):
        c = jnp.where(j < n_ctx_chunks, n_ctx_chunks - 1 - j, n_chunks - 1 - (j - n_ctx_chunks))
        r0 = pl.multiple_of(c * SUBLANES, SUBLANES)
        hh = ab_ref[pl.ds(r0, SUBLANES), :] * carry + ub_ref[pl.ds(r0, SUBLANES), :]
        ub_ref[pl.ds(r0, SUBLANES), :] = hh
        return jnp.broadcast_to(hh[0:1, :], (SUBLANES, LANES))

    lax.fori_loop(0, n_chunks, bwd_body, jnp.zeros((SUBLANES, LANES), F32))

    for blk in range(nblk):
        s0 = blk * ROW_TILE
        hsum = uf_ref[s0:s0 + ROW_TILE, :] + ub_ref[s0:s0 + ROW_TILE, :]
        y_ref[0, s0:s0 + ROW_TILE, :] = (_gelu(g_ref[0, s0:s0 + ROW_TILE, :]) * hsum).astype(BF16)


def _rglru(p3, conv_w, conv_b, wa, ba, wx, bx, lam):
    l = p3.shape[1]
    b = p3.shape[0] // (N_IN_SPLITS * N_HEADS)
    gw = N_HEADS * HEAD_DIM
    vec = lambda a: a.reshape(2, 1, gw)
    spec_vec = pl.BlockSpec((2, 1, HEAD_DIM), lambda bb, h: (0, 0, h))
    spec_w = pl.BlockSpec((2, 1, HEAD_DIM, HEAD_DIM), lambda bb, h: (0, h, 0, 0))
    return pl.pallas_call(
        functools.partial(_rglru_kernel, seq_len=l),
        out_shape=jax.ShapeDtypeStruct((b, l, gw), BF16),
        grid=(b, N_HEADS),
        in_specs=[_head_slab_spec(b, l, 0), _head_slab_spec(b, l, 1),
                  pl.BlockSpec((LRU_CONV_W, HEAD_DIM), lambda bb, h: (0, h)),
                  pl.BlockSpec((1, HEAD_DIM), lambda bb, h: (0, h)),
                  spec_w, spec_vec, spec_w, spec_vec, spec_vec],
        out_specs=pl.BlockSpec((1, l, HEAD_DIM), lambda bb, h: (bb, 0, h)),
        scratch_shapes=[pltpu.VMEM((l + 2 * SUBLANES, LANES), F32)] + [pltpu.VMEM((l, LANES), F32)] * 4,
        compiler_params=_cparams(("arbitrary", "arbitrary"), VMEM_LIMIT),
        name="rglru",
    )(p3, p3, conv_w, conv_b.reshape(1, gw), wa, vec(ba), wx, vec(bx), vec(lam))


def _na_kernel(*refs, seq_len):
    nh = HEADS_PER_STEP
    q_refs, k_refs, v_refs = (refs[i * nh:(i + 1) * nh] for i in range(3))
    qg_ref, kg_ref, bias_ref, y_ref = refs[3 * nh:3 * nh + 4]
    scratch = refs[3 * nh + 4:]
    qb_refs, kb_refs, vb_refs = (scratch[i * nh:(i + 1) * nh] for i in range(3))
    l = seq_len
    scale = HEAD_DIM ** -0.5
    rows = (l - CTX_LEN) // GRID_W
    kr = min(NA_WIN_ROWS, rows)
    n_loc = kr * GRID_W

    def headnorm(z, g):
        return z * lax.rsqrt(jnp.mean(z * z, axis=-1, keepdims=True) + EPS) * g

    for hh in range(nh):
        cols = slice(hh * HEAD_DIM, (hh + 1) * HEAD_DIM)
        for s0 in range(0, l, ROW_TILE):
            rs = slice(s0, s0 + ROW_TILE)
            qb_refs[hh][rs, :] = headnorm(q_refs[hh][0, rs, :], qg_ref[...]).astype(BF16)
            kb_refs[hh][rs, :] = headnorm(k_refs[hh][0, rs, :], kg_ref[...]).astype(BF16)
            vb_refs[hh][rs, :] = v_refs[hh][0, rs, :].astype(BF16)
        k_ctx = kb_refs[hh][0:CTX_LEN, :]
        s_c = _bdot_nt(qb_refs[hh][0:CTX_LEN, :], k_ctx) * scale
        e_c = jnp.exp(s_c - jnp.max(s_c, axis=-1, keepdims=True))
        p_c = e_c / jnp.sum(e_c, axis=-1, keepdims=True)
        y_ref[0, 0:CTX_LEN, cols] = _bdot(p_c, vb_refs[hh][0:CTX_LEN, :]).astype(BF16)

    def scores(r, hh):
        r0 = jnp.clip(r - kr // 2, 0, rows - kr)
        qs = pl.multiple_of(CTX_LEN + r * GRID_W, GRID_W)
        ks = pl.multiple_of(CTX_LEN + r0 * GRID_W, GRID_W)
        q_r = qb_refs[hh][pl.ds(qs, GRID_W), :]
        bias = bias_ref[hh, r0 - r + NA_WIN_ROWS - 1]
        s_loc = _bdot_nt(q_r, kb_refs[hh][pl.ds(ks, n_loc), :]) * scale + bias
        s_ctx = _bdot_nt(q_r, kb_refs[hh][0:CTX_LEN, :]) * scale
        return hh, qs, ks, s_loc, s_ctx

    def softmax(hh, qs, ks, s_loc, s_ctx):
        m = jnp.maximum(jnp.max(s_loc, axis=-1, keepdims=True), jnp.max(s_ctx, axis=-1, keepdims=True))
        e_loc = jnp.exp(s_loc - m)
        e_ctx = jnp.exp(s_ctx - m)
        inv = 1.0 / (jnp.sum(e_loc, axis=-1, keepdims=True) + jnp.sum(e_ctx, axis=-1, keepdims=True))
        return hh, qs, ks, (e_loc * inv).astype(BF16), (e_ctx * inv).astype(BF16)

    def readout(hh, qs, ks, p_loc, p_ctx):
        o = _bdot(p_loc, vb_refs[hh][pl.ds(ks, n_loc), :]) + _bdot(p_ctx, vb_refs[hh][0:CTX_LEN, :])
        y_ref[0, pl.ds(qs, GRID_W), hh * HEAD_DIM:(hh + 1) * HEAD_DIM] = o.astype(BF16)

    group = 4 if rows % 4 == 0 else 1

    def rows_body(gi, carry):
        chains = [scores(gi * group + t, hh) for t in range(group) for hh in range(nh)]
        chains = [softmax(*ch) for ch in chains]
        for ch in chains:
            readout(*ch)
        return carry

    lax.fori_loop(0, rows // group, rows_body, 0)


def _na_bias_table(rpb, rows):
    kr = min(NA_WIN_ROWS, rows)
    qc = jnp.arange(GRID_W)[:, None]
    kc = jnp.arange(GRID_W)[None, :]
    win0 = jnp.clip(qc - NA_WIN_COLS // 2, 0, GRID_W - NA_WIN_COLS)
    col_ok = (kc >= win0) & (kc < win0 + NA_WIN_COLS)
    rel_c = jnp.clip(kc - qc + NA_WIN_COLS - 1, 0, 2 * NA_WIN_COLS - 2)
    t = jnp.where(col_ok[None, None], rpb.astype(F32)[:, :, rel_c], NEG_INF)
    d = jnp.arange(NA_WIN_ROWS)[:, None] + jnp.arange(kr)[None, :]
    d = jnp.clip(d, 0, 2 * NA_WIN_ROWS - 2)
    tb = t[:, d]
    return tb.transpose(0, 1, 3, 2, 4).reshape(rpb.shape[0], NA_WIN_ROWS, GRID_W, kr * GRID_W)


def _natten(p3, qn_g, kn_g, bias_tab):
    l = p3.shape[1]
    b = p3.shape[0] // (N_IN_SPLITS * N_HEADS)
    gw = N_HEADS * HEAD_DIM
    n_loc = bias_tab.shape[-1]
    nh = HEADS_PER_STEP
    slabs = [s for split in (2, 3, 4) for s in _pair_slab_specs(b, l, split)]
    return pl.pallas_call(
        functools.partial(_na_kernel, seq_len=l),
        out_shape=jax.ShapeDtypeStruct((b, l, gw), BF16),
        grid=(b, N_HEADS // nh),
        in_specs=slabs + [pl.BlockSpec((1, HEAD_DIM), lambda bb, hp: (0, 0)),
                          pl.BlockSpec((1, HEAD_DIM), lambda bb, hp: (0, 0)),
                          pl.BlockSpec((nh, NA_WIN_ROWS, GRID_W, n_loc), lambda bb, hp: (hp, 0, 0, 0))],
        out_specs=pl.BlockSpec((1, l, nh * HEAD_DIM), lambda bb, hp: (bb, 0, hp)),
        scratch_shapes=[pltpu.VMEM((l, LANES), BF16)] * (3 * nh),
        compiler_params=_cparams(("arbitrary", "arbitrary"), VMEM_LIMIT),
        name="natten",
    )(*([p3] * (3 * nh)), qn_g.reshape(1, HEAD_DIM), kn_g.reshape(1, HEAD_DIM), bias_tab)


def _dir_cums(lf, rowmod, c, reverse):
    if lf.shape[0] == 1:
        n_inc = (c - rowmod) if reverse else (rowmod + 1)
        n_oth = rowmod if reverse else (c - 1 - rowmod)
        inc = n_inc.astype(F32) * lf
        oth = n_oth.astype(F32) * lf
    else:
        n = lf.shape[0]
        row = lax.broadcasted_iota(jnp.int32, (n, n), 0)
        col = lax.broadcasted_iota(jnp.int32, (n, n), 1)
        same = (row // c) == (col // c)
        own = same & ((col >= row) if reverse else (col <= row))
        other = same & ((col < row) if reverse else (col > row))
        tri = jnp.concatenate([own, other], axis=0).astype(F32).astype(BF16)
        hi = lf.astype(BF16)
        r1 = lf - hi.astype(F32)
        mid = r1.astype(BF16)
        lo = (r1 - mid.astype(F32)).astype(BF16)
        acc = jnp.dot(tri, hi, preferred_element_type=F32)
        acc += jnp.dot(tri, mid, preferred_element_type=F32)
        acc += jnp.dot(tri, lo, preferred_element_type=F32)
        inc, oth = acc[:n], acc[n:]
    return inc, oth


def _decay_attn_blocks(s0, heads, c, ref_sets):
    n = ATTN_BLOCK
    row = lax.broadcasted_iota(jnp.int32, (n, n), 0)
    col = lax.broadcasted_iota(jnp.int32, (n, n), 1)
    rowmod = row % c
    same = (row // c) == (col // c)
    cums = [[_dir_cums(lf, rowmod, c, reverse) for reverse, lf in ((False, lf_f), (True, lf_b))]
            for (_, _, _, _, lf_f, lf_b) in heads]
    pending = []
    for hd, ((q, k_f, k_b, v, _, _), refs) in enumerate(zip(heads, ref_sets)):
        (qdf_ref, ksf_ref, ktf_ref, decf_ref, qdb_ref, ksb_ref, ktb_ref, decb_ref, vb_ref, o_ref) = refs
        vb = v.astype(BF16)
        vb_ref[pl.ds(s0, n), :] = vb
        scores = []
        for reverse, kk, (qd_ref, ks_ref, kt_ref, dec_ref) in (
                (False, k_f, (qdf_ref, ksf_ref, ktf_ref, decf_ref)),
                (True, k_b, (qdb_ref, ksb_ref, ktb_ref, decb_ref))):
            inc, oth = cums[hd][int(reverse)]
            kt = kk * jnp.exp(oth)
            dec = jnp.exp(inc + oth)
            qd = (q * jnp.exp(inc)).astype(BF16)
            ks = (kk * jnp.exp(-inc)).astype(BF16)
            qd_ref[pl.ds(s0, n), :] = qd
            ks_ref[pl.ds(s0, n), :] = ks
            kt_ref[pl.ds(s0, n), :] = kt.astype(BF16)
            dec_ref[pl.ds(s0, n), :] = dec
            scores.append((reverse, _bdot_nt(qd, ks)))
        pending.append((scores, vb, o_ref))
    for scores, vb, o_ref in pending:
        o = jnp.zeros((n, HEAD_DIM), F32)
        for reverse, s in scores:
            keep = same & ((col >= row) if reverse else (col <= row))
            o = o + jnp.dot(jnp.where(keep, s, 0.0).astype(BF16), vb, preferred_element_type=F32)
        o_ref[pl.ds(s0, n), :] = o


def _decay_attn_state_pass(c, seq_len, ref_sets):
    n_chunks = seq_len // c
    n_ctx_chunks = CTX_LEN // c

    steps = 4 if n_chunks % 4 == 0 else 3 if n_chunks % 3 == 0 else 1

    def body(jj, carry):
        chains = []
        for hh, refs in enumerate(ref_sets):
            (qdf_ref, _, ktf_ref, decf_ref, qdb_ref, _, ktb_ref, decb_ref, vb_ref, o_ref) = refs
            chains.append((qdf_ref, ktf_ref, decf_ref, vb_ref, o_ref, False))
            chains.append((qdb_ref, ktb_ref, decb_ref, vb_ref, o_ref, True))
        starts = []
        for t in range(steps):
            j = jj * steps + t
            jb = jnp.where(j < n_ctx_chunks, n_ctx_chunks - 1 - j, n_chunks - 1 - (j - n_ctx_chunks))
            starts.append((pl.multiple_of(j * c, c), pl.multiple_of(jb * c, c)))
        upds = [[_bdot_tn(vb_ref[pl.ds(starts[t][rev], c), :], kt_ref[pl.ds(starts[t][rev], c), :])
                 for t in range(steps)]
                for (_, kt_ref, _, vb_ref, _, rev) in chains]
        states = list(carry)
        for t in range(steps):
            for ci, (qd_ref, _, dec_ref, _, o_ref, rev) in enumerate(chains):
                r0 = starts[t][rev]
                o_ref[pl.ds(r0, c), :] += _bdot_nt(qd_ref[pl.ds(r0, c), :], states[ci])
                states[ci] = states[ci] * dec_ref[pl.ds(r0, 1), :] + upds[ci][t]
        return tuple(states)

    zero = jnp.zeros((HEAD_DIM, HEAD_DIM), F32)
    lax.fori_loop(0, n_chunks // steps, body, (zero,) * (2 * len(ref_sets)))


DECAY_SCRATCH_PER_HEAD = 10
HEADS_PER_STEP = 2


def _decay_scratch(l):
    one_dir = [pltpu.VMEM((l, LANES), BF16)] * 3 + [pltpu.VMEM((l, LANES), F32)]
    one_head = one_dir + one_dir + [pltpu.VMEM((l, LANES), BF16), pltpu.VMEM((l, LANES), F32)]
    assert len(one_head) == DECAY_SCRATCH_PER_HEAD
    return one_head * HEADS_PER_STEP


def _head_sets(scratch):
    n = DECAY_SCRATCH_PER_HEAD
    return [tuple(scratch[hh * n:(hh + 1) * n]) for hh in range(HEADS_PER_STEP)]


def _pair_slab_specs(bsz, l, split):
    return [pl.BlockSpec((1, l, HEAD_DIM),
                         lambda bb, hp, hh=hh: ((split * N_HEADS + hp * HEADS_PER_STEP + hh) * bsz + bb, 0, 0))
            for hh in range(HEADS_PER_STEP)]


def _retention_kernel(*refs, seq_len):
    nh = HEADS_PER_STEP
    q_refs, k_refs, v_refs, g_refs = (refs[i * nh:(i + 1) * nh] for i in range(4))
    cos_ref, sin_ref, lg_ref, gn_ref, y_ref = refs[4 * nh:4 * nh + 5]
    sets = _head_sets(refs[4 * nh + 5:])
    l = seq_len
    n = ATTN_BLOCK
    lane = lax.broadcasted_iota(jnp.int32, (n, HEAD_DIM), 1)
    first = (lane % (HEAD_DIM // 2)) < (HEAD_DIM // 4)

    def rope(z, cos, sin):
        swapped = jnp.where(first, pltpu.roll(z, HEAD_DIM - HEAD_DIM // 4, 1), pltpu.roll(z, HEAD_DIM // 4, 1))
        return z * cos + swapped * sin

    def block_body(bi, carry):
        s0 = pl.multiple_of(bi * n, n)
        cos = cos_ref[pl.ds(s0, n), :]
        sin = sin_ref[pl.ds(s0, n), :]
        heads = []
        for hh in range(nh):
            lg = lg_ref[hh]
            q = rope(q_refs[hh][0, pl.ds(s0, n), :], cos, sin)
            k = rope(k_refs[hh][0, pl.ds(s0, n), :], cos, sin) * (HEAD_DIM ** -0.5)
            heads.append((q, k, k, v_refs[hh][0, pl.ds(s0, n), :], lg[0:1], lg[1:2]))
        _decay_attn_blocks(s0, heads, RET_CHUNK, sets)
        return carry

    lax.fori_loop(0, l // n, block_body, 0)
    _decay_attn_state_pass(RET_CHUNK, l, sets)

    def out_body(bi, carry):
        s0 = pl.multiple_of(bi * n, n)
        for hh in range(nh):
            cols = slice(hh * HEAD_DIM, (hh + 1) * HEAD_DIM)
            o = sets[hh][-1][pl.ds(s0, n), :]
            o = o - jnp.mean(o, axis=-1, keepdims=True)
            o = o * lax.rsqrt(jnp.mean(o * o, axis=-1, keepdims=True) + EPS) * gn_ref[:, cols]
            y_ref[0, pl.ds(s0, n), cols] = (_silu(g_refs[hh][0, pl.ds(s0, n), :]) * o).astype(BF16)
        return carry

    lax.fori_loop(0, l // n, out_body, 0)


def _rope_tables(l):
    n_lat = l - CTX_LEN
    quarter = HEAD_DIM // 4
    t = jnp.arange(n_lat)
    inv_freq = ROPE_BASE ** (-jnp.arange(quarter, dtype=F32) / quarter)
    ang_r = (t // GRID_W).astype(F32)[:, None] * inv_freq[None, :]
    ang_c = (t % GRID_W).astype(F32)[:, None] * inv_freq[None, :]
    cos = jnp.concatenate([jnp.cos(ang_r), jnp.cos(ang_r), jnp.cos(ang_c), jnp.cos(ang_c)], axis=-1)
    sin = jnp.concatenate([-jnp.sin(ang_r), jnp.sin(ang_r), -jnp.sin(ang_c), jnp.sin(ang_c)], axis=-1)
    cos = jnp.concatenate([jnp.ones((CTX_LEN, HEAD_DIM), F32), cos], axis=0)
    sin = jnp.concatenate([jnp.zeros((CTX_LEN, HEAD_DIM), F32), sin], axis=0)
    return cos, sin


def _retention(p3, gn_g, cos, sin):
    l = p3.shape[1]
    b = p3.shape[0] // (N_IN_SPLITS * N_HEADS)
    gw = N_HEADS * HEAD_DIM
    log_gamma = jnp.log(1.0 - 2.0 ** (-5.0 - jnp.arange(N_HEADS, dtype=F32)))
    lg = jnp.stack([log_gamma, log_gamma[::-1]], axis=1)
    lg = jnp.broadcast_to(lg[:, :, None], (N_HEADS, 2, HEAD_DIM))
    nh = HEADS_PER_STEP
    full = pl.BlockSpec((l, HEAD_DIM), lambda bb, hp: (0, 0))
    slabs = [s for split in (5, 6, 7, 8) for s in _pair_slab_specs(b, l, split)]
    return pl.pallas_call(
        functools.partial(_retention_kernel, seq_len=l),
        out_shape=jax.ShapeDtypeStruct((b, l, gw), BF16),
        grid=(b, N_HEADS // nh),
        in_specs=slabs + [full, full,
                          pl.BlockSpec((nh, 2, HEAD_DIM), lambda bb, hp: (hp, 0, 0)),
                          pl.BlockSpec((1, nh * HEAD_DIM), lambda bb, hp: (0, hp))],
        out_specs=pl.BlockSpec((1, l, nh * HEAD_DIM), lambda bb, hp: (bb, 0, hp)),
        scratch_shapes=_decay_scratch(l),
        compiler_params=_cparams(("arbitrary", "arbitrary"), VMEM_LIMIT),
        name="retention",
    )(*([p3] * (4 * nh)), cos, sin, lg, gn_g.reshape(1, gw))


def _hgrn_kernel(*refs, seq_len, layer, depth):
    nh = HEADS_PER_STEP
    ff_refs, fb_refs, i_refs, q_refs, g_refs = (refs[i * nh:(i + 1) * nh] for i in range(5))
    lbl_ref, on_ref, y_ref = refs[5 * nh:5 * nh + 3]
    sets = _head_sets(refs[5 * nh + 3:])
    l = seq_len
    n = ATTN_BLOCK
    logits = lbl_ref[...]
    log_lb, log1m_lb = [], []
    for dd in range(2):
        xs = [logits[ll * 2 + dd] for ll in range(depth)]
        mx = functools.reduce(jnp.maximum, xs)
        es = [jnp.exp(x - mx) for x in xs]
        tot = functools.reduce(lambda a, b_: a + b_, es)
        lb = jnp.zeros_like(tot)
        for ll in range(1, layer + 1):
            lb = lb + es[ll] / tot
        log_lb.append(jnp.log(lb))
        log1m_lb.append(jnp.log1p(-lb))

    def gate(fz, dd, cols):
        log_sig = jnp.minimum(fz, 0.0) - jnp.log1p(jnp.exp(-jnp.abs(fz)))
        a = jnp.broadcast_to(log_lb[dd][:, cols], fz.shape)
        bv = log1m_lb[dd][:, cols] + log_sig
        log_f = jnp.maximum(a, bv) + jnp.log1p(jnp.exp(-jnp.abs(a - bv)))
        return log_f, 1.0 - jnp.exp(log_f)

    def block_body(bi, carry):
        s0 = pl.multiple_of(bi * n, n)
        heads = []
        for hh in range(nh):
            cols = slice(hh * HEAD_DIM, (hh + 1) * HEAD_DIM)
            lf_f, k_f = gate(ff_refs[hh][0, pl.ds(s0, n), :], 0, cols)
            lf_b, k_b = gate(fb_refs[hh][0, pl.ds(s0, n), :], 1, cols)
            heads.append((q_refs[hh][0, pl.ds(s0, n), :], k_f, k_b, i_refs[hh][0, pl.ds(s0, n), :], lf_f, lf_b))
        _decay_attn_blocks(s0, heads, HGRN_CHUNK, sets)
        return carry

    lax.fori_loop(0, l // n, block_body, 0)
    _decay_attn_state_pass(HGRN_CHUNK, l, sets)

    def out_body(bi, carry):
        s0 = pl.multiple_of(bi * n, n)
        for hh in range(nh):
            cols = slice(hh * HEAD_DIM, (hh + 1) * HEAD_DIM)
            o = sets[hh][-1][pl.ds(s0, n), :]
            o = o * lax.rsqrt(jnp.mean(o * o, axis=-1, keepdims=True) + EPS) * on_ref[:, cols]
            y_ref[0, pl.ds(s0, n), cols] = (_silu(g_refs[hh][0, pl.ds(s0, n), :]) * o).astype(BF16)
        return carry

    lax.fori_loop(0, l // n, out_body, 0)


def _hgrn2(p3, lb_logits, on_g, layer):
    l = p3.shape[1]
    b = p3.shape[0] // (N_IN_SPLITS * N_HEADS)
    gw = N_HEADS * HEAD_DIM
    depth = lb_logits.shape[0]
    nh = HEADS_PER_STEP
    slabs = [s for split in (9, 10, 11, 12, 13) for s in _pair_slab_specs(b, l, split)]
    return pl.pallas_call(
        functools.partial(_hgrn_kernel, seq_len=l, layer=layer, depth=depth),
        out_shape=jax.ShapeDtypeStruct((b, l, gw), BF16),
        grid=(b, N_HEADS // nh),
        in_specs=slabs + [pl.BlockSpec((depth * 2, 1, nh * HEAD_DIM), lambda bb, hp: (0, 0, hp)),
                          pl.BlockSpec((1, nh * HEAD_DIM), lambda bb, hp: (0, hp))],
        out_specs=pl.BlockSpec((1, l, nh * HEAD_DIM), lambda bb, hp: (bb, 0, hp)),
        scratch_shapes=_decay_scratch(l),
        compiler_params=_cparams(("arbitrary", "arbitrary"), VMEM_LIMIT),
        name="hgrn2",
    )(*([p3] * (5 * nh)), lb_logits.astype(F32).reshape(depth * 2, 1, gw), on_g.reshape(1, gw))


def _topk_rows(x, k):
    r_n = x.shape[0]
    idx = lax.broadcasted_iota(jnp.int32, x.shape, 0).astype(F32)
    rank = jnp.full(x.shape, float(k), F32)
    vals = []
    for r in range(k):
        m = jnp.max(x, axis=0, keepdims=True)
        first = jnp.min(jnp.where(x == m, idx, float(r_n)), axis=0, keepdims=True)
        sel = idx == first
        rank = jnp.where(sel, float(r), rank)
        x = jnp.where(sel, -jnp.inf, x)
        vals.append(m)
    return rank, jnp.concatenate(vals, axis=0)


def _bf16_pair_words(x):
    bits = pltpu.bitcast(x.astype(BF16).astype(F32), jnp.uint32)
    return bits | (bits >> 16)


def _peer_route_kernel(zt_ref, wq_ref, sk_ref, rank2_ref, lim1_ref, e1_ref, e2_ref, *, tb):
    k = PEER_TOPK
    n_kc, _, kc = wq_ref.shape
    q = jnp.dot(wq_ref[0], zt_ref[0, 0:kc, :], preferred_element_type=F32)
    for ci in range(1, n_kc):
        q += jnp.dot(wq_ref[ci], zt_ref[0, ci * kc:(ci + 1) * kc, :], preferred_element_type=F32)
    half = q.shape[0] // 2
    for c0 in range(0, tb, LANES):
        s1 = _bdot(sk_ref[0, 0], q[:half, c0:c0 + LANES])
        s2 = _bdot(sk_ref[0, 1], q[half:, c0:c0 + LANES])
        rank1, top1 = _topk_rows(s1, k)
        rank2, top2 = _topk_rows(s2, k)
        e1 = jnp.exp(s1 - top1[0:1])
        e2 = jnp.exp(s2 - top2[0:1])
        e1t = jnp.exp(top1 - top1[0:1])
        e2t = jnp.exp(top2 - top2[0:1])
        n_multi = sum(1 for r in range(k) if k // (r + 1) > 1)
        assert k - n_multi == SUBLANES
        pieces, epieces, widths = [], [], []
        for r in range(n_multi):
            n_c = k // (r + 1)
            n_pad = -(-n_c // SUBLANES) * SUBLANES
            valid = lax.broadcasted_iota(jnp.int32, (n_pad, LANES), 0) < n_c
            pieces.append(jnp.where(valid, top1[r:r + 1] + top2[0:n_pad], -jnp.inf))
            epieces.append(e1t[r:r + 1] * e2t[0:n_pad])
            widths.append(n_pad)
        pieces.append(top1[n_multi:k] + top2[0:1])
        epieces.append(e1t[n_multi:k] * e2t[0:1])
        cand = jnp.concatenate(pieces, axis=0)
        ecand = jnp.concatenate(epieces, axis=0)
        crank, _ = _topk_rows(cand, k)
        sel = crank < k
        z = jnp.sum(jnp.where(sel, ecand, 0.0), axis=0, keepdims=True)
        lim1 = jnp.zeros(rank1.shape, F32)
        self32 = sel.astype(F32)
        off = 0
        for r in range(n_multi):
            cnt = jnp.sum(self32[off:off + widths[r]], axis=0, keepdims=True)
            lim1 = jnp.where(rank1 == r, cnt, lim1)
            off += widths[r]
        for r in range(n_multi, k):
            lim1 = jnp.where(rank1 == r, self32[off + r - n_multi:off + r - n_multi + 1], lim1)
        rank2_ref[0, 0, :, c0:c0 + LANES] = rank2.astype(BF16)
        lim1_ref[0, 0, :, c0:c0 + LANES] = _bf16_pair_words(lim1)
        e1_ref[0, 0, :, c0:c0 + LANES] = _bf16_pair_words(e1)
        e2_ref[0, 0, :, c0:c0 + LANES] = (e2 / z).astype(BF16)


def _peer_route(zt, wq_t, sub_keys):
    nb, d, tb = zt.shape
    t = nb * tb
    n_kc, _, kc = wq_t.shape
    qd = wq_t.shape[1] // PEER_HEADS
    tabs = [jax.ShapeDtypeStruct((nb, PEER_HEADS, PEER_NKEYS, tb), dt)
            for dt in (BF16, jnp.uint32, jnp.uint32, BF16)]
    tab_spec = pl.BlockSpec((1, 1, PEER_NKEYS, tb), lambda i, h: (i, h, 0, 0))
    return pl.pallas_call(
        functools.partial(_peer_route_kernel, tb=tb),
        out_shape=tabs,
        grid=(t // tb, PEER_HEADS),
        in_specs=[pl.BlockSpec((1, d, tb), lambda i, h: (i, 0, 0)),
                  pl.BlockSpec((n_kc, qd, kc), lambda i, h: (0, h, 0)),
                  pl.BlockSpec((1, 2, PEER_NKEYS, qd // 2), lambda i, h: (h, 0, 0, 0))],
        out_specs=[tab_spec] * 4,
        compiler_params=_cparams(("arbitrary", "arbitrary"), VMEM_LIMIT),
        name="peer_route",
    )(zt, wq_t, sub_keys)


def _peer_expert_kernel(zt_ref, u_ref, vt_ref, rank2_ref, lim1_ref, e1_ref, e2_ref, o_ref, w_ref, *, ek):
    e = pl.program_id(1)
    tb = zt_ref.shape[2]
    n_i1 = ek // PEER_NKEYS

    @pl.when(e == 0)
    def _():
        o_ref[...] = jnp.zeros(o_ref.shape, F32)

    half = (PEER_NKEYS // 2, PEER_LANE_CHUNK)

    def gated(j, act):
        i1 = e * n_i1 + j
        for c0 in range(0, tb, PEER_LANE_CHUNK):
            cs = slice(c0, c0 + PEER_LANE_CHUNK)
            g = jnp.zeros((PEER_NKEYS, PEER_LANE_CHUNK), BF16)
            for h in range(PEER_HEADS):
                lim = pltpu.bitcast(jnp.broadcast_to(lim1_ref[0, h, pl.ds(i1, 1), cs], half), BF16)
                w1 = pltpu.bitcast(jnp.broadcast_to(e1_ref[0, h, pl.ds(i1, 1), cs], half), BF16)
                g = g + jnp.where(rank2_ref[0, h, :, cs] < lim, e2_ref[0, h, :, cs], jnp.zeros_like(g)) * w1
            w_ref[j * PEER_NKEYS:(j + 1) * PEER_NKEYS, cs] = (g.astype(F32) * _gelu(act[:, cs])).astype(BF16)

    acts = [jnp.dot(u_ref[j * PEER_NKEYS:(j + 1) * PEER_NKEYS, :], zt_ref[0], preferred_element_type=F32)
            for j in range(n_i1)]
    per_group = PEER_DOWN_GROUP // PEER_NKEYS
    for g0 in range(0, n_i1, per_group):
        for j in range(g0, g0 + per_group):
            gated(j, acts[j])
        ks = slice(g0 * PEER_NKEYS, (g0 + per_group) * PEER_NKEYS)
        o_ref[0] += jnp.dot(vt_ref[0, :, ks], w_ref[ks, :], preferred_element_type=F32)


def _peer_experts(zt, u, vt, tabs, ek):
    nb, d, tb = zt.shape
    ne = u.shape[0]
    once = pl.Buffered(1)
    tab_spec = pl.BlockSpec((1, PEER_HEADS, PEER_NKEYS, tb), lambda i, e: (i, 0, 0, 0), pipeline_mode=once)
    return pl.pallas_call(
        functools.partial(_peer_expert_kernel, ek=ek),
        out_shape=jax.ShapeDtypeStruct((nb, d, tb), F32),
        grid=(nb, ne // ek),
        in_specs=[pl.BlockSpec((1, d, tb), lambda i, e: (i, 0, 0), pipeline_mode=once),
                  pl.BlockSpec((ek, d), lambda i, e: (e, 0)),
                  pl.BlockSpec((1, d, ek), lambda i, e: (e, 0, 0)),
                  tab_spec, tab_spec, tab_spec, tab_spec],
        out_specs=pl.BlockSpec((1, d, tb), lambda i, e: (i, 0, 0)),
        scratch_shapes=[pltpu.VMEM((ek, tb), BF16)],
        compiler_params=_cparams(("arbitrary", "arbitrary"), VMEM_LIMIT),
        name="peer_experts",
    )(zt, u, vt, *tabs)


def kernel(x, c, ctx, c_ctx, norm1_g, norm2_g, ada_w, ada_b, w_in, w_out, lru_conv_w, lru_conv_b, lru_wa, lru_ba,
           lru_wx, lru_bx, lru_lam, na_qn_g, na_kn_g, na_rpb, ret_gn_g, hgrn_lb_logits, hgrn_on_g, peer_wq,
           peer_subkeys, peer_u, peer_v):
    bsz, n_lat, d = x.shape
    depth = w_in.shape[0]
    l = CTX_LEN + n_lat
    gw = d // N_GROUPS
    assert bsz < MOD_ROWS and n_lat % ROW_TILE == 0 and CTX_LEN % ROW_TILE == 0
    ctx_row = bsz

    c16 = jnp.zeros((MOD_ROWS, d), F32).at[:bsz].set(c).at[ctx_row].set(c_ctx)
    cos, sin = _rope_tables(l)
    rows = n_lat // GRID_W

    h3 = jnp.concatenate([ctx, x], axis=1)
    tiles_full = l // ROW_TILE
    tiles_lat = n_lat // ROW_TILE
    ctx_tiles = CTX_LEN // ROW_TILE

    def row_full(i):
        return jnp.where(i % tiles_full < ctx_tiles, ctx_row, i // tiles_full)

    def row_lat(i):
        return i // tiles_lat

    h = h3.reshape(bsz * l, d)
    pending = None
    for layer in range(depth):
        last = layer == depth - 1
        mod3 = _ada_modulation(c16, ada_w, ada_b, layer).reshape(MOD_ROWS * N_MOD, 1, d)
        if pending is None:
            (u,) = _resid_norm(h, mod3, row_full, norm_g=norm1_g[layer], shift_k=0, scale_k=1)
        else:
            ft, mod3_prev = pending
            h, u = _resid_norm(h, mod3, row_full, ft=ft, gate_mod3=mod3_prev, gate_k=5,
                               norm_g=norm1_g[layer], shift_k=0, scale_k=1)
        p = _matmul(u, w_in, layer, 1024, 1024)
        p3 = p.reshape(N_IN_SPLITS * N_HEADS * bsz, l, HEAD_DIM)
        y_a = _rglru(p3, lru_conv_w[layer], lru_conv_b[layer], lru_wa[layer], lru_ba[layer],
                     lru_wx[layer], lru_bx[layer], lru_lam[layer])
        y_b = _natten(p3, na_qn_g[layer], na_kn_g[layer], _na_bias_table(na_rpb[layer], rows))
        y_c = _retention(p3, ret_gn_g[layer], cos, sin)
        y_d = _hgrn2(p3, hgrn_lb_logits, hgrn_on_g[layer], layer)
        if last:
            hm3 = _out_proj((y_a, y_b, y_c, y_d), w_out, layer, h.reshape(bsz, l, d), mod3, row_off=ctx_tiles,
                            row_fn=lambda bb, i: bb, gate_k=2)
            row_fn = row_lat
        else:
            hm3 = _out_proj((y_a, y_b, y_c, y_d), w_out, layer, h.reshape(bsz, l, d), mod3, row_off=0,
                            row_fn=lambda bb, i: jnp.where(i < ctx_tiles, ctx_row, bb), gate_k=2)
            row_fn = row_full
        h = hm3.reshape(-1, d)
        (zt,) = _resid_norm(h, mod3, row_fn, norm_g=norm2_g[layer], shift_k=3, scale_k=4, transpose_out=True)
        wq_t = _cast_weight(peer_wq, layer, transpose=True, rows_per_step=512)
        u_b = _cast_weight(peer_u, layer, transpose=False, rows_per_step=512)
        v_t = _cast_weight(peer_v, layer, transpose=True, rows_per_step=512)
        tabs = _peer_route(zt, wq_t, peer_subkeys[layer])
        ft = _peer_experts(zt, u_b, v_t, tabs, PEER_EXPERT_BLOCK)
        if last:
            (h,) = _resid_norm(h, mod3, row_fn, ft=ft, gate_k=5)
        else:
            pending = (ft, mod3)
    return h.reshape(bsz, n_lat, d)
```

```python
import functools

import jax
import jax.numpy as jnp
from jax import lax
from jax.experimental import pallas as pl
from jax.experimental.pallas import tpu as pltpu

F32 = jnp.float32
BF16 = jnp.bfloat16

N_GROUPS = 4
N_HEADS = 8
HEAD_DIM = 128
N_IN_SPLITS = 14
N_MOD = 6
CTX_LEN = 256
GRID_W = 64
LRU_CONV_W = 4
LRU_C = 8.0
NA_WIN_ROWS = 8
NA_WIN_COLS = 16
RET_CHUNK = 128
HGRN_CHUNK = 32
PEER_HEADS = 8
PEER_NKEYS = 128
PEER_TOPK = 16
ROPE_BASE = 10000.0
EPS = 1e-6
NEG_INF = -1e30

SUBLANES = 8
LANES = 128
ROW_TILE = 256
ATTN_BLOCK = 128
MOD_ROWS = 16
PEER_TOKEN_BLOCK = 512
PEER_EXPERT_BLOCK = 512
PEER_LANE_CHUNK = 256
PEER_DOWN_GROUP = 256
VMEM_LIMIT = 56 * 1024 * 1024


def _cparams(sem, vmem=None):
    return pltpu.CompilerParams(dimension_semantics=sem, vmem_limit_bytes=vmem)


def _bdot(a, b):
    return jnp.dot(a.astype(BF16), b.astype(BF16), preferred_element_type=F32)


def _bdot_nt(a, b):
    return lax.dot_general(a.astype(BF16), b.astype(BF16), (((1,), (1,)), ((), ())),
                           preferred_element_type=F32)


def _bdot_tn(a, b):
    return lax.dot_general(a.astype(BF16), b.astype(BF16), (((0,), (0,)), ((), ())),
                           preferred_element_type=F32)


def _gelu(x):
    c = 0.7978845608028654 * LOG2_E
    return x / (1.0 + jnp.exp2(x * (-2.0 * c - (2.0 * c * 0.044715) * (x * x))))


LOG2_E = 1.4426950408889634


def _sigmoid(x):
    return 1.0 / (1.0 + jnp.exp2(x * (-LOG2_E)))


def _silu(x):
    return x * _sigmoid(x)


def _ada_kernel(c_ref, w_ref, b_ref, o_ref):
    x = c_ref[...]
    x = _silu(x)
    hi = x.astype(BF16)
    lo = (x - hi.astype(F32)).astype(BF16)
    w = w_ref[0]
    whi = w.astype(BF16)
    wlo = (w - whi.astype(F32)).astype(BF16)
    acc = jnp.dot(hi, whi, preferred_element_type=F32)
    acc += jnp.dot(hi, wlo, preferred_element_type=F32)
    acc += jnp.dot(lo, whi, preferred_element_type=F32)
    o_ref[...] = acc + b_ref[0]


def _ada_modulation(c16, w_all, b_all, layer):
    depth, d, n = w_all.shape
    tn = 512
    return pl.pallas_call(
        _ada_kernel,
        out_shape=jax.ShapeDtypeStruct((MOD_ROWS, n), F32),
        grid=(n // tn,),
        in_specs=[pl.BlockSpec((MOD_ROWS, d), lambda j: (0, 0)),
                  pl.BlockSpec((1, d, tn), lambda j: (layer, 0, j)),
                  pl.BlockSpec((1, 1, tn), lambda j: (layer, 0, j))],
        out_specs=pl.BlockSpec((MOD_ROWS, tn), lambda j: (0, j)),
        compiler_params=_cparams(("arbitrary",), VMEM_LIMIT),
        name="ada_modulation",
    )(c16, w_all, b_all.reshape(depth, 1, n))


def _resid_norm_kernel(*refs, has_f, has_norm, transpose_out):
    it = iter(refs)
    h_ref = next(it)
    if has_f:
        ft_ref = next(it)
        gate_ref = next(it)
    if has_norm:
        g_ref = next(it)
        sh_ref = next(it)
        sc_ref = next(it)
    if has_f:
        hn_ref = next(it)
    if has_norm:
        u_ref = next(it)
    x = h_ref[...]
    if has_f:
        x = x + gate_ref[0] * ft_ref[0].T
        hn_ref[...] = x
    if has_norm:
        ms = jnp.mean(x * x, axis=-1, keepdims=True)
        y = x * lax.rsqrt(ms + EPS) * g_ref[...]
        y = y * (1.0 + sc_ref[0]) + sh_ref[0]
        if transpose_out:
            u_ref[0] = y.T.astype(BF16)
        else:
            u_ref[...] = y.astype(BF16)


def _resid_norm(h, mod3, row_fn, *, ft=None, gate_mod3=None, gate_k=None, norm_g=None, shift_k=None,
                scale_k=None, transpose_out=False):
    t, d = h.shape
    nt = t // ROW_TILE
    tiles_per_chunk = PEER_TOKEN_BLOCK // ROW_TILE
    chunk_spec = pl.BlockSpec((1, d, ROW_TILE), lambda i: (i // tiles_per_chunk, 0, i % tiles_per_chunk))
    has_f = ft is not None
    has_norm = norm_g is not None
    in_specs = [pl.BlockSpec((ROW_TILE, d), lambda i: (i, 0))]
    args = [h]
    if has_f:
        in_specs += [chunk_spec,
                     pl.BlockSpec((1, 1, d), lambda i: (row_fn(i) * N_MOD + gate_k, 0, 0))]
        args += [ft, mod3 if gate_mod3 is None else gate_mod3]
    if has_norm:
        in_specs += [pl.BlockSpec((1, d), lambda i: (0, 0)),
                     pl.BlockSpec((1, 1, d), lambda i: (row_fn(i) * N_MOD + shift_k, 0, 0)),
                     pl.BlockSpec((1, 1, d), lambda i: (row_fn(i) * N_MOD + scale_k, 0, 0))]
        args += [norm_g.reshape(1, d), mod3, mod3]
    out_shape, out_specs = [], []
    if has_f:
        out_shape.append(jax.ShapeDtypeStruct((t, d), F32))
        out_specs.append(pl.BlockSpec((ROW_TILE, d), lambda i: (i, 0)))
    if has_norm:
        if transpose_out:
            out_shape.append(jax.ShapeDtypeStruct((t // PEER_TOKEN_BLOCK, d, PEER_TOKEN_BLOCK), BF16))
            out_specs.append(chunk_spec)
        else:
            out_shape.append(jax.ShapeDtypeStruct((t, d), BF16))
            out_specs.append(pl.BlockSpec((ROW_TILE, d), lambda i: (i, 0)))
    outs = pl.pallas_call(
        functools.partial(_resid_norm_kernel, has_f=has_f, has_norm=has_norm, transpose_out=transpose_out),
        out_shape=out_shape, grid=(nt,), in_specs=in_specs, out_specs=out_specs,
        compiler_params=_cparams(("arbitrary",), VMEM_LIMIT),
        name="resid_norm",
    )(*args)
    return outs


def _mm_kernel(x_ref, w_ref, o_ref, wb_ref):
    @pl.when(pl.program_id(1) == 0)
    def _():
        wb_ref[...] = w_ref[0].astype(BF16)

    r = jnp.dot(x_ref[...], wb_ref[...], preferred_element_type=F32)
    for hh in range(o_ref.shape[0]):
        o_ref[hh] = r[:, hh * HEAD_DIM:(hh + 1) * HEAD_DIM]


def _matmul(x, w_all, layer, tm, tn):
    m, k = x.shape
    n = w_all.shape[2]
    return pl.pallas_call(
        _mm_kernel,
        out_shape=jax.ShapeDtypeStruct((n // HEAD_DIM, m, HEAD_DIM), F32),
        grid=(n // tn, m // tm),
        in_specs=[pl.BlockSpec((tm, k), lambda j, i: (i, 0)),
                  pl.BlockSpec((1, k, tn), lambda j, i: (layer, 0, j), pipeline_mode=pl.Buffered(1))],
        out_specs=pl.BlockSpec((tn // HEAD_DIM, tm, HEAD_DIM), lambda j, i: (j, i, 0)),
        scratch_shapes=[pltpu.VMEM((k, tn), BF16)],
        compiler_params=_cparams(("arbitrary", "arbitrary"), VMEM_LIMIT),
        name="in_proj",
    )(x, w_all)


def _cast_kernel(w_ref, o_ref, *, transpose):
    w = w_ref[0]
    if transpose:
        o_ref[0] = w.T.astype(BF16)
    else:
        o_ref[...] = w.astype(BF16)


def _cast_weight(w_all, layer, *, transpose, rows_per_step):
    _, r, c = w_all.shape
    tr = rows_per_step
    if transpose:
        out_shape, out_spec = (r // tr, c, tr), pl.BlockSpec((1, c, tr), lambda i: (i, 0, 0))
    else:
        out_shape, out_spec = (r, c), pl.BlockSpec((tr, c), lambda i: (i, 0))
    return pl.pallas_call(
        functools.partial(_cast_kernel, transpose=transpose),
        out_shape=jax.ShapeDtypeStruct(out_shape, BF16),
        grid=(r // tr,),
        in_specs=[pl.BlockSpec((1, tr, c), lambda i: (layer, i, 0))],
        out_specs=out_spec,
        compiler_params=_cparams(("arbitrary",), VMEM_LIMIT),
        name="cast_weight",
    )(w_all)


def _head_slab_spec(bsz, l, split):
    return pl.BlockSpec((1, l, HEAD_DIM), lambda bb, h: ((split * N_HEADS + h) * bsz + bb, 0, 0))


def _wout_kernel(ya_ref, yb_ref, yc_ref, yd_ref, w_ref, h_ref, gate_ref, o_ref, wb_ref):
    @pl.when((pl.program_id(1) == 0) & (pl.program_id(2) == 0))
    def _():
        wb_ref[...] = w_ref[0].astype(BF16)

    gw = ya_ref.shape[2]
    acc = jnp.dot(ya_ref[0], wb_ref[0 * gw:1 * gw, :], preferred_element_type=F32)
    acc += jnp.dot(yb_ref[0], wb_ref[1 * gw:2 * gw, :], preferred_element_type=F32)
    acc += jnp.dot(yc_ref[0], wb_ref[2 * gw:3 * gw, :], preferred_element_type=F32)
    acc += jnp.dot(yd_ref[0], wb_ref[3 * gw:4 * gw, :], preferred_element_type=F32)
    o_ref[0] = h_ref[0] + gate_ref[0] * acc


def _out_proj(ys, w_all, layer, h3, mod3, *, row_off, row_fn, gate_k):
    b, l, gw = ys[0].shape
    d = w_all.shape[2]
    nrt = l // ROW_TILE - row_off
    tn = 1024
    y_spec = pl.BlockSpec((1, ROW_TILE, gw), lambda j, bb, i: (bb, i + row_off, 0))
    return pl.pallas_call(
        _wout_kernel,
        out_shape=jax.ShapeDtypeStruct((b, nrt * ROW_TILE, d), F32),
        grid=(d // tn, b, nrt),
        in_specs=[y_spec, y_spec, y_spec, y_spec,
                  pl.BlockSpec((1, N_GROUPS * gw, tn), lambda j, bb, i: (layer, 0, j),
                               pipeline_mode=pl.Buffered(1)),
                  pl.BlockSpec((1, ROW_TILE, tn), lambda j, bb, i: (bb, i + row_off, j)),
                  pl.BlockSpec((1, 1, tn), lambda j, bb, i: (row_fn(bb, i) * N_MOD + gate_k, 0, j))],
        out_specs=pl.BlockSpec((1, ROW_TILE, tn), lambda j, bb, i: (bb, i, j)),
        scratch_shapes=[pltpu.VMEM((N_GROUPS * gw, tn), BF16)],
        compiler_params=_cparams(("arbitrary", "arbitrary", "arbitrary"), VMEM_LIMIT),
        name="out_proj",
    )(*ys, w_all, h3, mod3)


def _group_cumsum(x, rowmod, c, reverse):
    n = x.shape[0]
    k = 1
    while k < c:
        if reverse:
            x = x + jnp.where(rowmod < c - k, pltpu.roll(x, n - k, 0), 0.0)
        else:
            x = x + jnp.where(rowmod >= k, pltpu.roll(x, k, 0), 0.0)
        k *= 2
    return x


def _group_linear_scan(a, u, rowmod, c, reverse):
    n, w = a.shape

    def shifted(x, s):
        return pltpu.roll(x.reshape(n // c, c, w), s, 1).reshape(n, w)

    k = 1
    while k < c:
        if reverse:
            ok = rowmod < c - k
            a_s = shifted(a, c - k)
            u_s = shifted(u, c - k)
        else:
            ok = rowmod >= k
            a_s = shifted(a, k)
            u_s = shifted(u, k)
        u = jnp.where(ok, a * u_s + u, u)
        a = jnp.where(ok, a * a_s, a)
        k *= 2
    return a, u


def _rglru_kernel(x_ref, g_ref, cw_ref, cb_ref, wa_ref, ba_ref, wx_ref, bx_ref, lam_ref, y_ref,
                  xp_ref, af_ref, uf_ref, ab_ref, ub_ref, *, seq_len):
    l = seq_len
    pad = SUBLANES
    nblk = l // ROW_TILE
    zeros_pad = jnp.zeros((pad, LANES), F32)
    xp_ref[0:pad, :] = zeros_pad
    xp_ref[pad + l:pad + l + pad, :] = zeros_pad
    xp_ref[pad:pad + l, :] = x_ref[0]

    row = lax.broadcasted_iota(jnp.int32, (ROW_TILE, LANES), 0)
    rowmod = row % SUBLANES
    cw = cw_ref[...]
    cb = cb_ref[...]
    lam = lam_ref[...]
    log2_decay = [(jnp.maximum(-lam[dd], 0.0) + jnp.log1p(jnp.exp(-jnp.abs(lam[dd])))) * (-LRU_C * LOG2_E)
                  for dd in range(2)]

    for blk in range(nblk):
        s0 = blk * ROW_TILE
        taps = []
        for tap in range(LRU_CONV_W):
            off = tap - LRU_CONV_W // 2
            v = xp_ref[pad + s0 + off:pad + s0 + off + ROW_TILE, :]
            if blk == 0 and off > 0:
                v = jnp.where(row + off < CTX_LEN, v, 0.0)
            if s0 == CTX_LEN and off < 0:
                v = jnp.where(row + off >= 0, v, 0.0)
            taps.append(v)
        xc = cb + taps[0] * cw[0:1]
        for tap in range(1, LRU_CONV_W):
            xc = xc + taps[tap] * cw[tap:tap + 1]
        xcb = xc.astype(BF16)
        for dd, (a_ref, u_ref) in enumerate(((af_ref, uf_ref), (ab_ref, ub_ref))):
            r = _sigmoid(jnp.dot(xcb, wa_ref[dd, 0].astype(BF16), preferred_element_type=F32) + ba_ref[dd])
            i = _sigmoid(jnp.dot(xcb, wx_ref[dd, 0].astype(BF16), preferred_element_type=F32) + bx_ref[dd])
            a = jnp.exp2(r * log2_decay[dd])
            y = 1.0 - a * a
            u = jnp.where(y > 0.0, y * lax.rsqrt(y), 0.0) * (i * xc)
            a_loc, u_loc = _group_linear_scan(a, u, rowmod, SUBLANES, reverse=(dd == 1))
            a_ref[s0:s0 + ROW_TILE, :] = a_loc
            u_ref[s0:s0 + ROW_TILE, :] = u_loc

    n_chunks = l // SUBLANES
    n_ctx_chunks = CTX_LEN // SUBLANES

    def fwd_body(c, carry):
        r0 = pl.multiple_of(c * SUBLANES, SUBLANES)
        hh = af_ref[pl.ds(r0, SUBLANES), :] * carry + uf_ref[pl.ds(r0, SUBLANES), :]
        uf_ref[pl.ds(r0, SUBLANES), :] = hh
        return jnp.broadcast_to(hh[SUBLANES - 1:SUBLANES, :], (SUBLANES, LANES))

    lax.fori_loop(0, n_chunks, fwd_body, jnp.zeros((SUBLANES, LANES), F32))

    def bwd_body(j, carry):
        c = jnp.where(j < n_ctx_chunks, n_ctx_chunks - 1 - j, n_chunks - 1 - (j - n_ctx_chunks))
        r0 = pl.multiple_of(c * SUBLANES, SUBLANES)
        hh = ab_ref[pl.ds(r0, SUBLANES), :] * carry + ub_ref[pl.ds(r0, SUBLANES), :]
        ub_ref[pl.ds(r0, SUBLANES), :] = hh
        return jnp.broadcast_to(hh[0:1, :], (SUBLANES, LANES))

    lax.fori_loop(0, n_chunks, bwd_body, jnp.zeros((SUBLANES, LANES), F32))

    for blk in range(nblk):
        s0 = blk * ROW_TILE
        hsum = uf_ref[s0:s0 + ROW_TILE, :] + ub_ref[s0:s0 + ROW_TILE, :]
        y_ref[0, s0:s0 + ROW_TILE, :] = (_gelu(g_ref[0, s0:s0 + ROW_TILE, :]) * hsum).astype(BF16)


def _rglru(p3, conv_w, conv_b, wa, ba, wx, bx, lam):
    l = p3.shape[1]
    b = p3.shape[0] // (N_IN_SPLITS * N_HEADS)
    gw = N_HEADS * HEAD_DIM
    vec = lambda a: a.reshape(2, 1, gw)
    spec_vec = pl.BlockSpec((2, 1, HEAD_DIM), lambda bb, h: (0, 0, h))
    spec_w = pl.BlockSpec((2, 1, HEAD_DIM, HEAD_DIM), lambda bb, h: (0, h, 0, 0))
    return pl.pallas_call(
        functools.partial(_rglru_kernel, seq_len=l),
        out_shape=jax.ShapeDtypeStruct((b, l, gw), BF16),
        grid=(b, N_HEADS),
        in_specs=[_head_slab_spec(b, l, 0), _head_slab_spec(b, l, 1),
                  pl.BlockSpec((LRU_CONV_W, HEAD_DIM), lambda bb, h: (0, h)),
                  pl.BlockSpec((1, HEAD_DIM), lambda bb, h: (0, h)),
                  spec_w, spec_vec, spec_w, spec_vec, spec_vec],
        out_specs=pl.BlockSpec((1, l, HEAD_DIM), lambda bb, h: (bb, 0, h)),
        scratch_shapes=[pltpu.VMEM((l + 2 * SUBLANES, LANES), F32)] + [pltpu.VMEM((l, LANES), F32)] * 4,
        compiler_params=_cparams(("arbitrary", "arbitrary"), VMEM_LIMIT),
        name="rglru",
    )(p3, p3, conv_w, conv_b.reshape(1, gw), wa, vec(ba), wx, vec(bx), vec(lam))


def _na_kernel(*refs, seq_len):
    nh = HEADS_PER_STEP
    q_refs, k_refs, v_refs = (refs[i * nh:(i + 1) * nh] for i in range(3))
    qg_ref, kg_ref, bias_ref, y_ref = refs[3 * nh:3 * nh + 4]
    scratch = refs[3 * nh + 4:]
    qb_refs, kb_refs, vb_refs = (scratch[i * nh:(i + 1) * nh] for i in range(3))
    l = seq_len
    scale = HEAD_DIM ** -0.5
    rows = (l - CTX_LEN) // GRID_W
    kr = min(NA_WIN_ROWS, rows)
    n_loc = kr * GRID_W

    def headnorm(z, g):
        return z * lax.rsqrt(jnp.mean(z * z, axis=-1, keepdims=True) + EPS) * g

    for hh in range(nh):
        cols = slice(hh * HEAD_DIM, (hh + 1) * HEAD_DIM)
        for s0 in range(0, l, ROW_TILE):
            rs = slice(s0, s0 + ROW_TILE)
            qb_refs[hh][rs, :] = headnorm(q_refs[hh][0, rs, :], qg_ref[...]).astype(BF16)
            kb_refs[hh][rs, :] = headnorm(k_refs[hh][0, rs, :], kg_ref[...]).astype(BF16)
            vb_refs[hh][rs, :] = v_refs[hh][0, rs, :].astype(BF16)
        k_ctx = kb_refs[hh][0:CTX_LEN, :]
        s_c = _bdot_nt(qb_refs[hh][0:CTX_LEN, :], k_ctx) * scale
        e_c = jnp.exp(s_c - jnp.max(s_c, axis=-1, keepdims=True))
        p_c = e_c / jnp.sum(e_c, axis=-1, keepdims=True)
        y_ref[0, 0:CTX_LEN, cols] = _bdot(p_c, vb_refs[hh][0:CTX_LEN, :]).astype(BF16)

    def scores(r, hh):
        r0 = jnp.clip(r - kr // 2, 0, rows - kr)
        qs = pl.multiple_of(CTX_LEN + r * GRID_W, GRID_W)
        ks = pl.multiple_of(CTX_LEN + r0 * GRID_W, GRID_W)
        q_r = qb_refs[hh][pl.ds(qs, GRID_W), :]
        bias = bias_ref[hh, r0 - r + NA_WIN_ROWS - 1]
        s_loc = _bdot_nt(q_r, kb_refs[hh][pl.ds(ks, n_loc), :]) * scale + bias
        s_ctx = _bdot_nt(q_r, kb_refs[hh][0:CTX_LEN, :]) * scale
        return hh, qs, ks, s_loc, s_ctx

    def softmax(hh, qs, ks, s_loc, s_ctx):
        m = jnp.maximum(jnp.max(s_loc, axis=-1, keepdims=True), jnp.max(s_ctx, axis=-1, keepdims=True))
        e_loc = jnp.exp(s_loc - m)
        e_ctx = jnp.exp(s_ctx - m)
        inv = 1.0 / (jnp.sum(e_loc, axis=-1, keepdims=True) + jnp.sum(e_ctx, axis=-1, keepdims=True))
        return hh, qs, ks, (e_loc * inv).astype(BF16), (e_ctx * inv).astype(BF16)

    def readout(hh, qs, ks, p_loc, p_ctx):
        o = _bdot(p_loc, vb_refs[hh][pl.ds(ks, n_loc), :]) + _bdot(p_ctx, vb_refs[hh][0:CTX_LEN, :])
        y_ref[0, pl.ds(qs, GRID_W), hh * HEAD_DIM:(hh + 1) * HEAD_DIM] = o.astype(BF16)

    group = 4 if rows % 4 == 0 else 1

    def rows_body(gi, carry):
        chains = [scores(gi * group + t, hh) for t in range(group) for hh in range(nh)]
        chains = [softmax(*ch) for ch in chains]
        for ch in chains:
            readout(*ch)
        return carry

    lax.fori_loop(0, rows // group, rows_body, 0)


def _na_bias_table(rpb, rows):
    kr = min(NA_WIN_ROWS, rows)
    qc = jnp.arange(GRID_W)[:, None]
    kc = jnp.arange(GRID_W)[None, :]
    win0 = jnp.clip(qc - NA_WIN_COLS // 2, 0, GRID_W - NA_WIN_COLS)
    col_ok = (kc >= win0) & (kc < win0 + NA_WIN_COLS)
    rel_c = jnp.clip(kc - qc + NA_WIN_COLS - 1, 0, 2 * NA_WIN_COLS - 2)
    t = jnp.where(col_ok[None, None], rpb.astype(F32)[:, :, rel_c], NEG_INF)
    d = jnp.arange(NA_WIN_ROWS)[:, None] + jnp.arange(kr)[None, :]
    d = jnp.clip(d, 0, 2 * NA_WIN_ROWS - 2)
    tb = t[:, d]
    return tb.transpose(0, 1, 3, 2, 4).reshape(rpb.shape[0], NA_WIN_ROWS, GRID_W, kr * GRID_W)


def _natten(p3, qn_g, kn_g, bias_tab):
    l = p3.shape[1]
    b = p3.shape[0] // (N_IN_SPLITS * N_HEADS)
    gw = N_HEADS * HEAD_DIM
    n_loc = bias_tab.shape[-1]
    nh = HEADS_PER_STEP
    slabs = [s for split in (2, 3, 4) for s in _pair_slab_specs(b, l, split)]
    return pl.pallas_call(
        functools.partial(_na_kernel, seq_len=l),
        out_shape=jax.ShapeDtypeStruct((b, l, gw), BF16),
        grid=(b, N_HEADS // nh),
        in_specs=slabs + [pl.BlockSpec((1, HEAD_DIM), lambda bb, hp: (0, 0)),
                          pl.BlockSpec((1, HEAD_DIM), lambda bb, hp: (0, 0)),
                          pl.BlockSpec((nh, NA_WIN_ROWS, GRID_W, n_loc), lambda bb, hp: (hp, 0, 0, 0))],
        out_specs=pl.BlockSpec((1, l, nh * HEAD_DIM), lambda bb, hp: (bb, 0, hp)),
        scratch_shapes=[pltpu.VMEM((l, LANES), BF16)] * (3 * nh),
        compiler_params=_cparams(("arbitrary", "arbitrary"), VMEM_LIMIT),
        name="natten",
    )(*([p3] * (3 * nh)), qn_g.reshape(1, HEAD_DIM), kn_g.reshape(1, HEAD_DIM), bias_tab)


def _dir_cums(lf, rowmod, c, reverse):
    if lf.shape[0] == 1:
        n_inc = (c - rowmod) if reverse else (rowmod + 1)
        n_oth = rowmod if reverse else (c - 1 - rowmod)
        inc = n_inc.astype(F32) * lf
        oth = n_oth.astype(F32) * lf
    else:
        n = lf.shape[0]
        row = lax.broadcasted_iota(jnp.int32, (n, n), 0)
        col = lax.broadcasted_iota(jnp.int32, (n, n), 1)
        same = (row // c) == (col // c)
        own = same & ((col >= row) if reverse else (col <= row))
        other = same & ((col < row) if reverse else (col > row))
        tri = jnp.concatenate([own, other], axis=0).astype(F32).astype(BF16)
        hi = lf.astype(BF16)
        r1 = lf - hi.astype(F32)
        mid = r1.astype(BF16)
        lo = (r1 - mid.astype(F32)).astype(BF16)
        acc = jnp.dot(tri, hi, preferred_element_type=F32)
        acc += jnp.dot(tri, mid, preferred_element_type=F32)
        acc += jnp.dot(tri, lo, preferred_element_type=F32)
        inc, oth = acc[:n], acc[n:]
    return inc, oth


def _decay_attn_blocks(s0, heads, c, ref_sets):
    n = ATTN_BLOCK
    row = lax.broadcasted_iota(jnp.int32, (n, n), 0)
    col = lax.broadcasted_iota(jnp.int32, (n, n), 1)
    rowmod = row % c
    same = (row // c) == (col // c)
    cums = [[_dir_cums(lf, rowmod, c, reverse) for reverse, lf in ((False, lf_f), (True, lf_b))]
            for (_, _, _, _, lf_f, lf_b) in heads]
    pending = []
    for hd, ((q, k_f, k_b, v, _, _), refs) in enumerate(zip(heads, ref_sets)):
        (qdf_ref, ksf_ref, ktf_ref, decf_ref, qdb_ref, ksb_ref, ktb_ref, decb_ref, vb_ref, o_ref) = refs
        vb = v.astype(BF16)
        vb_ref[pl.ds(s0, n), :] = vb
        scores = []
        for reverse, kk, (qd_ref, ks_ref, kt_ref, dec_ref) in (
                (False, k_f, (qdf_ref, ksf_ref, ktf_ref, decf_ref)),
                (True, k_b, (qdb_ref, ksb_ref, ktb_ref, decb_ref))):
            inc, oth = cums[hd][int(reverse)]
            kt = kk * jnp.exp(oth)
            dec = jnp.exp(inc + oth)
            qd = (q * jnp.exp(inc)).astype(BF16)
            ks = (kk * jnp.exp(-inc)).astype(BF16)
            qd_ref[pl.ds(s0, n), :] = qd
            ks_ref[pl.ds(s0, n), :] = ks
            kt_ref[pl.ds(s0, n), :] = kt.astype(BF16)
            dec_ref[pl.ds(s0, n), :] = dec
            scores.append((reverse, _bdot_nt(qd, ks)))
        pending.append((scores, vb, o_ref))
    for scores, vb, o_ref in pending:
        o = jnp.zeros((n, HEAD_DIM), F32)
        for reverse, s in scores:
            keep = same & ((col >= row) if reverse else (col <= row))
            o = o + jnp.dot(jnp.where(keep, s, 0.0).astype(BF16), vb, preferred_element_type=F32)
        o_ref[pl.ds(s0, n), :] = o


def _decay_attn_state_pass(c, seq_len, ref_sets):
    n_chunks = seq_len // c
    n_ctx_chunks = CTX_LEN // c

    steps = 4 if n_chunks % 4 == 0 else 3 if n_chunks % 3 == 0 else 1

    def body(jj, carry):
        chains = []
        for hh, refs in enumerate(ref_sets):
            (qdf_ref, _, ktf_ref, decf_ref, qdb_ref, _, ktb_ref, decb_ref, vb_ref, o_ref) = refs
            chains.append((qdf_ref, ktf_ref, decf_ref, vb_ref, o_ref, False))
            chains.append((qdb_ref, ktb_ref, decb_ref, vb_ref, o_ref, True))
        starts = []
        for t in range(steps):
            j = jj * steps + t
            jb = jnp.where(j < n_ctx_chunks, n_ctx_chunks - 1 - j, n_chunks - 1 - (j - n_ctx_chunks))
            starts.append((pl.multiple_of(j * c, c), pl.multiple_of(jb * c, c)))
        upds = [[_bdot_tn(vb_ref[pl.ds(starts[t][rev], c), :], kt_ref[pl.ds(starts[t][rev], c), :])
                 for t in range(steps)]
                for (_, kt_ref, _, vb_ref, _, rev) in chains]
        states = list(carry)
        for t in range(steps):
            for ci, (qd_ref, _, dec_ref, _, o_ref, rev) in enumerate(chains):
                r0 = starts[t][rev]
                o_ref[pl.ds(r0, c), :] += _bdot_nt(qd_ref[pl.ds(r0, c), :], states[ci])
                states[ci] = states[ci] * dec_ref[pl.ds(r0, 1), :] + upds[ci][t]
        return tuple(states)

    zero = jnp.zeros((HEAD_DIM, HEAD_DIM), F32)
    lax.fori_loop(0, n_chunks // steps, body, (zero,) * (2 * len(ref_sets)))


DECAY_SCRATCH_PER_HEAD = 10
HEADS_PER_STEP = 2


def _decay_scratch(l):
    one_dir = [pltpu.VMEM((l, LANES), BF16)] * 3 + [pltpu.VMEM((l, LANES), F32)]
    one_head = one_dir + one_dir + [pltpu.VMEM((l, LANES), BF16), pltpu.VMEM((l, LANES), F32)]
    assert len(one_head) == DECAY_SCRATCH_PER_HEAD
    return one_head * HEADS_PER_STEP


def _head_sets(scratch):
    n = DECAY_SCRATCH_PER_HEAD
    return [tuple(scratch[hh * n:(hh + 1) * n]) for hh in range(HEADS_PER_STEP)]


def _pair_slab_specs(bsz, l, split):
    return [pl.BlockSpec((1, l, HEAD_DIM),
                         lambda bb, hp, hh=hh: ((split * N_HEADS + hp * HEADS_PER_STEP + hh) * bsz + bb, 0, 0))
            for hh in range(HEADS_PER_STEP)]


def _retention_kernel(*refs, seq_len):
    nh = HEADS_PER_STEP
    q_refs, k_refs, v_refs, g_refs = (refs[i * nh:(i + 1) * nh] for i in range(4))
    cos_ref, sin_ref, lg_ref, gn_ref, y_ref = refs[4 * nh:4 * nh + 5]
    sets = _head_sets(refs[4 * nh + 5:])
    l = seq_len
    n = ATTN_BLOCK
    lane = lax.broadcasted_iota(jnp.int32, (n, HEAD_DIM), 1)
    first = (lane % (HEAD_DIM // 2)) < (HEAD_DIM // 4)

    def rope(z, cos, sin):
        swapped = jnp.where(first, pltpu.roll(z, HEAD_DIM - HEAD_DIM // 4, 1), pltpu.roll(z, HEAD_DIM // 4, 1))
        return z * cos + swapped * sin

    def block_body(bi, carry):
        s0 = pl.multiple_of(bi * n, n)
        cos = cos_ref[pl.ds(s0, n), :]
        sin = sin_ref[pl.ds(s0, n), :]
        heads = []
        for hh in range(nh):
            lg = lg_ref[hh]
            q = rope(q_refs[hh][0, pl.ds(s0, n), :], cos, sin)
            k = rope(k_refs[hh][0, pl.ds(s0, n), :], cos, sin) * (HEAD_DIM ** -0.5)
            heads.append((q, k, k, v_refs[hh][0, pl.ds(s0, n), :], lg[0:1], lg[1:2]))
        _decay_attn_blocks(s0, heads, RET_CHUNK, sets)
        return carry

    lax.fori_loop(0, l // n, block_body, 0)
    _decay_attn_state_pass(RET_CHUNK, l, sets)

    def out_body(bi, carry):
        s0 = pl.multiple_of(bi * n, n)
        for hh in range(nh):
            cols = slice(hh * HEAD_DIM, (hh + 1) * HEAD_DIM)
            o = sets[hh][-1][pl.ds(s0, n), :]
            o = o - jnp.mean(o, axis=-1, keepdims=True)
            o = o * lax.rsqrt(jnp.mean(o * o, axis=-1, keepdims=True) + EPS) * gn_ref[:, cols]
            y_ref[0, pl.ds(s0, n), cols] = (_silu(g_refs[hh][0, pl.ds(s0, n), :]) * o).astype(BF16)
        return carry

    lax.fori_loop(0, l // n, out_body, 0)


def _rope_tables(l):
    n_lat = l - CTX_LEN
    quarter = HEAD_DIM // 4
    t = jnp.arange(n_lat)
    inv_freq = ROPE_BASE ** (-jnp.arange(quarter, dtype=F32) / quarter)
    ang_r = (t // GRID_W).astype(F32)[:, None] * inv_freq[None, :]
    ang_c = (t % GRID_W).astype(F32)[:, None] * inv_freq[None, :]
    cos = jnp.concatenate([jnp.cos(ang_r), jnp.cos(ang_r), jnp.cos(ang_c), jnp.cos(ang_c)], axis=-1)
    sin = jnp.concatenate([-jnp.sin(ang_r), jnp.sin(ang_r), -jnp.sin(ang_c), jnp.sin(ang_c)], axis=-1)
    cos = jnp.concatenate([jnp.ones((CTX_LEN, HEAD_DIM), F32), cos], axis=0)
    sin = jnp.concatenate([jnp.zeros((CTX_LEN, HEAD_DIM), F32), sin], axis=0)
    return cos, sin


def _retention(p3, gn_g, cos, sin):
    l = p3.shape[1]
    b = p3.shape[0] // (N_IN_SPLITS * N_HEADS)
    gw = N_HEADS * HEAD_DIM
    log_gamma = jnp.log(1.0 - 2.0 ** (-5.0 - jnp.arange(N_HEADS, dtype=F32)))
    lg = jnp.stack([log_gamma, log_gamma[::-1]], axis=1)
    lg = jnp.broadcast_to(lg[:, :, None], (N_HEADS, 2, HEAD_DIM))
    nh = HEADS_PER_STEP
    full = pl.BlockSpec((l, HEAD_DIM), lambda bb, hp: (0, 0))
    slabs = [s for split in (5, 6, 7, 8) for s in _pair_slab_specs(b, l, split)]
    return pl.pallas_call(
        functools.partial(_retention_kernel, seq_len=l),
        out_shape=jax.ShapeDtypeStruct((b, l, gw), BF16),
        grid=(b, N_HEADS // nh),
        in_specs=slabs + [full, full,
                          pl.BlockSpec((nh, 2, HEAD_DIM), lambda bb, hp: (hp, 0, 0)),
                          pl.BlockSpec((1, nh * HEAD_DIM), lambda bb, hp: (0, hp))],
        out_specs=pl.BlockSpec((1, l, nh * HEAD_DIM), lambda bb, hp: (bb, 0, hp)),
        scratch_shapes=_decay_scratch(l),
        compiler_params=_cparams(("arbitrary", "arbitrary"), VMEM_LIMIT),
        name="retention",
    )(*([p3] * (4 * nh)), cos, sin, lg, gn_g.reshape(1, gw))


def _hgrn_kernel(*refs, seq_len, layer, depth):
    nh = HEADS_PER_STEP
    ff_refs, fb_refs, i_refs, q_refs, g_refs = (refs[i * nh:(i + 1) * nh] for i in range(5))
    lbl_ref, on_ref, y_ref = refs[5 * nh:5 * nh + 3]
    sets = _head_sets(refs[5 * nh + 3:])
    l = seq_len
    n = ATTN_BLOCK
    logits = lbl_ref[...]
    log_lb, log1m_lb = [], []
    for dd in range(2):
        xs = [logits[ll * 2 + dd] for ll in range(depth)]
        mx = functools.reduce(jnp.maximum, xs)
        es = [jnp.exp(x - mx) for x in xs]
        tot = functools.reduce(lambda a, b_: a + b_, es)
        lb = jnp.zeros_like(tot)
        for ll in range(1, layer + 1):
            lb = lb + es[ll] / tot
        log_lb.append(jnp.log(lb))
        log1m_lb.append(jnp.log1p(-lb))

    def gate(fz, dd, cols):
        log_sig = jnp.minimum(fz, 0.0) - jnp.log1p(jnp.exp(-jnp.abs(fz)))
        a = jnp.broadcast_to(log_lb[dd][:, cols], fz.shape)
        bv = log1m_lb[dd][:, cols] + log_sig
        log_f = jnp.maximum(a, bv) + jnp.log1p(jnp.exp(-jnp.abs(a - bv)))
        return log_f, 1.0 - jnp.exp(log_f)

    def block_body(bi, carry):
        s0 = pl.multiple_of(bi * n, n)
        heads = []
        for hh in range(nh):
            cols = slice(hh * HEAD_DIM, (hh + 1) * HEAD_DIM)
            lf_f, k_f = gate(ff_refs[hh][0, pl.ds(s0, n), :], 0, cols)
            lf_b, k_b = gate(fb_refs[hh][0, pl.ds(s0, n), :], 1, cols)
            heads.append((q_refs[hh][0, pl.ds(s0, n), :], k_f, k_b, i_refs[hh][0, pl.ds(s0, n), :], lf_f, lf_b))
        _decay_attn_blocks(s0, heads, HGRN_CHUNK, sets)
        return carry

    lax.fori_loop(0, l // n, block_body, 0)
    _decay_attn_state_pass(HGRN_CHUNK, l, sets)

    def out_body(bi, carry):
        s0 = pl.multiple_of(bi * n, n)
        for hh in range(nh):
            cols = slice(hh * HEAD_DIM, (hh + 1) * HEAD_DIM)
            o = sets[hh][-1][pl.ds(s0, n), :]
            o = o * lax.rsqrt(jnp.mean(o * o, axis=-1, keepdims=True) + EPS) * on_ref[:, cols]
            y_ref[0, pl.ds(s0, n), cols] = (_silu(g_refs[hh][0, pl.ds(s0, n), :]) * o).astype(BF16)
        return carry

    lax.fori_loop(0, l // n, out_body, 0)


def _hgrn2(p3, lb_logits, on_g, layer):
    l = p3.shape[1]
    b = p3.shape[0] // (N_IN_SPLITS * N_HEADS)
    gw = N_HEADS * HEAD_DIM
    depth = lb_logits.shape[0]
    nh = HEADS_PER_STEP
    slabs = [s for split in (9, 10, 11, 12, 13) for s in _pair_slab_specs(b, l, split)]
    return pl.pallas_call(
        functools.partial(_hgrn_kernel, seq_len=l, layer=layer, depth=depth),
        out_shape=jax.ShapeDtypeStruct((b, l, gw), BF16),
        grid=(b, N_HEADS // nh),
        in_specs=slabs + [pl.BlockSpec((depth * 2, 1, nh * HEAD_DIM), lambda bb, hp: (0, 0, hp)),
                          pl.BlockSpec((1, nh * HEAD_DIM), lambda bb, hp: (0, hp))],
        out_specs=pl.BlockSpec((1, l, nh * HEAD_DIM), lambda bb, hp: (bb, 0, hp)),
        scratch_shapes=_decay_scratch(l),
        compiler_params=_cparams(("arbitrary", "arbitrary"), VMEM_LIMIT),
        name="hgrn2",
    )(*([p3] * (5 * nh)), lb_logits.astype(F32).reshape(depth * 2, 1, gw), on_g.reshape(1, gw))


def _topk_rows(x, k):
    r_n = x.shape[0]
    idx = lax.broadcasted_iota(jnp.int32, x.shape, 0).astype(F32)
    rank = jnp.full(x.shape, float(k), F32)
    vals = []
    for r in range(k):
        m = jnp.max(x, axis=0, keepdims=True)
        first = jnp.min(jnp.where(x == m, idx, float(r_n)), axis=0, keepdims=True)
        sel = idx == first
        rank = jnp.where(sel, float(r), rank)
        x = jnp.where(sel, -jnp.inf, x)
        vals.append(m)
    return rank, jnp.concatenate(vals, axis=0)


def _bf16_pair_words(x):
    bits = pltpu.bitcast(x.astype(BF16).astype(F32), jnp.uint32)
    return bits | (bits >> 16)


def _peer_route_kernel(zt_ref, wq_ref, sk_ref, rank2_ref, lim1_ref, e1_ref, e2_ref, *, tb):
    k = PEER_TOPK
    n_kc, _, kc = wq_ref.shape
    q = jnp.dot(wq_ref[0], zt_ref[0, 0:kc, :], preferred_element_type=F32)
    for ci in range(1, n_kc):
        q += jnp.dot(wq_ref[ci], zt_ref[0, ci * kc:(ci + 1) * kc, :], preferred_element_type=F32)
    half = q.shape[0] // 2
    for c0 in range(0, tb, LANES):
        s1 = _bdot(sk_ref[0, 0], q[:half, c0:c0 + LANES])
        s2 = _bdot(sk_ref[0, 1], q[half:, c0:c0 + LANES])
        rank1, top1 = _topk_rows(s1, k)
        rank2, top2 = _topk_rows(s2, k)
        e1 = jnp.exp(s1 - top1[0:1])
        e2 = jnp.exp(s2 - top2[0:1])
        e1t = jnp.exp(top1 - top1[0:1])
        e2t = jnp.exp(top2 - top2[0:1])
        n_multi = sum(1 for r in range(k) if k // (r + 1) > 1)
        assert k - n_multi == SUBLANES
        pieces, epieces, widths = [], [], []
        for r in range(n_multi):
            n_c = k // (r + 1)
            n_pad = -(-n_c // SUBLANES) * SUBLANES
            valid = lax.broadcasted_iota(jnp.int32, (n_pad, LANES), 0) < n_c
            pieces.append(jnp.where(valid, top1[r:r + 1] + top2[0:n_pad], -jnp.inf))
            epieces.append(e1t[r:r + 1] * e2t[0:n_pad])
            widths.append(n_pad)
        pieces.append(top1[n_multi:k] + top2[0:1])
        epieces.append(e1t[n_multi:k] * e2t[0:1])
        cand = jnp.concatenate(pieces, axis=0)
        ecand = jnp.concatenate(epieces, axis=0)
        crank, _ = _topk_rows(cand, k)
        sel = crank < k
        z = jnp.sum(jnp.where(sel, ecand, 0.0), axis=0, keepdims=True)
        lim1 = jnp.zeros(rank1.shape, F32)
        self32 = sel.astype(F32)
        off = 0
        for r in range(n_multi):
            cnt = jnp.sum(self32[off:off + widths[r]], axis=0, keepdims=True)
            lim1 = jnp.where(rank1 == r, cnt, lim1)
            off += widths[r]
        for r in range(n_multi, k):
            lim1 = jnp.where(rank1 == r, self32[off + r - n_multi:off + r - n_multi + 1], lim1)
        rank2_ref[0, 0, :, c0:c0 + LANES] = rank2.astype(BF16)
        lim1_ref[0, 0, :, c0:c0 + LANES] = _bf16_pair_words(lim1)
        e1_ref[0, 0, :, c0:c0 + LANES] = _bf16_pair_words(e1)
        e2_ref[0, 0, :, c0:c0 + LANES] = (e2 / z).astype(BF16)


def _peer_route(zt, wq_t, sub_keys):
    nb, d, tb = zt.shape
    t = nb * tb
    n_kc, _, kc = wq_t.shape
    qd = wq_t.shape[1] // PEER_HEADS
    tabs = [jax.ShapeDtypeStruct((nb, PEER_HEADS, PEER_NKEYS, tb), dt)
            for dt in (BF16, jnp.uint32, jnp.uint32, BF16)]
    tab_spec = pl.BlockSpec((1, 1, PEER_NKEYS, tb), lambda i, h: (i, h, 0, 0))
    return pl.pallas_call(
        functools.partial(_peer_route_kernel, tb=tb),
        out_shape=tabs,
        grid=(t // tb, PEER_HEADS),
        in_specs=[pl.BlockSpec((1, d, tb), lambda i, h: (i, 0, 0)),
                  pl.BlockSpec((n_kc, qd, kc), lambda i, h: (0, h, 0)),
                  pl.BlockSpec((1, 2, PEER_NKEYS, qd // 2), lambda i, h: (h, 0, 0, 0))],
        out_specs=[tab_spec] * 4,
        compiler_params=_cparams(("arbitrary", "arbitrary"), VMEM_LIMIT),
        name="peer_route",
    )(zt, wq_t, sub_keys)


def _peer_expert_kernel(zt_ref, u_ref, vt_ref, rank2_ref, lim1_ref, e1_ref, e2_ref, o_ref, w_ref, *, ek):
    e = pl.program_id(1)
    tb = zt_ref.shape[2]
    n_i1 = ek // PEER_NKEYS

    @pl.when(e == 0)
    def _():
        o_ref[...] = jnp.zeros(o_ref.shape, F32)

    half = (PEER_NKEYS // 2, PEER_LANE_CHUNK)

    def gated(j, act):
        i1 = e * n_i1 + j
        for c0 in range(0, tb, PEER_LANE_CHUNK):
            cs = slice(c0, c0 + PEER_LANE_CHUNK)
            g = jnp.zeros((PEER_NKEYS, PEER_LANE_CHUNK), BF16)
            for h in range(PEER_HEADS):
                lim = pltpu.bitcast(jnp.broadcast_to(lim1_ref[0, h, pl.ds(i1, 1), cs], half), BF16)
                w1 = pltpu.bitcast(jnp.broadcast_to(e1_ref[0, h, pl.ds(i1, 1), cs], half), BF16)
                g = g + jnp.where(rank2_ref[0, h, :, cs] < lim, e2_ref[0, h, :, cs], jnp.zeros_like(g)) * w1
            w_ref[j * PEER_NKEYS:(j + 1) * PEER_NKEYS, cs] = (g.astype(F32) * _gelu(act[:, cs])).astype(BF16)

    acts = [jnp.dot(u_ref[j * PEER_NKEYS:(j + 1) * PEER_NKEYS, :], zt_ref[0], preferred_element_type=F32)
            for j in range(n_i1)]
    per_group = PEER_DOWN_GROUP // PEER_NKEYS
    for g0 in range(0, n_i1, per_group):
        for j in range(g0, g0 + per_group):
            gated(j, acts[j])
        ks = slice(g0 * PEER_NKEYS, (g0 + per_group) * PEER_NKEYS)
        o_ref[0] += jnp.dot(vt_ref[0, :, ks], w_ref[ks, :], preferred_element_type=F32)


def _peer_experts(zt, u, vt, tabs, ek):
    nb, d, tb = zt.shape
    ne = u.shape[0]
    once = pl.Buffered(1)
    tab_spec = pl.BlockSpec((1, PEER_HEADS, PEER_NKEYS, tb), lambda i, e: (i, 0, 0, 0), pipeline_mode=once)
    return pl.pallas_call(
        functools.partial(_peer_expert_kernel, ek=ek),
        out_shape=jax.ShapeDtypeStruct((nb, d, tb), F32),
        grid=(nb, ne // ek),
        in_specs=[pl.BlockSpec((1, d, tb), lambda i, e: (i, 0, 0), pipeline_mode=once),
                  pl.BlockSpec((ek, d), lambda i, e: (e, 0)),
                  pl.BlockSpec((1, d, ek), lambda i, e: (e, 0, 0)),
                  tab_spec, tab_spec, tab_spec, tab_spec],
        out_specs=pl.BlockSpec((1, d, tb), lambda i, e: (i, 0, 0)),
        scratch_shapes=[pltpu.VMEM((ek, tb), BF16)],
        compiler_params=_cparams(("arbitrary", "arbitrary"), VMEM_LIMIT),
        name="peer_experts",
    )(zt, u, vt, *tabs)


def kernel(x, c, ctx, c_ctx, norm1_g, norm2_g, ada_w, ada_b, w_in, w_out, lru_conv_w, lru_conv_b, lru_wa, lru_ba,
           lru_wx, lru_bx, lru_lam, na_qn_g, na_kn_g, na_rpb, ret_gn_g, hgrn_lb_logits, hgrn_on_g, peer_wq,
           peer_subkeys, peer_u, peer_v):
    bsz, n_lat, d = x.shape
    depth = w_in.shape[0]
    l = CTX_LEN + n_lat
    gw = d // N_GROUPS
    assert bsz < MOD_ROWS and n_lat % ROW_TILE == 0 and CTX_LEN % ROW_TILE == 0
    ctx_row = bsz

    c16 = jnp.zeros((MOD_ROWS, d), F32).at[:bsz].set(c).at[ctx_row].set(c_ctx)
    cos, sin = _rope_tables(l)
    rows = n_lat // GRID_W

    h3 = jnp.concatenate([ctx, x], axis=1)
    tiles_full = l // ROW_TILE
    tiles_lat = n_lat // ROW_TILE
    ctx_tiles = CTX_LEN // ROW_TILE

    def row_full(i):
        return jnp.where(i % tiles_full < ctx_tiles, ctx_row, i // tiles_full)

    def row_lat(i):
        return i // tiles_lat

    h = h3.reshape(bsz * l, d)
    pending = None
    for layer in range(depth):
        last = layer == depth - 1
        mod3 = _ada_modulation(c16, ada_w, ada_b, layer).reshape(MOD_ROWS * N_MOD, 1, d)
        if pending is None:
            (u,) = _resid_norm(h, mod3, row_full, norm_g=norm1_g[layer], shift_k=0, scale_k=1)
        else:
            ft, mod3_prev = pending
            h, u = _resid_norm(h, mod3, row_full, ft=ft, gate_mod3=mod3_prev, gate_k=5,
                               norm_g=norm1_g[layer], shift_k=0, scale_k=1)
        p = _matmul(u, w_in, layer, 1024, 1024)
        p3 = p.reshape(N_IN_SPLITS * N_HEADS * bsz, l, HEAD_DIM)
        y_a = _rglru(p3, lru_conv_w[layer], lru_conv_b[layer], lru_wa[layer], lru_ba[layer],
                     lru_wx[layer], lru_bx[layer], lru_lam[layer])
        y_b = _natten(p3, na_qn_g[layer], na_kn_g[layer], _na_bias_table(na_rpb[layer], rows))
        y_c = _retention(p3, ret_gn_g[layer], cos, sin)
        y_d = _hgrn2(p3, hgrn_lb_logits, hgrn_on_g[layer], layer)
        if last:
            hm3 = _out_proj((y_a, y_b, y_c, y_d), w_out, layer, h.reshape(bsz, l, d), mod3, row_off=ctx_tiles,
                            row_fn=lambda bb, i: bb, gate_k=2)
            row_fn = row_lat
        else:
            hm3 = _out_proj((y_a, y_b, y_c, y_d), w_out, layer, h.reshape(bsz, l, d), mod3, row_off=0,
                            row_fn=lambda bb, i: jnp.where(i < ctx_tiles, ctx_row, bb), gate_k=2)
            row_fn = row_full
        h = hm3.reshape(-1, d)
        (zt,) = _resid_norm(h, mod3, row_fn, norm_g=norm2_g[layer], shift_k=3, scale_k=4, transpose_out=True)
        wq_t = _cast_weight(peer_wq, layer, transpose=True, rows_per_step=512)
        u_b = _cast_weight(peer_u, layer, transpose=False, rows_per_step=512)
        v_t = _cast_weight(peer_v, layer, transpose=True, rows_per_step=512)
        tabs = _peer_route(zt, wq_t, peer_subkeys[layer])
        ft = _peer_experts(zt, u_b, v_t, tabs, PEER_EXPERT_BLOCK)
        if last:
            (h,) = _resid_norm(h, mod3, row_fn, ft=ft, gate_k=5)
        else:
            pending = (ft, mod3)
    return h.reshape(bsz, n_lat, d)
```

```python
import functools

import jax
import jax.numpy as jnp
from jax import lax
from jax.experimental import pallas as pl
from jax.experimental.pallas import tpu as pltpu

F32 = jnp.float32
BF16 = jnp.bfloat16

N_GROUPS = 4
N_HEADS = 8
HEAD_DIM = 128
N_IN_SPLITS = 14
N_MOD = 6
CTX_LEN = 256
GRID_W = 64
LRU_CONV_W = 4
LRU_C = 8.0
NA_WIN_ROWS = 8
NA_WIN_COLS = 16
RET_CHUNK = 128
HGRN_CHUNK = 32
PEER_HEADS = 8
PEER_NKEYS = 128
PEER_TOPK = 16
ROPE_BASE = 10000.0
EPS = 1e-6
NEG_INF = -1e30

SUBLANES = 8
LANES = 128
ROW_TILE = 256
ATTN_BLOCK = 128
MOD_ROWS = 16
PEER_TOKEN_BLOCK = 512
PEER_EXPERT_BLOCK = 1024
PEER_LANE_CHUNK = 256
PEER_DOWN_GROUP = 256
PEER_UP_BATCH = 1024
VMEM_LIMIT = 56 * 1024 * 1024
PEER_VMEM_LIMIT = 62 * 1024 * 1024


def _cparams(sem, vmem=None):
    return pltpu.CompilerParams(dimension_semantics=sem, vmem_limit_bytes=vmem)


def _bdot(a, b):
    return jnp.dot(a.astype(BF16), b.astype(BF16), preferred_element_type=F32)


def _bdot_nt(a, b):
    return lax.dot_general(a.astype(BF16), b.astype(BF16), (((1,), (1,)), ((), ())),
                           preferred_element_type=F32)


def _bdot_tn(a, b):
    return lax.dot_general(a.astype(BF16), b.astype(BF16), (((0,), (0,)), ((), ())),
                           preferred_element_type=F32)


def _gelu(x):
    c = 0.7978845608028654 * LOG2_E
    return x / (1.0 + jnp.exp2(x * (-2.0 * c - (2.0 * c * 0.044715) * (x * x))))


LOG2_E = 1.4426950408889634


def _sigmoid(x):
    return 1.0 / (1.0 + jnp.exp2(x * (-LOG2_E)))


def _silu(x):
    return x * _sigmoid(x)


def _ada_kernel(c_ref, w_ref, b_ref, o_ref):
    x = c_ref[...]
    x = _silu(x)
    hi = x.astype(BF16)
    lo = (x - hi.astype(F32)).astype(BF16)
    w = w_ref[0]
    whi = w.astype(BF16)
    wlo = (w - whi.astype(F32)).astype(BF16)
    acc = jnp.dot(hi, whi, preferred_element_type=F32)
    acc += jnp.dot(hi, wlo, preferred_element_type=F32)
    acc += jnp.dot(lo, whi, preferred_element_type=F32)
    o_ref[...] = acc + b_ref[0]


def _ada_modulation(c16, w_all, b_all, layer):
    depth, d, n = w_all.shape
    tn = 512
    return pl.pallas_call(
        _ada_kernel,
        out_shape=jax.ShapeDtypeStruct((MOD_ROWS, n), F32),
        grid=(n // tn,),
        in_specs=[pl.BlockSpec((MOD_ROWS, d), lambda j: (0, 0)),
                  pl.BlockSpec((1, d, tn), lambda j: (layer, 0, j)),
                  pl.BlockSpec((1, 1, tn), lambda j: (layer, 0, j))],
        out_specs=pl.BlockSpec((MOD_ROWS, tn), lambda j: (0, j)),
        compiler_params=_cparams(("arbitrary",), VMEM_LIMIT),
        name="ada_modulation",
    )(c16, w_all, b_all.reshape(depth, 1, n))


def _resid_norm_kernel(*refs, has_f, has_norm, transpose_out):
    it = iter(refs)
    h_ref = next(it)
    if has_f:
        ft_ref = next(it)
        gate_ref = next(it)
    if has_norm:
        g_ref = next(it)
        sh_ref = next(it)
        sc_ref = next(it)
    if has_f:
        hn_ref = next(it)
    if has_norm:
        u_ref = next(it)
    x = h_ref[...]
    if has_f:
        x = x + gate_ref[0] * ft_ref[0].T
        hn_ref[...] = x
    if has_norm:
        ms = jnp.mean(x * x, axis=-1, keepdims=True)
        y = x * lax.rsqrt(ms + EPS) * g_ref[...]
        y = y * (1.0 + sc_ref[0]) + sh_ref[0]
        if transpose_out:
            u_ref[0] = y.T.astype(BF16)
        else:
            u_ref[...] = y.astype(BF16)


def _resid_norm(h, mod3, row_fn, *, ft=None, gate_mod3=None, gate_k=None, norm_g=None, shift_k=None,
                scale_k=None, transpose_out=False):
    t, d = h.shape
    nt = t // ROW_TILE
    tiles_per_chunk = PEER_TOKEN_BLOCK // ROW_TILE
    chunk_spec = pl.BlockSpec((1, d, ROW_TILE), lambda i: (i // tiles_per_chunk, 0, i % tiles_per_chunk))
    has_f = ft is not None
    has_norm = norm_g is not None
    in_specs = [pl.BlockSpec((ROW_TILE, d), lambda i: (i, 0))]
    args = [h]
    if has_f:
        in_specs += [chunk_spec,
                     pl.BlockSpec((1, 1, d), lambda i: (row_fn(i) * N_MOD + gate_k, 0, 0))]
        args += [ft, mod3 if gate_mod3 is None else gate_mod3]
    if has_norm:
        in_specs += [pl.BlockSpec((1, d), lambda i: (0, 0)),
                     pl.BlockSpec((1, 1, d), lambda i: (row_fn(i) * N_MOD + shift_k, 0, 0)),
                     pl.BlockSpec((1, 1, d), lambda i: (row_fn(i) * N_MOD + scale_k, 0, 0))]
        args += [norm_g.reshape(1, d), mod3, mod3]
    out_shape, out_specs = [], []
    if has_f:
        out_shape.append(jax.ShapeDtypeStruct((t, d), F32))
        out_specs.append(pl.BlockSpec((ROW_TILE, d), lambda i: (i, 0)))
    if has_norm:
        if transpose_out:
            out_shape.append(jax.ShapeDtypeStruct((t // PEER_TOKEN_BLOCK, d, PEER_TOKEN_BLOCK), BF16))
            out_specs.append(chunk_spec)
        else:
            out_shape.append(jax.ShapeDtypeStruct((t, d), BF16))
            out_specs.append(pl.BlockSpec((ROW_TILE, d), lambda i: (i, 0)))
    outs = pl.pallas_call(
        functools.partial(_resid_norm_kernel, has_f=has_f, has_norm=has_norm, transpose_out=transpose_out),
        out_shape=out_shape, grid=(nt,), in_specs=in_specs, out_specs=out_specs,
        compiler_params=_cparams(("arbitrary",), VMEM_LIMIT),
        name="resid_norm",
    )(*args)
    return outs


def _mm_kernel(x_ref, w_ref, o_ref, wb_ref):
    @pl.when(pl.program_id(1) == 0)
    def _():
        wb_ref[...] = w_ref[0].astype(BF16)

    r = jnp.dot(x_ref[...], wb_ref[...], preferred_element_type=F32)
    for hh in range(o_ref.shape[0]):
        o_ref[hh] = r[:, hh * HEAD_DIM:(hh + 1) * HEAD_DIM]


def _matmul(x, w_all, layer, tm, tn):
    m, k = x.shape
    n = w_all.shape[2]
    return pl.pallas_call(
        _mm_kernel,
        out_shape=jax.ShapeDtypeStruct((n // HEAD_DIM, m, HEAD_DIM), F32),
        grid=(n // tn, m // tm),
        in_specs=[pl.BlockSpec((tm, k), lambda j, i: (i, 0)),
                  pl.BlockSpec((1, k, tn), lambda j, i: (layer, 0, j), pipeline_mode=pl.Buffered(1))],
        out_specs=pl.BlockSpec((tn // HEAD_DIM, tm, HEAD_DIM), lambda j, i: (j, i, 0)),
        scratch_shapes=[pltpu.VMEM((k, tn), BF16)],
        compiler_params=_cparams(("arbitrary", "arbitrary"), VMEM_LIMIT),
        name="in_proj",
    )(x, w_all)


def _cast_kernel(w_ref, o_ref, *, transpose):
    w = w_ref[0]
    if transpose:
        o_ref[0] = w.T.astype(BF16)
    else:
        o_ref[...] = w.astype(BF16)


def _cast_weight(w_all, layer, *, transpose, rows_per_step):
    _, r, c = w_all.shape
    tr = rows_per_step
    if transpose:
        out_shape, out_spec = (r // tr, c, tr), pl.BlockSpec((1, c, tr), lambda i: (i, 0, 0))
    else:
        out_shape, out_spec = (r, c), pl.BlockSpec((tr, c), lambda i: (i, 0))
    return pl.pallas_call(
        functools.partial(_cast_kernel, transpose=transpose),
        out_shape=jax.ShapeDtypeStruct(out_shape, BF16),
        grid=(r // tr,),
        in_specs=[pl.BlockSpec((1, tr, c), lambda i: (layer, i, 0))],
        out_specs=out_spec,
        compiler_params=_cparams(("arbitrary",), VMEM_LIMIT),
        name="cast_weight",
    )(w_all)


def _head_slab_spec(bsz, l, split):
    return pl.BlockSpec((1, l, HEAD_DIM), lambda bb, h: ((split * N_HEADS + h) * bsz + bb, 0, 0))


def _wout_kernel(ya_ref, yb_ref, yc_ref, yd_ref, w_ref, h_ref, gate_ref, o_ref, wb_ref):
    @pl.when((pl.program_id(1) == 0) & (pl.program_id(2) == 0))
    def _():
        wb_ref[...] = w_ref[0].astype(BF16)

    gw = ya_ref.shape[2]
    acc = jnp.dot(ya_ref[0], wb_ref[0 * gw:1 * gw, :], preferred_element_type=F32)
    acc += jnp.dot(yb_ref[0], wb_ref[1 * gw:2 * gw, :], preferred_element_type=F32)
    acc += jnp.dot(yc_ref[0], wb_ref[2 * gw:3 * gw, :], preferred_element_type=F32)
    acc += jnp.dot(yd_ref[0], wb_ref[3 * gw:4 * gw, :], preferred_element_type=F32)
    o_ref[0] = h_ref[0] + gate_ref[0] * acc


def _out_proj(ys, w_all, layer, h3, mod3, *, row_off, row_fn, gate_k):
    b, l, gw = ys[0].shape
    d = w_all.shape[2]
    nrt = l // ROW_TILE - row_off
    tn = 1024
    y_spec = pl.BlockSpec((1, ROW_TILE, gw), lambda j, bb, i: (bb, i + row_off, 0))
    return pl.pallas_call(
        _wout_kernel,
        out_shape=jax.ShapeDtypeStruct((b, nrt * ROW_TILE, d), F32),
        grid=(d // tn, b, nrt),
        in_specs=[y_spec, y_spec, y_spec, y_spec,
                  pl.BlockSpec((1, N_GROUPS * gw, tn), lambda j, bb, i: (layer, 0, j),
                               pipeline_mode=pl.Buffered(1)),
                  pl.BlockSpec((1, ROW_TILE, tn), lambda j, bb, i: (bb, i + row_off, j)),
                  pl.BlockSpec((1, 1, tn), lambda j, bb, i: (row_fn(bb, i) * N_MOD + gate_k, 0, j))],
        out_specs=pl.BlockSpec((1, ROW_TILE, tn), lambda j, bb, i: (bb, i, j)),
        scratch_shapes=[pltpu.VMEM((N_GROUPS * gw, tn), BF16)],
        compiler_params=_cparams(("arbitrary", "arbitrary", "arbitrary"), VMEM_LIMIT),
        name="out_proj",
    )(*ys, w_all, h3, mod3)


def _group_cumsum(x, rowmod, c, reverse):
    n = x.shape[0]
    k = 1
    while k < c:
        if reverse:
            x = x + jnp.where(rowmod < c - k, pltpu.roll(x, n - k, 0), 0.0)
        else:
            x = x + jnp.where(rowmod >= k, pltpu.roll(x, k, 0), 0.0)
        k *= 2
    return x


def _group_linear_scan(a, u, rowmod, c, reverse):
    n, w = a.shape

    def shifted(x, s):
        return pltpu.roll(x.reshape(n // c, c, w), s, 1).reshape(n, w)

    k = 1
    while k < c:
        if reverse:
            ok = rowmod < c - k
            a_s = shifted(a, c - k)
            u_s = shifted(u, c - k)
        else:
            ok = rowmod >= k
            a_s = shifted(a, k)
            u_s = shifted(u, k)
        u = jnp.where(ok, a * u_s + u, u)
        a = jnp.where(ok, a * a_s, a)
        k *= 2
    return a, u


def _rglru_kernel(x_ref, g_ref, cw_ref, cb_ref, wa_ref, ba_ref, wx_ref, bx_ref, lam_ref, y_ref,
                  xp_ref, af_ref, uf_ref, ab_ref, ub_ref, *, seq_len):
    l = seq_len
    pad = SUBLANES
    nblk = l // ROW_TILE
    zeros_pad = jnp.zeros((pad, LANES), F32)
    xp_ref[0:pad, :] = zeros_pad
    xp_ref[pad + l:pad + l + pad, :] = zeros_pad
    xp_ref[pad:pad + l, :] = x_ref[0]

    row = lax.broadcasted_iota(jnp.int32, (ROW_TILE, LANES), 0)
    rowmod = row % SUBLANES
    cw = cw_ref[...]
    cb = cb_ref[...]
    lam = lam_ref[...]
    log2_decay = [(jnp.maximum(-lam[dd], 0.0) + jnp.log1p(jnp.exp(-jnp.abs(lam[dd])))) * (-LRU_C * LOG2_E)
                  for dd in range(2)]

    for blk in range(nblk):
        s0 = blk * ROW_TILE
        taps = []
        for tap in range(LRU_CONV_W):
            off = tap - LRU_CONV_W // 2
            v = xp_ref[pad + s0 + off:pad + s0 + off + ROW_TILE, :]
            if blk == 0 and off > 0:
                v = jnp.where(row + off < CTX_LEN, v, 0.0)
            if s0 == CTX_LEN and off < 0:
                v = jnp.where(row + off >= 0, v, 0.0)
            taps.append(v)
        xc = cb + taps[0] * cw[0:1]
        for tap in range(1, LRU_CONV_W):
            xc = xc + taps[tap] * cw[tap:tap + 1]
        xcb = xc.astype(BF16)
        for dd, (a_ref, u_ref) in enumerate(((af_ref, uf_ref), (ab_ref, ub_ref))):
            r = _sigmoid(jnp.dot(xcb, wa_ref[dd, 0].astype(BF16), preferred_element_type=F32) + ba_ref[dd])
            i = _sigmoid(jnp.dot(xcb, wx_ref[dd, 0].astype(BF16), preferred_element_type=F32) + bx_ref[dd])
            a = jnp.exp2(r * log2_decay[dd])
            y = 1.0 - a * a
            u = jnp.where(y > 0.0, y * lax.rsqrt(y), 0.0) * (i * xc)
            a_loc, u_loc = _group_linear_scan(a, u, rowmod, SUBLANES, reverse=(dd == 1))
            a_ref[s0:s0 + ROW_TILE, :] = a_loc
            u_ref[s0:s0 + ROW_TILE, :] = u_loc

    n_chunks = l // SUBLANES
    n_ctx_chunks = CTX_LEN // SUBLANES

    def fwd_body(c, carry):
        r0 = pl.multiple_of(c * SUBLANES, SUBLANES)
        hh = af_ref[pl.ds(r0, SUBLANES), :] * carry + uf_ref[pl.ds(r0, SUBLANES), :]
        uf_ref[pl.ds(r0, SUBLANES), :] = hh
        return jnp.broadcast_to(hh[SUBLANES - 1:SUBLANES, :], (SUBLANES, LANES))

    lax.fori_loop(0, n_chunks, fwd_body, jnp.zeros((SUBLANES, LANES), F32))

    def bwd_body(j, carry):
        c = jnp.where(j < n_ctx_chunks, n_ctx_chunks - 1 - j, n_chunks - 1 - (j - n_ctx_chunks))
        r0 = pl.multiple_of(c * SUBLANES, SUBLANES)
        hh = ab_ref[pl.ds(r0, SUBLANES), :] * carry + ub_ref[pl.ds(r0, SUBLANES), :]
        ub_ref[pl.ds(r0, SUBLANES), :] = hh
        return jnp.broadcast_to(hh[0:1, :], (SUBLANES, LANES))

    lax.fori_loop(0, n_chunks, bwd_body, jnp.zeros((SUBLANES, LANES), F32))

    for blk in range(nblk):
        s0 = blk * ROW_TILE
        hsum = uf_ref[s0:s0 + ROW_TILE, :] + ub_ref[s0:s0 + ROW_TILE, :]
        y_ref[0, s0:s0 + ROW_TILE, :] = (_gelu(g_ref[0, s0:s0 + ROW_TILE, :]) * hsum).astype(BF16)


def _rglru(p3, conv_w, conv_b, wa, ba, wx, bx, lam):
    l = p3.shape[1]
    b = p3.shape[0] // (N_IN_SPLITS * N_HEADS)
    gw = N_HEADS * HEAD_DIM
    vec = lambda a: a.reshape(2, 1, gw)
    spec_vec = pl.BlockSpec((2, 1, HEAD_DIM), lambda bb, h: (0, 0, h))
    spec_w = pl.BlockSpec((2, 1, HEAD_DIM, HEAD_DIM), lambda bb, h: (0, h, 0, 0))
    return pl.pallas_call(
        functools.partial(_rglru_kernel, seq_len=l),
        out_shape=jax.ShapeDtypeStruct((b, l, gw), BF16),
        grid=(b, N_HEADS),
        in_specs=[_head_slab_spec(b, l, 0), _head_slab_spec(b, l, 1),
                  pl.BlockSpec((LRU_CONV_W, HEAD_DIM), lambda bb, h: (0, h)),
                  pl.BlockSpec((1, HEAD_DIM), lambda bb, h: (0, h)),
                  spec_w, spec_vec, spec_w, spec_vec, spec_vec],
        out_specs=pl.BlockSpec((1, l, HEAD_DIM), lambda bb, h: (bb, 0, h)),
        scratch_shapes=[pltpu.VMEM((l + 2 * SUBLANES, LANES), F32)] + [pltpu.VMEM((l, LANES), F32)] * 4,
        compiler_params=_cparams(("arbitrary", "arbitrary"), VMEM_LIMIT),
        name="rglru",
    )(p3, p3, conv_w, conv_b.reshape(1, gw), wa, vec(ba), wx, vec(bx), vec(lam))


def _na_kernel(*refs, seq_len):
    nh = HEADS_PER_STEP
    q_refs, k_refs, v_refs = (refs[i * nh:(i + 1) * nh] for i in range(3))
    qg_ref, kg_ref, bias_ref, y_ref = refs[3 * nh:3 * nh + 4]
    scratch = refs[3 * nh + 4:]
    qb_refs, kb_refs, vb_refs = (scratch[i * nh:(i + 1) * nh] for i in range(3))
    l = seq_len
    scale = HEAD_DIM ** -0.5
    rows = (l - CTX_LEN) // GRID_W
    kr = min(NA_WIN_ROWS, rows)
    n_loc = kr * GRID_W

    def headnorm(z, g):
        return z * lax.rsqrt(jnp.mean(z * z, axis=-1, keepdims=True) + EPS) * g

    for hh in range(nh):
        cols = slice(hh * HEAD_DIM, (hh + 1) * HEAD_DIM)
        for s0 in range(0, l, ROW_TILE):
            rs = slice(s0, s0 + ROW_TILE)
            qb_refs[hh][rs, :] = headnorm(q_refs[hh][0, rs, :], qg_ref[...]).astype(BF16)
            kb_refs[hh][rs, :] = headnorm(k_refs[hh][0, rs, :], kg_ref[...]).astype(BF16)
            vb_refs[hh][rs, :] = v_refs[hh][0, rs, :].astype(BF16)
        k_ctx = kb_refs[hh][0:CTX_LEN, :]
        s_c = _bdot_nt(qb_refs[hh][0:CTX_LEN, :], k_ctx) * scale
        e_c = jnp.exp(s_c - jnp.max(s_c, axis=-1, keepdims=True))
        p_c = e_c / jnp.sum(e_c, axis=-1, keepdims=True)
        y_ref[0, 0:CTX_LEN, cols] = _bdot(p_c, vb_refs[hh][0:CTX_LEN, :]).astype(BF16)

    def scores(r, hh):
        r0 = jnp.clip(r - kr // 2, 0, rows - kr)
        qs = pl.multiple_of(CTX_LEN + r * GRID_W, GRID_W)
        ks = pl.multiple_of(CTX_LEN + r0 * GRID_W, GRID_W)
        q_r = qb_refs[hh][pl.ds(qs, GRID_W), :]
        bias = bias_ref[hh, r0 - r + NA_WIN_ROWS - 1]
        s_loc = _bdot_nt(q_r, kb_refs[hh][pl.ds(ks, n_loc), :]) * scale + bias
        s_ctx = _bdot_nt(q_r, kb_refs[hh][0:CTX_LEN, :]) * scale
        return hh, qs, ks, s_loc, s_ctx

    def softmax(hh, qs, ks, s_loc, s_ctx):
        m = jnp.maximum(jnp.max(s_loc, axis=-1, keepdims=True), jnp.max(s_ctx, axis=-1, keepdims=True))
        e_loc = jnp.exp(s_loc - m)
        e_ctx = jnp.exp(s_ctx - m)
        inv = 1.0 / (jnp.sum(e_loc, axis=-1, keepdims=True) + jnp.sum(e_ctx, axis=-1, keepdims=True))
        return hh, qs, ks, (e_loc * inv).astype(BF16), (e_ctx * inv).astype(BF16)

    def readout(hh, qs, ks, p_loc, p_ctx):
        o = _bdot(p_loc, vb_refs[hh][pl.ds(ks, n_loc), :]) + _bdot(p_ctx, vb_refs[hh][0:CTX_LEN, :])
        y_ref[0, pl.ds(qs, GRID_W), hh * HEAD_DIM:(hh + 1) * HEAD_DIM] = o.astype(BF16)

    group = 4 if rows % 4 == 0 else 1

    def rows_body(gi, carry):
        chains = [scores(gi * group + t, hh) for t in range(group) for hh in range(nh)]
        chains = [softmax(*ch) for ch in chains]
        for ch in chains:
            readout(*ch)
        return carry

    lax.fori_loop(0, rows // group, rows_body, 0)


def _na_bias_table(rpb, rows):
    kr = min(NA_WIN_ROWS, rows)
    qc = jnp.arange(GRID_W)[:, None]
    kc = jnp.arange(GRID_W)[None, :]
    win0 = jnp.clip(qc - NA_WIN_COLS // 2, 0, GRID_W - NA_WIN_COLS)
    col_ok = (kc >= win0) & (kc < win0 + NA_WIN_COLS)
    rel_c = jnp.clip(kc - qc + NA_WIN_COLS - 1, 0, 2 * NA_WIN_COLS - 2)
    t = jnp.where(col_ok[None, None], rpb.astype(F32)[:, :, rel_c], NEG_INF)
    d = jnp.arange(NA_WIN_ROWS)[:, None] + jnp.arange(kr)[None, :]
    d = jnp.clip(d, 0, 2 * NA_WIN_ROWS - 2)
    tb = t[:, d]
    return tb.transpose(0, 1, 3, 2, 4).reshape(rpb.shape[0], NA_WIN_ROWS, GRID_W, kr * GRID_W)


def _natten(p3, qn_g, kn_g, bias_tab):
    l = p3.shape[1]
    b = p3.shape[0] // (N_IN_SPLITS * N_HEADS)
    gw = N_HEADS * HEAD_DIM
    n_loc = bias_tab.shape[-1]
    nh = HEADS_PER_STEP
    slabs = [s for split in (2, 3, 4) for s in _pair_slab_specs(b, l, split)]
    return pl.pallas_call(
        functools.partial(_na_kernel, seq_len=l),
        out_shape=jax.ShapeDtypeStruct((b, l, gw), BF16),
        grid=(b, N_HEADS // nh),
        in_specs=slabs + [pl.BlockSpec((1, HEAD_DIM), lambda bb, hp: (0, 0)),
                          pl.BlockSpec((1, HEAD_DIM), lambda bb, hp: (0, 0)),
                          pl.BlockSpec((nh, NA_WIN_ROWS, GRID_W, n_loc), lambda bb, hp: (hp, 0, 0, 0))],
        out_specs=pl.BlockSpec((1, l, nh * HEAD_DIM), lambda bb, hp: (bb, 0, hp)),
        scratch_shapes=[pltpu.VMEM((l, LANES), BF16)] * (3 * nh),
        compiler_params=_cparams(("arbitrary", "arbitrary"), VMEM_LIMIT),
        name="natten",
    )(*([p3] * (3 * nh)), qn_g.reshape(1, HEAD_DIM), kn_g.reshape(1, HEAD_DIM), bias_tab)


def _dir_cums(lf, rowmod, c, reverse):
    if lf.shape[0] == 1:
        n_inc = (c - rowmod) if reverse else (rowmod + 1)
        n_oth = rowmod if reverse else (c - 1 - rowmod)
        inc = n_inc.astype(F32) * lf
        oth = n_oth.astype(F32) * lf
    else:
        n = lf.shape[0]
        row = lax.broadcasted_iota(jnp.int32, (n, n), 0)
        col = lax.broadcasted_iota(jnp.int32, (n, n), 1)
        same = (row // c) == (col // c)
        own = same & ((col >= row) if reverse else (col <= row))
        other = same & ((col < row) if reverse else (col > row))
        tri = jnp.concatenate([own, other], axis=0).astype(F32).astype(BF16)
        hi = lf.astype(BF16)
        r1 = lf - hi.astype(F32)
        mid = r1.astype(BF16)
        lo = (r1 - mid.astype(F32)).astype(BF16)
        acc = jnp.dot(tri, hi, preferred_element_type=F32)
        acc += jnp.dot(tri, mid, preferred_element_type=F32)
        acc += jnp.dot(tri, lo, preferred_element_type=F32)
        inc, oth = acc[:n], acc[n:]
    return inc, oth


def _decay_attn_blocks(s0, heads, c, ref_sets):
    n = ATTN_BLOCK
    row = lax.broadcasted_iota(jnp.int32, (n, n), 0)
    col = lax.broadcasted_iota(jnp.int32, (n, n), 1)
    rowmod = row % c
    same = (row // c) == (col // c)
    cums = [[_dir_cums(lf, rowmod, c, reverse) for reverse, lf in ((False, lf_f), (True, lf_b))]
            for (_, _, _, _, lf_f, lf_b) in heads]
    pending = []
    for hd, ((q, k_f, k_b, v, _, _), refs) in enumerate(zip(heads, ref_sets)):
        (qdf_ref, ksf_ref, ktf_ref, decf_ref, qdb_ref, ksb_ref, ktb_ref, decb_ref, vb_ref, o_ref) = refs
        vb = v.astype(BF16)
        vb_ref[pl.ds(s0, n), :] = vb
        scores = []
        for reverse, kk, (qd_ref, ks_ref, kt_ref, dec_ref) in (
                (False, k_f, (qdf_ref, ksf_ref, ktf_ref, decf_ref)),
                (True, k_b, (qdb_ref, ksb_ref, ktb_ref, decb_ref))):
            inc, oth = cums[hd][int(reverse)]
            kt = kk * jnp.exp(oth)
            dec = jnp.exp(inc + oth)
            qd = (q * jnp.exp(inc)).astype(BF16)
            ks = (kk * jnp.exp(-inc)).astype(BF16)
            qd_ref[pl.ds(s0, n), :] = qd
            ks_ref[pl.ds(s0, n), :] = ks
            kt_ref[pl.ds(s0, n), :] = kt.astype(BF16)
            dec_ref[pl.ds(s0, n), :] = dec
            scores.append((reverse, _bdot_nt(qd, ks)))
        pending.append((scores, vb, o_ref))
    for scores, vb, o_ref in pending:
        o = jnp.zeros((n, HEAD_DIM), F32)
        for reverse, s in scores:
            keep = same & ((col >= row) if reverse else (col <= row))
            o = o + jnp.dot(jnp.where(keep, s, 0.0).astype(BF16), vb, preferred_element_type=F32)
        o_ref[pl.ds(s0, n), :] = o


def _decay_attn_state_pass(c, seq_len, ref_sets):
    n_chunks = seq_len // c
    n_ctx_chunks = CTX_LEN // c

    steps = 4 if n_chunks % 4 == 0 else 3 if n_chunks % 3 == 0 else 1

    def body(jj, carry):
        chains = []
        for hh, refs in enumerate(ref_sets):
            (qdf_ref, _, ktf_ref, decf_ref, qdb_ref, _, ktb_ref, decb_ref, vb_ref, o_ref) = refs
            chains.append((qdf_ref, ktf_ref, decf_ref, vb_ref, o_ref, False))
            chains.append((qdb_ref, ktb_ref, decb_ref, vb_ref, o_ref, True))
        starts = []
        for t in range(steps):
            j = jj * steps + t
            jb = jnp.where(j < n_ctx_chunks, n_ctx_chunks - 1 - j, n_chunks - 1 - (j - n_ctx_chunks))
            starts.append((pl.multiple_of(j * c, c), pl.multiple_of(jb * c, c)))
        upds = [[_bdot_tn(vb_ref[pl.ds(starts[t][rev], c), :], kt_ref[pl.ds(starts[t][rev], c), :])
                 for t in range(steps)]
                for (_, kt_ref, _, vb_ref, _, rev) in chains]
        states = list(carry)
        for t in range(steps):
            for ci, (qd_ref, _, dec_ref, _, o_ref, rev) in enumerate(chains):
                r0 = starts[t][rev]
                o_ref[pl.ds(r0, c), :] += _bdot_nt(qd_ref[pl.ds(r0, c), :], states[ci])
                states[ci] = states[ci] * dec_ref[pl.ds(r0, 1), :] + upds[ci][t]
        return tuple(states)

    zero = jnp.zeros((HEAD_DIM, HEAD_DIM), F32)
    lax.fori_loop(0, n_chunks // steps, body, (zero,) * (2 * len(ref_sets)))


DECAY_SCRATCH_PER_HEAD = 10
HEADS_PER_STEP = 2


def _decay_scratch(l):
    one_dir = [pltpu.VMEM((l, LANES), BF16)] * 3 + [pltpu.VMEM((l, LANES), F32)]
    one_head = one_dir + one_dir + [pltpu.VMEM((l, LANES), BF16), pltpu.VMEM((l, LANES), F32)]
    assert len(one_head) == DECAY_SCRATCH_PER_HEAD
    return one_head * HEADS_PER_STEP


def _head_sets(scratch):
    n = DECAY_SCRATCH_PER_HEAD
    return [tuple(scratch[hh * n:(hh + 1) * n]) for hh in range(HEADS_PER_STEP)]


def _pair_slab_specs(bsz, l, split):
    return [pl.BlockSpec((1, l, HEAD_DIM),
                         lambda bb, hp, hh=hh: ((split * N_HEADS + hp * HEADS_PER_STEP + hh) * bsz + bb, 0, 0))
            for hh in range(HEADS_PER_STEP)]


def _retention_kernel(*refs, seq_len):
    nh = HEADS_PER_STEP
    q_refs, k_refs, v_refs, g_refs = (refs[i * nh:(i + 1) * nh] for i in range(4))
    cos_ref, sin_ref, lg_ref, gn_ref, y_ref = refs[4 * nh:4 * nh + 5]
    sets = _head_sets(refs[4 * nh + 5:])
    l = seq_len
    n = ATTN_BLOCK
    lane = lax.broadcasted_iota(jnp.int32, (n, HEAD_DIM), 1)
    first = (lane % (HEAD_DIM // 2)) < (HEAD_DIM // 4)

    def rope(z, cos, sin):
        swapped = jnp.where(first, pltpu.roll(z, HEAD_DIM - HEAD_DIM // 4, 1), pltpu.roll(z, HEAD_DIM // 4, 1))
        return z * cos + swapped * sin

    def block_body(bi, carry):
        s0 = pl.multiple_of(bi * n, n)
        cos = cos_ref[pl.ds(s0, n), :]
        sin = sin_ref[pl.ds(s0, n), :]
        heads = []
        for hh in range(nh):
            lg = lg_ref[hh]
            q = rope(q_refs[hh][0, pl.ds(s0, n), :], cos, sin)
            k = rope(k_refs[hh][0, pl.ds(s0, n), :], cos, sin) * (HEAD_DIM ** -0.5)
            heads.append((q, k, k, v_refs[hh][0, pl.ds(s0, n), :], lg[0:1], lg[1:2]))
        _decay_attn_blocks(s0, heads, RET_CHUNK, sets)
        return carry

    lax.fori_loop(0, l // n, block_body, 0)
    _decay_attn_state_pass(RET_CHUNK, l, sets)

    def out_body(bi, carry):
        s0 = pl.multiple_of(bi * n, n)
        for hh in range(nh):
            cols = slice(hh * HEAD_DIM, (hh + 1) * HEAD_DIM)
            o = sets[hh][-1][pl.ds(s0, n), :]
            o = o - jnp.mean(o, axis=-1, keepdims=True)
            o = o * lax.rsqrt(jnp.mean(o * o, axis=-1, keepdims=True) + EPS) * gn_ref[:, cols]
            y_ref[0, pl.ds(s0, n), cols] = (_silu(g_refs[hh][0, pl.ds(s0, n), :]) * o).astype(BF16)
        return carry

    lax.fori_loop(0, l // n, out_body, 0)


def _rope_tables(l):
    n_lat = l - CTX_LEN
    quarter = HEAD_DIM // 4
    t = jnp.arange(n_lat)
    inv_freq = ROPE_BASE ** (-jnp.arange(quarter, dtype=F32) / quarter)
    ang_r = (t // GRID_W).astype(F32)[:, None] * inv_freq[None, :]
    ang_c = (t % GRID_W).astype(F32)[:, None] * inv_freq[None, :]
    cos = jnp.concatenate([jnp.cos(ang_r), jnp.cos(ang_r), jnp.cos(ang_c), jnp.cos(ang_c)], axis=-1)
    sin = jnp.concatenate([-jnp.sin(ang_r), jnp.sin(ang_r), -jnp.sin(ang_c), jnp.sin(ang_c)], axis=-1)
    cos = jnp.concatenate([jnp.ones((CTX_LEN, HEAD_DIM), F32), cos], axis=0)
    sin = jnp.concatenate([jnp.zeros((CTX_LEN, HEAD_DIM), F32), sin], axis=0)
    return cos, sin


def _retention(p3, gn_g, cos, sin):
    l = p3.shape[1]
    b = p3.shape[0] // (N_IN_SPLITS * N_HEADS)
    gw = N_HEADS * HEAD_DIM
    log_gamma = jnp.log(1.0 - 2.0 ** (-5.0 - jnp.arange(N_HEADS, dtype=F32)))
    lg = jnp.stack([log_gamma, log_gamma[::-1]], axis=1)
    lg = jnp.broadcast_to(lg[:, :, None], (N_HEADS, 2, HEAD_DIM))
    nh = HEADS_PER_STEP
    full = pl.BlockSpec((l, HEAD_DIM), lambda bb, hp: (0, 0))
    slabs = [s for split in (5, 6, 7, 8) for s in _pair_slab_specs(b, l, split)]
    return pl.pallas_call(
        functools.partial(_retention_kernel, seq_len=l),
        out_shape=jax.ShapeDtypeStruct((b, l, gw), BF16),
        grid=(b, N_HEADS // nh),
        in_specs=slabs + [full, full,
                          pl.BlockSpec((nh, 2, HEAD_DIM), lambda bb, hp: (hp, 0, 0)),
                          pl.BlockSpec((1, nh * HEAD_DIM), lambda bb, hp: (0, hp))],
        out_specs=pl.BlockSpec((1, l, nh * HEAD_DIM), lambda bb, hp: (bb, 0, hp)),
        scratch_shapes=_decay_scratch(l),
        compiler_params=_cparams(("arbitrary", "arbitrary"), VMEM_LIMIT),
        name="retention",
    )(*([p3] * (4 * nh)), cos, sin, lg, gn_g.reshape(1, gw))


def _hgrn_kernel(*refs, seq_len, layer, depth):
    nh = HEADS_PER_STEP
    ff_refs, fb_refs, i_refs, q_refs, g_refs = (refs[i * nh:(i + 1) * nh] for i in range(5))
    lbl_ref, on_ref, y_ref = refs[5 * nh:5 * nh + 3]
    sets = _head_sets(refs[5 * nh + 3:])
    l = seq_len
    n = ATTN_BLOCK
    logits = lbl_ref[...]
    log_lb, log1m_lb = [], []
    for dd in range(2):
        xs = [logits[ll * 2 + dd] for ll in range(depth)]
        mx = functools.reduce(jnp.maximum, xs)
        es = [jnp.exp(x - mx) for x in xs]
        tot = functools.reduce(lambda a, b_: a + b_, es)
        lb = jnp.zeros_like(tot)
        for ll in range(1, layer + 1):
            lb = lb + es[ll] / tot
        log_lb.append(jnp.log(lb))
        log1m_lb.append(jnp.log1p(-lb))

    def gate(fz, dd, cols):
        log_sig = jnp.minimum(fz, 0.0) - jnp.log(1.0 + jnp.exp(-jnp.abs(fz)))
        a = jnp.broadcast_to(log_lb[dd][:, cols], fz.shape)
        bv = log1m_lb[dd][:, cols] + log_sig
        log_f = jnp.maximum(a, bv) + jnp.log(1.0 + jnp.exp(-jnp.abs(a - bv)))
        return log_f, 1.0 - jnp.exp(log_f)

    def block_body(bi, carry):
        s0 = pl.multiple_of(bi * n, n)
        heads = []
        for hh in range(nh):
            cols = slice(hh * HEAD_DIM, (hh + 1) * HEAD_DIM)
            lf_f, k_f = gate(ff_refs[hh][0, pl.ds(s0, n), :], 0, cols)
            lf_b, k_b = gate(fb_refs[hh][0, pl.ds(s0, n), :], 1, cols)
            heads.append((q_refs[hh][0, pl.ds(s0, n), :], k_f, k_b, i_refs[hh][0, pl.ds(s0, n), :], lf_f, lf_b))
        _decay_attn_blocks(s0, heads, HGRN_CHUNK, sets)
        return carry

    lax.fori_loop(0, l // n, block_body, 0)
    _decay_attn_state_pass(HGRN_CHUNK, l, sets)

    def out_body(bi, carry):
        s0 = pl.multiple_of(bi * n, n)
        for hh in range(nh):
            cols = slice(hh * HEAD_DIM, (hh + 1) * HEAD_DIM)
            o = sets[hh][-1][pl.ds(s0, n), :]
            o = o * lax.rsqrt(jnp.mean(o * o, axis=-1, keepdims=True) + EPS) * on_ref[:, cols]
            y_ref[0, pl.ds(s0, n), cols] = (_silu(g_refs[hh][0, pl.ds(s0, n), :]) * o).astype(BF16)
        return carry

    lax.fori_loop(0, l // n, out_body, 0)


def _hgrn2(p3, lb_logits, on_g, layer):
    l = p3.shape[1]
    b = p3.shape[0] // (N_IN_SPLITS * N_HEADS)
    gw = N_HEADS * HEAD_DIM
    depth = lb_logits.shape[0]
    nh = HEADS_PER_STEP
    slabs = [s for split in (9, 10, 11, 12, 13) for s in _pair_slab_specs(b, l, split)]
    return pl.pallas_call(
        functools.partial(_hgrn_kernel, seq_len=l, layer=layer, depth=depth),
        out_shape=jax.ShapeDtypeStruct((b, l, gw), BF16),
        grid=(b, N_HEADS // nh),
        in_specs=slabs + [pl.BlockSpec((depth * 2, 1, nh * HEAD_DIM), lambda bb, hp: (0, 0, hp)),
                          pl.BlockSpec((1, nh * HEAD_DIM), lambda bb, hp: (0, hp))],
        out_specs=pl.BlockSpec((1, l, nh * HEAD_DIM), lambda bb, hp: (bb, 0, hp)),
        scratch_shapes=_decay_scratch(l),
        compiler_params=_cparams(("arbitrary", "arbitrary"), VMEM_LIMIT),
        name="hgrn2",
    )(*([p3] * (5 * nh)), lb_logits.astype(F32).reshape(depth * 2, 1, gw), on_g.reshape(1, gw))


def _topk_rows(x, k):
    r_n = x.shape[0]
    idx = lax.broadcasted_iota(jnp.int32, x.shape, 0).astype(F32)
    rank = jnp.full(x.shape, float(k), F32)
    vals = []
    for r in range(k):
        m = jnp.max(x, axis=0, keepdims=True)
        first = jnp.min(jnp.where(x == m, idx, float(r_n)), axis=0, keepdims=True)
        sel = idx == first
        rank = jnp.where(sel, float(r), rank)
        x = jnp.where(sel, -jnp.inf, x)
        vals.append(m)
    return rank, jnp.concatenate(vals, axis=0)


def _bf16_pair_words(x):
    bits = pltpu.bitcast(x.astype(BF16).astype(F32), jnp.uint32)
    return bits | (bits >> 16)


def _peer_route_kernel(zt_ref, wq_ref, sk_ref, rank2_ref, lim1_ref, e1_ref, e2_ref, *, tb):
    k = PEER_TOPK
    n_kc, _, kc = wq_ref.shape
    q = jnp.dot(wq_ref[0], zt_ref[0, 0:kc, :], preferred_element_type=F32)
    for ci in range(1, n_kc):
        q += jnp.dot(wq_ref[ci], zt_ref[0, ci * kc:(ci + 1) * kc, :], preferred_element_type=F32)
    half = q.shape[0] // 2
    for c0 in range(0, tb, LANES):
        s1 = _bdot(sk_ref[0, 0], q[:half, c0:c0 + LANES])
        s2 = _bdot(sk_ref[0, 1], q[half:, c0:c0 + LANES])
        rank1, top1 = _topk_rows(s1, k)
        rank2, top2 = _topk_rows(s2, k)
        e1 = jnp.exp(s1 - top1[0:1])
        e2 = jnp.exp(s2 - top2[0:1])
        e1t = jnp.exp(top1 - top1[0:1])
        e2t = jnp.exp(top2 - top2[0:1])
        n_multi = sum(1 for r in range(k) if k // (r + 1) > 1)
        assert k - n_multi == SUBLANES
        pieces, epieces, widths = [], [], []
        for r in range(n_multi):
            n_c = k // (r + 1)
            n_pad = -(-n_c // SUBLANES) * SUBLANES
            valid = lax.broadcasted_iota(jnp.int32, (n_pad, LANES), 0) < n_c
            pieces.append(jnp.where(valid, top1[r:r + 1] + top2[0:n_pad], -jnp.inf))
            epieces.append(e1t[r:r + 1] * e2t[0:n_pad])
            widths.append(n_pad)
        pieces.append(top1[n_multi:k] + top2[0:1])
        epieces.append(e1t[n_multi:k] * e2t[0:1])
        cand = jnp.concatenate(pieces, axis=0)
        ecand = jnp.concatenate(epieces, axis=0)
        crank, _ = _topk_rows(cand, k)
        sel = crank < k
        z = jnp.sum(jnp.where(sel, ecand, 0.0), axis=0, keepdims=True)
        lim1 = jnp.zeros(rank1.shape, F32)
        self32 = sel.astype(F32)
        off = 0
        for r in range(n_multi):
            cnt = jnp.sum(self32[off:off + widths[r]], axis=0, keepdims=True)
            lim1 = jnp.where(rank1 == r, cnt, lim1)
            off += widths[r]
        for r in range(n_multi, k):
            lim1 = jnp.where(rank1 == r, self32[off + r - n_multi:off + r - n_multi + 1], lim1)
        rank2_ref[0, 0, :, c0:c0 + LANES] = rank2.astype(BF16)
        lim1_ref[0, 0, :, c0:c0 + LANES] = _bf16_pair_words(lim1)
        e1_ref[0, 0, :, c0:c0 + LANES] = _bf16_pair_words(e1)
        e2_ref[0, 0, :, c0:c0 + LANES] = (e2 / z).astype(BF16)


def _peer_route(zt, wq_t, sub_keys):
    nb, d, tb = zt.shape
    t = nb * tb
    n_kc, _, kc = wq_t.shape
    qd = wq_t.shape[1] // PEER_HEADS
    tabs = [jax.ShapeDtypeStruct((nb, PEER_HEADS, PEER_NKEYS, tb), dt)
            for dt in (BF16, jnp.uint32, jnp.uint32, BF16)]
    tab_spec = pl.BlockSpec((1, 1, PEER_NKEYS, tb), lambda i, h: (i, h, 0, 0))
    return pl.pallas_call(
        functools.partial(_peer_route_kernel, tb=tb),
        out_shape=tabs,
        grid=(t // tb, PEER_HEADS),
        in_specs=[pl.BlockSpec((1, d, tb), lambda i, h: (i, 0, 0)),
                  pl.BlockSpec((n_kc, qd, kc), lambda i, h: (0, h, 0)),
                  pl.BlockSpec((1, 2, PEER_NKEYS, qd // 2), lambda i, h: (h, 0, 0, 0))],
        out_specs=[tab_spec] * 4,
        compiler_params=_cparams(("arbitrary", "arbitrary"), VMEM_LIMIT),
        name="peer_route",
    )(zt, wq_t, sub_keys)


def _peer_expert_kernel(zt_ref, u_ref, vt_ref, rank2_ref, lim1_ref, e1_ref, e2_ref, o_ref, w_ref, *, ek):
    e = pl.program_id(1)
    tb = zt_ref.shape[2]
    n_i1 = ek // PEER_NKEYS

    @pl.when(e == 0)
    def _():
        o_ref[...] = jnp.zeros(o_ref.shape, F32)

    half = (PEER_NKEYS // 2, PEER_LANE_CHUNK)

    def gated(j, act):
        i1 = e * n_i1 + j
        for c0 in range(0, tb, PEER_LANE_CHUNK):
            cs = slice(c0, c0 + PEER_LANE_CHUNK)
            g = jnp.zeros((PEER_NKEYS, PEER_LANE_CHUNK), BF16)
            for h in range(PEER_HEADS):
                lim = pltpu.bitcast(jnp.broadcast_to(lim1_ref[0, h, pl.ds(i1, 1), cs], half), BF16)
                w1 = pltpu.bitcast(jnp.broadcast_to(e1_ref[0, h, pl.ds(i1, 1), cs], half), BF16)
                g = g + jnp.where(rank2_ref[0, h, :, cs] < lim, e2_ref[0, h, :, cs], jnp.zeros_like(g)) * w1
            w_ref[j * PEER_NKEYS:(j + 1) * PEER_NKEYS, cs] = (g.astype(F32) * _gelu(act[:, cs])).astype(BF16)

    per_batch = PEER_UP_BATCH // PEER_NKEYS
    per_group = PEER_DOWN_GROUP // PEER_NKEYS
    for b0 in range(0, n_i1, per_batch):
        acts = {j: jnp.dot(u_ref[j * PEER_NKEYS:(j + 1) * PEER_NKEYS, :], zt_ref[0], preferred_element_type=F32)
                for j in range(b0, b0 + per_batch)}
        for g0 in range(b0, b0 + per_batch, per_group):
            for j in range(g0, g0 + per_group):
                gated(j, acts[j])
            ks = slice(g0 * PEER_NKEYS, (g0 + per_group) * PEER_NKEYS)
            o_ref[0] += jnp.dot(vt_ref[0, :, ks], w_ref[ks, :], preferred_element_type=F32)


def _peer_experts(zt, u, vt, tabs, ek):
    nb, d, tb = zt.shape
    ne = u.shape[0]
    once = pl.Buffered(1)
    tab_spec = pl.BlockSpec((1, PEER_HEADS, PEER_NKEYS, tb), lambda i, e: (i, 0, 0, 0), pipeline_mode=once)
    return pl.pallas_call(
        functools.partial(_peer_expert_kernel, ek=ek),
        out_shape=jax.ShapeDtypeStruct((nb, d, tb), F32),
        grid=(nb, ne // ek),
        in_specs=[pl.BlockSpec((1, d, tb), lambda i, e: (i, 0, 0), pipeline_mode=once),
                  pl.BlockSpec((ek, d), lambda i, e: (e, 0)),
                  pl.BlockSpec((1, d, ek), lambda i, e: (e, 0, 0)),
                  tab_spec, tab_spec, tab_spec, tab_spec],
        out_specs=pl.BlockSpec((1, d, tb), lambda i, e: (i, 0, 0)),
        scratch_shapes=[pltpu.VMEM((ek, tb), BF16)],
        compiler_params=_cparams(("arbitrary", "arbitrary"), PEER_VMEM_LIMIT),
        name="peer_experts",
    )(zt, u, vt, *tabs)


def kernel(x, c, ctx, c_ctx, norm1_g, norm2_g, ada_w, ada_b, w_in, w_out, lru_conv_w, lru_conv_b, lru_wa, lru_ba,
           lru_wx, lru_bx, lru_lam, na_qn_g, na_kn_g, na_rpb, ret_gn_g, hgrn_lb_logits, hgrn_on_g, peer_wq,
           peer_subkeys, peer_u, peer_v):
    bsz, n_lat, d = x.shape
    depth = w_in.shape[0]
    l = CTX_LEN + n_lat
    gw = d // N_GROUPS
    assert bsz < MOD_ROWS and n_lat % ROW_TILE == 0 and CTX_LEN % ROW_TILE == 0
    ctx_row = bsz

    c16 = jnp.zeros((MOD_ROWS, d), F32).at[:bsz].set(c).at[ctx_row].set(c_ctx)
    cos, sin = _rope_tables(l)
    rows = n_lat // GRID_W

    h3 = jnp.concatenate([ctx, x], axis=1)
    tiles_full = l // ROW_TILE
    tiles_lat = n_lat // ROW_TILE
    ctx_tiles = CTX_LEN // ROW_TILE

    def row_full(i):
        return jnp.where(i % tiles_full < ctx_tiles, ctx_row, i // tiles_full)

    def row_lat(i):
        return i // tiles_lat

    h = h3.reshape(bsz * l, d)
    pending = None
    for layer in range(depth):
        last = layer == depth - 1
        mod3 = _ada_modulation(c16, ada_w, ada_b, layer).reshape(MOD_ROWS * N_MOD, 1, d)
        if pending is None:
            (u,) = _resid_norm(h, mod3, row_full, norm_g=norm1_g[layer], shift_k=0, scale_k=1)
        else:
            ft, mod3_prev = pending
            h, u = _resid_norm(h, mod3, row_full, ft=ft, gate_mod3=mod3_prev, gate_k=5,
                               norm_g=norm1_g[layer], shift_k=0, scale_k=1)
        p = _matmul(u, w_in, layer, 1024, 1024)
        p3 = p.reshape(N_IN_SPLITS * N_HEADS * bsz, l, HEAD_DIM)
        y_a = _rglru(p3, lru_conv_w[layer], lru_conv_b[layer], lru_wa[layer], lru_ba[layer],
                     lru_wx[layer], lru_bx[layer], lru_lam[layer])
        y_b = _natten(p3, na_qn_g[layer], na_kn_g[layer], _na_bias_table(na_rpb[layer], rows))
        y_c = _retention(p3, ret_gn_g[layer], cos, sin)
        y_d = _hgrn2(p3, hgrn_lb_logits, hgrn_on_g[layer], layer)
        if last:
            hm3 = _out_proj((y_a, y_b, y_c, y_d), w_out, layer, h.reshape(bsz, l, d), mod3, row_off=ctx_tiles,
                            row_fn=lambda bb, i: bb, gate_k=2)
            row_fn = row_lat
        else:
            hm3 = _out_proj((y_a, y_b, y_c, y_d), w_out, layer, h.reshape(bsz, l, d), mod3, row_off=0,
                            row_fn=lambda bb, i: jnp.where(i < ctx_tiles, ctx_row, bb), gate_k=2)
            row_fn = row_full
        h = hm3.reshape(-1, d)
        (zt,) = _resid_norm(h, mod3, row_fn, norm_g=norm2_g[layer], shift_k=3, scale_k=4, transpose_out=True)
        wq_t = _cast_weight(peer_wq, layer, transpose=True, rows_per_step=512)
        u_b = _cast_weight(peer_u, layer, transpose=False, rows_per_step=512)
        v_t = _cast_weight(peer_v, layer, transpose=True, rows_per_step=PEER_EXPERT_BLOCK)
        tabs = _peer_route(zt, wq_t, peer_subkeys[layer])
        ft = _peer_experts(zt, u_b, v_t, tabs, PEER_EXPERT_BLOCK)
        if last:
            (h,) = _resid_norm(h, mod3, row_fn, ft=ft, gate_k=5)
        else:
            pending = (ft, mod3)
    return h.reshape(bsz, n_lat, d)
```

```python
import functools

import jax
import jax.numpy as jnp
from jax import lax
from jax.experimental import pallas as pl
from jax.experimental.pallas import tpu as pltpu

F32 = jnp.float32
BF16 = jnp.bfloat16

N_GROUPS = 4
N_HEADS = 8
HEAD_DIM = 128
N_IN_SPLITS = 14
N_MOD = 6
CTX_LEN = 256
GRID_W = 64
LRU_CONV_W = 4
LRU_C = 8.0
NA_WIN_ROWS = 8
NA_WIN_COLS = 16
RET_CHUNK = 128
HGRN_CHUNK = 32
PEER_HEADS = 8
PEER_NKEYS = 128
PEER_TOPK = 16
ROPE_BASE = 10000.0
EPS = 1e-6
NEG_INF = -1e30

SUBLANES = 8
LANES = 128
ROW_TILE = 256
ATTN_BLOCK = 128
MOD_ROWS = 16
PEER_TOKEN_BLOCK = 512
PEER_EXPERT_BLOCK = 1024
PEER_LANE_CHUNK = 256
PEER_DOWN_GROUP = 512
PEER_UP_BATCH = 1024
VMEM_LIMIT = 56 * 1024 * 1024
PEER_VMEM_LIMIT = 62 * 1024 * 1024


def _cparams(sem, vmem=None):
    return pltpu.CompilerParams(dimension_semantics=sem, vmem_limit_bytes=vmem)


def _bdot(a, b):
    return jnp.dot(a.astype(BF16), b.astype(BF16), preferred_element_type=F32)


def _bdot_nt(a, b):
    return lax.dot_general(a.astype(BF16), b.astype(BF16), (((1,), (1,)), ((), ())),
                           preferred_element_type=F32)


def _bdot_tn(a, b):
    return lax.dot_general(a.astype(BF16), b.astype(BF16), (((0,), (0,)), ((), ())),
                           preferred_element_type=F32)


def _gelu(x):
    c = 0.7978845608028654 * LOG2_E
    return x / (1.0 + jnp.exp2(x * (-2.0 * c - (2.0 * c * 0.044715) * (x * x))))


LOG2_E = 1.4426950408889634


def _sigmoid(x):
    return 1.0 / (1.0 + jnp.exp2(x * (-LOG2_E)))


def _silu(x):
    return x * _sigmoid(x)


def _ada_kernel(c_ref, w_ref, b_ref, o_ref):
    x = c_ref[...]
    x = _silu(x)
    hi = x.astype(BF16)
    lo = (x - hi.astype(F32)).astype(BF16)
    w = w_ref[0]
    whi = w.astype(BF16)
    wlo = (w - whi.astype(F32)).astype(BF16)
    acc = jnp.dot(hi, whi, preferred_element_type=F32)
    acc += jnp.dot(hi, wlo, preferred_element_type=F32)
    acc += jnp.dot(lo, whi, preferred_element_type=F32)
    o_ref[...] = acc + b_ref[0]


def _ada_modulation(c16, w_all, b_all, layer):
    depth, d, n = w_all.shape
    tn = 512
    return pl.pallas_call(
        _ada_kernel,
        out_shape=jax.ShapeDtypeStruct((MOD_ROWS, n), F32),
        grid=(n // tn,),
        in_specs=[pl.BlockSpec((MOD_ROWS, d), lambda j: (0, 0)),
                  pl.BlockSpec((1, d, tn), lambda j: (layer, 0, j)),
                  pl.BlockSpec((1, 1, tn), lambda j: (layer, 0, j))],
        out_specs=pl.BlockSpec((MOD_ROWS, tn), lambda j: (0, j)),
        compiler_params=_cparams(("arbitrary",), VMEM_LIMIT),
        name="ada_modulation",
    )(c16, w_all, b_all.reshape(depth, 1, n))


def _resid_norm_kernel(*refs, has_f, has_norm, transpose_out):
    it = iter(refs)
    h_ref = next(it)
    if has_f:
        ft_ref = next(it)
        gate_ref = next(it)
    if has_norm:
        g_ref = next(it)
        sh_ref = next(it)
        sc_ref = next(it)
    if has_f:
        hn_ref = next(it)
    if has_norm:
        u_ref = next(it)
    x = h_ref[...]
    if has_f:
        x = x + gate_ref[0] * ft_ref[0].T
        hn_ref[...] = x
    if has_norm:
        ms = jnp.mean(x * x, axis=-1, keepdims=True)
        y = x * lax.rsqrt(ms + EPS) * g_ref[...]
        y = y * (1.0 + sc_ref[0]) + sh_ref[0]
        if transpose_out:
            u_ref[0] = y.T.astype(BF16)
        else:
            u_ref[...] = y.astype(BF16)


def _resid_norm(h, mod3, row_fn, *, ft=None, gate_mod3=None, gate_k=None, norm_g=None, shift_k=None,
                scale_k=None, transpose_out=False):
    t, d = h.shape
    nt = t // ROW_TILE
    tiles_per_chunk = PEER_TOKEN_BLOCK // ROW_TILE
    chunk_spec = pl.BlockSpec((1, d, ROW_TILE), lambda i: (i // tiles_per_chunk, 0, i % tiles_per_chunk))
    has_f = ft is not None
    has_norm = norm_g is not None
    in_specs = [pl.BlockSpec((ROW_TILE, d), lambda i: (i, 0))]
    args = [h]
    if has_f:
        in_specs += [chunk_spec,
                     pl.BlockSpec((1, 1, d), lambda i: (row_fn(i) * N_MOD + gate_k, 0, 0))]
        args += [ft, mod3 if gate_mod3 is None else gate_mod3]
    if has_norm:
        in_specs += [pl.BlockSpec((1, d), lambda i: (0, 0)),
                     pl.BlockSpec((1, 1, d), lambda i: (row_fn(i) * N_MOD + shift_k, 0, 0)),
                     pl.BlockSpec((1, 1, d), lambda i: (row_fn(i) * N_MOD + scale_k, 0, 0))]
        args += [norm_g.reshape(1, d), mod3, mod3]
    out_shape, out_specs = [], []
    if has_f:
        out_shape.append(jax.ShapeDtypeStruct((t, d), F32))
        out_specs.append(pl.BlockSpec((ROW_TILE, d), lambda i: (i, 0)))
    if has_norm:
        if transpose_out:
            out_shape.append(jax.ShapeDtypeStruct((t // PEER_TOKEN_BLOCK, d, PEER_TOKEN_BLOCK), BF16))
            out_specs.append(chunk_spec)
        else:
            out_shape.append(jax.ShapeDtypeStruct((t, d), BF16))
            out_specs.append(pl.BlockSpec((ROW_TILE, d), lambda i: (i, 0)))
    outs = pl.pallas_call(
        functools.partial(_resid_norm_kernel, has_f=has_f, has_norm=has_norm, transpose_out=transpose_out),
        out_shape=out_shape, grid=(nt,), in_specs=in_specs, out_specs=out_specs,
        compiler_params=_cparams(("arbitrary",), VMEM_LIMIT),
        name="resid_norm",
    )(*args)
    return outs


def _mm_kernel(x_ref, w_ref, o_ref, wb_ref):
    @pl.when(pl.program_id(1) == 0)
    def _():
        wb_ref[...] = w_ref[0].astype(BF16)

    r = jnp.dot(x_ref[...], wb_ref[...], preferred_element_type=F32)
    for hh in range(o_ref.shape[0]):
        o_ref[hh] = r[:, hh * HEAD_DIM:(hh + 1) * HEAD_DIM]


def _matmul(x, w_all, layer, tm, tn):
    m, k = x.shape
    n = w_all.shape[2]
    return pl.pallas_call(
        _mm_kernel,
        out_shape=jax.ShapeDtypeStruct((n // HEAD_DIM, m, HEAD_DIM), F32),
        grid=(n // tn, m // tm),
        in_specs=[pl.BlockSpec((tm, k), lambda j, i: (i, 0)),
                  pl.BlockSpec((1, k, tn), lambda j, i: (layer, 0, j), pipeline_mode=pl.Buffered(1))],
        out_specs=pl.BlockSpec((tn // HEAD_DIM, tm, HEAD_DIM), lambda j, i: (j, i, 0)),
        scratch_shapes=[pltpu.VMEM((k, tn), BF16)],
        compiler_params=_cparams(("arbitrary", "arbitrary"), VMEM_LIMIT),
        name="in_proj",
    )(x, w_all)


def _cast_kernel(w_ref, o_ref, *, transpose):
    w = w_ref[0]
    if transpose:
        o_ref[0] = w.T.astype(BF16)
    else:
        o_ref[...] = w.astype(BF16)


def _cast_weight(w_all, layer, *, transpose, rows_per_step):
    _, r, c = w_all.shape
    tr = rows_per_step
    if transpose:
        out_shape, out_spec = (r // tr, c, tr), pl.BlockSpec((1, c, tr), lambda i: (i, 0, 0))
    else:
        out_shape, out_spec = (r, c), pl.BlockSpec((tr, c), lambda i: (i, 0))
    return pl.pallas_call(
        functools.partial(_cast_kernel, transpose=transpose),
        out_shape=jax.ShapeDtypeStruct(out_shape, BF16),
        grid=(r // tr,),
        in_specs=[pl.BlockSpec((1, tr, c), lambda i: (layer, i, 0))],
        out_specs=out_spec,
        compiler_params=_cparams(("arbitrary",), VMEM_LIMIT),
        name="cast_weight",
    )(w_all)


def _head_slab_spec(bsz, l, split):
    return pl.BlockSpec((1, l, HEAD_DIM), lambda bb, h: ((split * N_HEADS + h) * bsz + bb, 0, 0))


def _wout_kernel(ya_ref, yb_ref, yc_ref, yd_ref, w_ref, h_ref, gate_ref, o_ref, wb_ref):
    @pl.when((pl.program_id(1) == 0) & (pl.program_id(2) == 0))
    def _():
        wb_ref[...] = w_ref[0].astype(BF16)

    gw = ya_ref.shape[2]
    acc = jnp.dot(ya_ref[0], wb_ref[0 * gw:1 * gw, :], preferred_element_type=F32)
    acc += jnp.dot(yb_ref[0], wb_ref[1 * gw:2 * gw, :], preferred_element_type=F32)
    acc += jnp.dot(yc_ref[0], wb_ref[2 * gw:3 * gw, :], preferred_element_type=F32)
    acc += jnp.dot(yd_ref[0], wb_ref[3 * gw:4 * gw, :], preferred_element_type=F32)
    o_ref[0] = h_ref[0] + gate_ref[0] * acc


def _out_proj(ys, w_all, layer, h3, mod3, *, row_off, row_fn, gate_k):
    b, l, gw = ys[0].shape
    d = w_all.shape[2]
    nrt = l // ROW_TILE - row_off
    tn = 1024
    y_spec = pl.BlockSpec((1, ROW_TILE, gw), lambda j, bb, i: (bb, i + row_off, 0))
    return pl.pallas_call(
        _wout_kernel,
        out_shape=jax.ShapeDtypeStruct((b, nrt * ROW_TILE, d), F32),
        grid=(d // tn, b, nrt),
        in_specs=[y_spec, y_spec, y_spec, y_spec,
                  pl.BlockSpec((1, N_GROUPS * gw, tn), lambda j, bb, i: (layer, 0, j),
                               pipeline_mode=pl.Buffered(1)),
                  pl.BlockSpec((1, ROW_TILE, tn), lambda j, bb, i: (bb, i + row_off, j)),
                  pl.BlockSpec((1, 1, tn), lambda j, bb, i: (row_fn(bb, i) * N_MOD + gate_k, 0, j))],
        out_specs=pl.BlockSpec((1, ROW_TILE, tn), lambda j, bb, i: (bb, i, j)),
        scratch_shapes=[pltpu.VMEM((N_GROUPS * gw, tn), BF16)],
        compiler_params=_cparams(("arbitrary", "arbitrary", "arbitrary"), VMEM_LIMIT),
        name="out_proj",
    )(*ys, w_all, h3, mod3)


def _group_cumsum(x, rowmod, c, reverse):
    n = x.shape[0]
    k = 1
    while k < c:
        if reverse:
            x = x + jnp.where(rowmod < c - k, pltpu.roll(x, n - k, 0), 0.0)
        else:
            x = x + jnp.where(rowmod >= k, pltpu.roll(x, k, 0), 0.0)
        k *= 2
    return x


def _group_linear_scan(a, u, rowmod, c, reverse):
    n, w = a.shape

    def shifted(x, s):
        return pltpu.roll(x.reshape(n // c, c, w), s, 1).reshape(n, w)

    k = 1
    while k < c:
        if reverse:
            ok = rowmod < c - k
            a_s = shifted(a, c - k)
            u_s = shifted(u, c - k)
        else:
            ok = rowmod >= k
            a_s = shifted(a, k)
            u_s = shifted(u, k)
        u = jnp.where(ok, a * u_s + u, u)
        a = jnp.where(ok, a * a_s, a)
        k *= 2
    return a, u


def _rglru_kernel(x_ref, g_ref, cw_ref, cb_ref, wa_ref, ba_ref, wx_ref, bx_ref, lam_ref, y_ref,
                  xp_ref, af_ref, uf_ref, ab_ref, ub_ref, *, seq_len):
    l = seq_len
    pad = SUBLANES
    nblk = l // ROW_TILE
    zeros_pad = jnp.zeros((pad, LANES), F32)
    xp_ref[0:pad, :] = zeros_pad
    xp_ref[pad + l:pad + l + pad, :] = zeros_pad
    xp_ref[pad:pad + l, :] = x_ref[0]

    row = lax.broadcasted_iota(jnp.int32, (ROW_TILE, LANES), 0)
    rowmod = row % SUBLANES
    cw = cw_ref[...]
    cb = cb_ref[...]
    lam = lam_ref[...]
    log2_decay = [(jnp.maximum(-lam[dd], 0.0) + jnp.log1p(jnp.exp(-jnp.abs(lam[dd])))) * (-LRU_C * LOG2_E)
                  for dd in range(2)]

    for blk in range(nblk):
        s0 = blk * ROW_TILE
        taps = []
        for tap in range(LRU_CONV_W):
            off = tap - LRU_CONV_W // 2
            v = xp_ref[pad + s0 + off:pad + s0 + off + ROW_TILE, :]
            if blk == 0 and off > 0:
                v = jnp.where(row + off < CTX_LEN, v, 0.0)
            if s0 == CTX_LEN and off < 0:
                v = jnp.where(row + off >= 0, v, 0.0)
            taps.append(v)
        xc = cb + taps[0] * cw[0:1]
        for tap in range(1, LRU_CONV_W):
            xc = xc + taps[tap] * cw[tap:tap + 1]
        xcb = xc.astype(BF16)
        for dd, (a_ref, u_ref) in enumerate(((af_ref, uf_ref), (ab_ref, ub_ref))):
            r = _sigmoid(jnp.dot(xcb, wa_ref[dd, 0].astype(BF16), preferred_element_type=F32) + ba_ref[dd])
            i = _sigmoid(jnp.dot(xcb, wx_ref[dd, 0].astype(BF16), preferred_element_type=F32) + bx_ref[dd])
            a = jnp.exp2(r * log2_decay[dd])
            y = 1.0 - a * a
            u = jnp.where(y > 0.0, y * lax.rsqrt(y), 0.0) * (i * xc)
            a_loc, u_loc = _group_linear_scan(a, u, rowmod, SUBLANES, reverse=(dd == 1))
            a_ref[s0:s0 + ROW_TILE, :] = a_loc
            u_ref[s0:s0 + ROW_TILE, :] = u_loc

    n_chunks = l // SUBLANES
    n_ctx_chunks = CTX_LEN // SUBLANES

    def fwd_body(c, carry):
        r0 = pl.multiple_of(c * SUBLANES, SUBLANES)
        hh = af_ref[pl.ds(r0, SUBLANES), :] * carry + uf_ref[pl.ds(r0, SUBLANES), :]
        uf_ref[pl.ds(r0, SUBLANES), :] = hh
        return jnp.broadcast_to(hh[SUBLANES - 1:SUBLANES, :], (SUBLANES, LANES))

    lax.fori_loop(0, n_chunks, fwd_body, jnp.zeros((SUBLANES, LANES), F32))

    def bwd_body(j, carry):
        c = jnp.where(j < n_ctx_chunks, n_ctx_chunks - 1 - j, n_chunks - 1 - (j - n_ctx_chunks))
        r0 = pl.multiple_of(c * SUBLANES, SUBLANES)
        hh = ab_ref[pl.ds(r0, SUBLANES), :] * carry + ub_ref[pl.ds(r0, SUBLANES), :]
        ub_ref[pl.ds(r0, SUBLANES), :] = hh
        return jnp.broadcast_to(hh[0:1, :], (SUBLANES, LANES))

    lax.fori_loop(0, n_chunks, bwd_body, jnp.zeros((SUBLANES, LANES), F32))

    for blk in range(nblk):
        s0 = blk * ROW_TILE
        hsum = uf_ref[s0:s0 + ROW_TILE, :] + ub_ref[s0:s0 + ROW_TILE, :]
        y_ref[0, s0:s0 + ROW_TILE, :] = (_gelu(g_ref[0, s0:s0 + ROW_TILE, :]) * hsum).astype(BF16)


def _rglru(p3, conv_w, conv_b, wa, ba, wx, bx, lam):
    l = p3.shape[1]
    b = p3.shape[0] // (N_IN_SPLITS * N_HEADS)
    gw = N_HEADS * HEAD_DIM
    vec = lambda a: a.reshape(2, 1, gw)
    spec_vec = pl.BlockSpec((2, 1, HEAD_DIM), lambda bb, h: (0, 0, h))
    spec_w = pl.BlockSpec((2, 1, HEAD_DIM, HEAD_DIM), lambda bb, h: (0, h, 0, 0))
    return pl.pallas_call(
        functools.partial(_rglru_kernel, seq_len=l),
        out_shape=jax.ShapeDtypeStruct((b, l, gw), BF16),
        grid=(b, N_HEADS),
        in_specs=[_head_slab_spec(b, l, 0), _head_slab_spec(b, l, 1),
                  pl.BlockSpec((LRU_CONV_W, HEAD_DIM), lambda bb, h: (0, h)),
                  pl.BlockSpec((1, HEAD_DIM), lambda bb, h: (0, h)),
                  spec_w, spec_vec, spec_w, spec_vec, spec_vec],
        out_specs=pl.BlockSpec((1, l, HEAD_DIM), lambda bb, h: (bb, 0, h)),
        scratch_shapes=[pltpu.VMEM((l + 2 * SUBLANES, LANES), F32)] + [pltpu.VMEM((l, LANES), F32)] * 4,
        compiler_params=_cparams(("arbitrary", "arbitrary"), VMEM_LIMIT),
        name="rglru",
    )(p3, p3, conv_w, conv_b.reshape(1, gw), wa, vec(ba), wx, vec(bx), vec(lam))


def _na_kernel(*refs, seq_len):
    nh = HEADS_PER_STEP
    q_refs, k_refs, v_refs = (refs[i * nh:(i + 1) * nh] for i in range(3))
    qg_ref, kg_ref, bias_ref, y_ref = refs[3 * nh:3 * nh + 4]
    scratch = refs[3 * nh + 4:]
    qb_refs, kb_refs, vb_refs = (scratch[i * nh:(i + 1) * nh] for i in range(3))
    l = seq_len
    scale = HEAD_DIM ** -0.5
    rows = (l - CTX_LEN) // GRID_W
    kr = min(NA_WIN_ROWS, rows)
    n_loc = kr * GRID_W

    def headnorm(z, g):
        return z * lax.rsqrt(jnp.mean(z * z, axis=-1, keepdims=True) + EPS) * g

    for hh in range(nh):
        cols = slice(hh * HEAD_DIM, (hh + 1) * HEAD_DIM)
        for s0 in range(0, l, ROW_TILE):
            rs = slice(s0, s0 + ROW_TILE)
            qb_refs[hh][rs, :] = headnorm(q_refs[hh][0, rs, :], qg_ref[...]).astype(BF16)
            kb_refs[hh][rs, :] = headnorm(k_refs[hh][0, rs, :], kg_ref[...]).astype(BF16)
            vb_refs[hh][rs, :] = v_refs[hh][0, rs, :].astype(BF16)
        k_ctx = kb_refs[hh][0:CTX_LEN, :]
        s_c = _bdot_nt(qb_refs[hh][0:CTX_LEN, :], k_ctx) * scale
        e_c = jnp.exp(s_c - jnp.max(s_c, axis=-1, keepdims=True))
        p_c = e_c / jnp.sum(e_c, axis=-1, keepdims=True)
        y_ref[0, 0:CTX_LEN, cols] = _bdot(p_c, vb_refs[hh][0:CTX_LEN, :]).astype(BF16)

    def scores(r, hh):
        r0 = jnp.clip(r - kr // 2, 0, rows - kr)
        qs = pl.multiple_of(CTX_LEN + r * GRID_W, GRID_W)
        ks = pl.multiple_of(CTX_LEN + r0 * GRID_W, GRID_W)
        q_r = qb_refs[hh][pl.ds(qs, GRID_W), :]
        bias = bias_ref[hh, r0 - r + NA_WIN_ROWS - 1]
        s_loc = _bdot_nt(q_r, kb_refs[hh][pl.ds(ks, n_loc), :]) * scale + bias
        s_ctx = _bdot_nt(q_r, kb_refs[hh][0:CTX_LEN, :]) * scale
        return hh, qs, ks, s_loc, s_ctx

    def softmax(hh, qs, ks, s_loc, s_ctx):
        m = jnp.maximum(jnp.max(s_loc, axis=-1, keepdims=True), jnp.max(s_ctx, axis=-1, keepdims=True))
        e_loc = jnp.exp(s_loc - m)
        e_ctx = jnp.exp(s_ctx - m)
        inv = 1.0 / (jnp.sum(e_loc, axis=-1, keepdims=True) + jnp.sum(e_ctx, axis=-1, keepdims=True))
        return hh, qs, ks, (e_loc * inv).astype(BF16), (e_ctx * inv).astype(BF16)

    def readout(hh, qs, ks, p_loc, p_ctx):
        o = _bdot(p_loc, vb_refs[hh][pl.ds(ks, n_loc), :]) + _bdot(p_ctx, vb_refs[hh][0:CTX_LEN, :])
        y_ref[0, pl.ds(qs, GRID_W), hh * HEAD_DIM:(hh + 1) * HEAD_DIM] = o.astype(BF16)

    group = 4 if rows % 4 == 0 else 1

    def rows_body(gi, carry):
        chains = [scores(gi * group + t, hh) for t in range(group) for hh in range(nh)]
        chains = [softmax(*ch) for ch in chains]
        for ch in chains:
            readout(*ch)
        return carry

    lax.fori_loop(0, rows // group, rows_body, 0)


def _na_bias_table(rpb, rows):
    kr = min(NA_WIN_ROWS, rows)
    qc = jnp.arange(GRID_W)[:, None]
    kc = jnp.arange(GRID_W)[None, :]
    win0 = jnp.clip(qc - NA_WIN_COLS // 2, 0, GRID_W - NA_WIN_COLS)
    col_ok = (kc >= win0) & (kc < win0 + NA_WIN_COLS)
    rel_c = jnp.clip(kc - qc + NA_WIN_COLS - 1, 0, 2 * NA_WIN_COLS - 2)
    t = jnp.where(col_ok[None, None], rpb.astype(F32)[:, :, rel_c], NEG_INF)
    d = jnp.arange(NA_WIN_ROWS)[:, None] + jnp.arange(kr)[None, :]
    d = jnp.clip(d, 0, 2 * NA_WIN_ROWS - 2)
    tb = t[:, d]
    return tb.transpose(0, 1, 3, 2, 4).reshape(rpb.shape[0], NA_WIN_ROWS, GRID_W, kr * GRID_W)


def _natten(p3, qn_g, kn_g, bias_tab):
    l = p3.shape[1]
    b = p3.shape[0] // (N_IN_SPLITS * N_HEADS)
    gw = N_HEADS * HEAD_DIM
    n_loc = bias_tab.shape[-1]
    nh = HEADS_PER_STEP
    slabs = [s for split in (2, 3, 4) for s in _pair_slab_specs(b, l, split)]
    return pl.pallas_call(
        functools.partial(_na_kernel, seq_len=l),
        out_shape=jax.ShapeDtypeStruct((b, l, gw), BF16),
        grid=(b, N_HEADS // nh),
        in_specs=slabs + [pl.BlockSpec((1, HEAD_DIM), lambda bb, hp: (0, 0)),
                          pl.BlockSpec((1, HEAD_DIM), lambda bb, hp: (0, 0)),
                          pl.BlockSpec((nh, NA_WIN_ROWS, GRID_W, n_loc), lambda bb, hp: (hp, 0, 0, 0))],
        out_specs=pl.BlockSpec((1, l, nh * HEAD_DIM), lambda bb, hp: (bb, 0, hp)),
        scratch_shapes=[pltpu.VMEM((l, LANES), BF16)] * (3 * nh),
        compiler_params=_cparams(("arbitrary", "arbitrary"), VMEM_LIMIT),
        name="natten",
    )(*([p3] * (3 * nh)), qn_g.reshape(1, HEAD_DIM), kn_g.reshape(1, HEAD_DIM), bias_tab)


def _dir_cums(lf, rowmod, c, reverse):
    if lf.shape[0] == 1:
        n_inc = (c - rowmod) if reverse else (rowmod + 1)
        n_oth = rowmod if reverse else (c - 1 - rowmod)
        inc = n_inc.astype(F32) * lf
        oth = n_oth.astype(F32) * lf
    else:
        n = lf.shape[0]
        row = lax.broadcasted_iota(jnp.int32, (n, n), 0)
        col = lax.broadcasted_iota(jnp.int32, (n, n), 1)
        same = (row // c) == (col // c)
        own = same & ((col >= row) if reverse else (col <= row))
        other = same & ((col < row) if reverse else (col > row))
        tri = jnp.concatenate([own, other], axis=0).astype(F32).astype(BF16)
        hi = lf.astype(BF16)
        r1 = lf - hi.astype(F32)
        mid = r1.astype(BF16)
        lo = (r1 - mid.astype(F32)).astype(BF16)
        acc = jnp.dot(tri, hi, preferred_element_type=F32)
        acc += jnp.dot(tri, mid, preferred_element_type=F32)
        acc += jnp.dot(tri, lo, preferred_element_type=F32)
        inc, oth = acc[:n], acc[n:]
    return inc, oth


def _decay_attn_blocks(s0, heads, c, ref_sets):
    n = ATTN_BLOCK
    row = lax.broadcasted_iota(jnp.int32, (n, n), 0)
    col = lax.broadcasted_iota(jnp.int32, (n, n), 1)
    rowmod = row % c
    same = (row // c) == (col // c)
    cums = [[_dir_cums(lf, rowmod, c, reverse) for reverse, lf in ((False, lf_f), (True, lf_b))]
            for (_, _, _, _, lf_f, lf_b) in heads]
    pending = []
    for hd, ((q, k_f, k_b, v, _, _), refs) in enumerate(zip(heads, ref_sets)):
        (qdf_ref, ksf_ref, ktf_ref, decf_ref, qdb_ref, ksb_ref, ktb_ref, decb_ref, vb_ref, o_ref) = refs
        vb = v.astype(BF16)
        vb_ref[pl.ds(s0, n), :] = vb
        scores = []
        for reverse, kk, (qd_ref, ks_ref, kt_ref, dec_ref) in (
                (False, k_f, (qdf_ref, ksf_ref, ktf_ref, decf_ref)),
                (True, k_b, (qdb_ref, ksb_ref, ktb_ref, decb_ref))):
            inc, oth = cums[hd][int(reverse)]
            kt = kk * jnp.exp(oth)
            dec = jnp.exp(inc + oth)
            qd = (q * jnp.exp(inc)).astype(BF16)
            ks = (kk * jnp.exp(-inc)).astype(BF16)
            qd_ref[pl.ds(s0, n), :] = qd
            ks_ref[pl.ds(s0, n), :] = ks
            kt_ref[pl.ds(s0, n), :] = kt.astype(BF16)
            dec_ref[pl.ds(s0, n), :] = dec
            scores.append((reverse, _bdot_nt(qd, ks)))
        pending.append((scores, vb, o_ref))
    for scores, vb, o_ref in pending:
        o = jnp.zeros((n, HEAD_DIM), F32)
        for reverse, s in scores:
            keep = same & ((col >= row) if reverse else (col <= row))
            o = o + jnp.dot(jnp.where(keep, s, 0.0).astype(BF16), vb, preferred_element_type=F32)
        o_ref[pl.ds(s0, n), :] = o


def _decay_attn_state_pass(c, seq_len, ref_sets):
    n_chunks = seq_len // c
    n_ctx_chunks = CTX_LEN // c

    steps = 4 if n_chunks % 4 == 0 else 3 if n_chunks % 3 == 0 else 1

    def body(jj, carry):
        chains = []
        for hh, refs in enumerate(ref_sets):
            (qdf_ref, _, ktf_ref, decf_ref, qdb_ref, _, ktb_ref, decb_ref, vb_ref, o_ref) = refs
            chains.append((qdf_ref, ktf_ref, decf_ref, vb_ref, o_ref, False))
            chains.append((qdb_ref, ktb_ref, decb_ref, vb_ref, o_ref, True))
        starts = []
        for t in range(steps):
            j = jj * steps + t
            jb = jnp.where(j < n_ctx_chunks, n_ctx_chunks - 1 - j, n_chunks - 1 - (j - n_ctx_chunks))
            starts.append((pl.multiple_of(j * c, c), pl.multiple_of(jb * c, c)))
        upds = [[_bdot_tn(vb_ref[pl.ds(starts[t][rev], c), :], kt_ref[pl.ds(starts[t][rev], c), :])
                 for t in range(steps)]
                for (_, kt_ref, _, vb_ref, _, rev) in chains]
        states = list(carry)
        for t in range(steps):
            for ci, (qd_ref, _, dec_ref, _, o_ref, rev) in enumerate(chains):
                r0 = starts[t][rev]
                o_ref[pl.ds(r0, c), :] += _bdot_nt(qd_ref[pl.ds(r0, c), :], states[ci])
                states[ci] = states[ci] * dec_ref[pl.ds(r0, 1), :] + upds[ci][t]
        return tuple(states)

    zero = jnp.zeros((HEAD_DIM, HEAD_DIM), F32)
    lax.fori_loop(0, n_chunks // steps, body, (zero,) * (2 * len(ref_sets)))


DECAY_SCRATCH_PER_HEAD = 10
HEADS_PER_STEP = 2


def _decay_scratch(l):
    one_dir = [pltpu.VMEM((l, LANES), BF16)] * 3 + [pltpu.VMEM((l, LANES), F32)]
    one_head = one_dir + one_dir + [pltpu.VMEM((l, LANES), BF16), pltpu.VMEM((l, LANES), F32)]
    assert len(one_head) == DECAY_SCRATCH_PER_HEAD
    return one_head * HEADS_PER_STEP


def _head_sets(scratch):
    n = DECAY_SCRATCH_PER_HEAD
    return [tuple(scratch[hh * n:(hh + 1) * n]) for hh in range(HEADS_PER_STEP)]


def _pair_slab_specs(bsz, l, split):
    return [pl.BlockSpec((1, l, HEAD_DIM),
                         lambda bb, hp, hh=hh: ((split * N_HEADS + hp * HEADS_PER_STEP + hh) * bsz + bb, 0, 0))
            for hh in range(HEADS_PER_STEP)]


def _retention_kernel(*refs, seq_len):
    nh = HEADS_PER_STEP
    q_refs, k_refs, v_refs, g_refs = (refs[i * nh:(i + 1) * nh] for i in range(4))
    cos_ref, sin_ref, lg_ref, gn_ref, y_ref = refs[4 * nh:4 * nh + 5]
    sets = _head_sets(refs[4 * nh + 5:])
    l = seq_len
    n = ATTN_BLOCK
    lane = lax.broadcasted_iota(jnp.int32, (n, HEAD_DIM), 1)
    first = (lane % (HEAD_DIM // 2)) < (HEAD_DIM // 4)

    def rope(z, cos, sin):
        swapped = jnp.where(first, pltpu.roll(z, HEAD_DIM - HEAD_DIM // 4, 1), pltpu.roll(z, HEAD_DIM // 4, 1))
        return z * cos + swapped * sin

    def block_body(bi, carry):
        s0 = pl.multiple_of(bi * n, n)
        cos = cos_ref[pl.ds(s0, n), :]
        sin = sin_ref[pl.ds(s0, n), :]
        heads = []
        for hh in range(nh):
            lg = lg_ref[hh]
            q = rope(q_refs[hh][0, pl.ds(s0, n), :], cos, sin)
            k = rope(k_refs[hh][0, pl.ds(s0, n), :], cos, sin) * (HEAD_DIM ** -0.5)
            heads.append((q, k, k, v_refs[hh][0, pl.ds(s0, n), :], lg[0:1], lg[1:2]))
        _decay_attn_blocks(s0, heads, RET_CHUNK, sets)
        return carry

    lax.fori_loop(0, l // n, block_body, 0)
    _decay_attn_state_pass(RET_CHUNK, l, sets)

    def out_body(bi, carry):
        s0 = pl.multiple_of(bi * n, n)
        for hh in range(nh):
            cols = slice(hh * HEAD_DIM, (hh + 1) * HEAD_DIM)
            o = sets[hh][-1][pl.ds(s0, n), :]
            o = o - jnp.mean(o, axis=-1, keepdims=True)
            o = o * lax.rsqrt(jnp.mean(o * o, axis=-1, keepdims=True) + EPS) * gn_ref[:, cols]
            y_ref[0, pl.ds(s0, n), cols] = (_silu(g_refs[hh][0, pl.ds(s0, n), :]) * o).astype(BF16)
        return carry

    lax.fori_loop(0, l // n, out_body, 0)


def _rope_tables(l):
    n_lat = l - CTX_LEN
    quarter = HEAD_DIM // 4
    t = jnp.arange(n_lat)
    inv_freq = ROPE_BASE ** (-jnp.arange(quarter, dtype=F32) / quarter)
    ang_r = (t // GRID_W).astype(F32)[:, None] * inv_freq[None, :]
    ang_c = (t % GRID_W).astype(F32)[:, None] * inv_freq[None, :]
    cos = jnp.concatenate([jnp.cos(ang_r), jnp.cos(ang_r), jnp.cos(ang_c), jnp.cos(ang_c)], axis=-1)
    sin = jnp.concatenate([-jnp.sin(ang_r), jnp.sin(ang_r), -jnp.sin(ang_c), jnp.sin(ang_c)], axis=-1)
    cos = jnp.concatenate([jnp.ones((CTX_LEN, HEAD_DIM), F32), cos], axis=0)
    sin = jnp.concatenate([jnp.zeros((CTX_LEN, HEAD_DIM), F32), sin], axis=0)
    return cos, sin


def _retention(p3, gn_g, cos, sin):
    l = p3.shape[1]
    b = p3.shape[0] // (N_IN_SPLITS * N_HEADS)
    gw = N_HEADS * HEAD_DIM
    log_gamma = jnp.log(1.0 - 2.0 ** (-5.0 - jnp.arange(N_HEADS, dtype=F32)))
    lg = jnp.stack([log_gamma, log_gamma[::-1]], axis=1)
    lg = jnp.broadcast_to(lg[:, :, None], (N_HEADS, 2, HEAD_DIM))
    nh = HEADS_PER_STEP
    full = pl.BlockSpec((l, HEAD_DIM), lambda bb, hp: (0, 0))
    slabs = [s for split in (5, 6, 7, 8) for s in _pair_slab_specs(b, l, split)]
    return pl.pallas_call(
        functools.partial(_retention_kernel, seq_len=l),
        out_shape=jax.ShapeDtypeStruct((b, l, gw), BF16),
        grid=(b, N_HEADS // nh),
        in_specs=slabs + [full, full,
                          pl.BlockSpec((nh, 2, HEAD_DIM), lambda bb, hp: (hp, 0, 0)),
                          pl.BlockSpec((1, nh * HEAD_DIM), lambda bb, hp: (0, hp))],
        out_specs=pl.BlockSpec((1, l, nh * HEAD_DIM), lambda bb, hp: (bb, 0, hp)),
        scratch_shapes=_decay_scratch(l),
        compiler_params=_cparams(("arbitrary", "arbitrary"), VMEM_LIMIT),
        name="retention",
    )(*([p3] * (4 * nh)), cos, sin, lg, gn_g.reshape(1, gw))


def _hgrn_kernel(*refs, seq_len, layer, depth):
    nh = HEADS_PER_STEP
    ff_refs, fb_refs, i_refs, q_refs, g_refs = (refs[i * nh:(i + 1) * nh] for i in range(5))
    lbl_ref, on_ref, y_ref = refs[5 * nh:5 * nh + 3]
    sets = _head_sets(refs[5 * nh + 3:])
    l = seq_len
    n = ATTN_BLOCK
    logits = lbl_ref[...]
    log_lb, log1m_lb = [], []
    for dd in range(2):
        xs = [logits[ll * 2 + dd] for ll in range(depth)]
        mx = functools.reduce(jnp.maximum, xs)
        es = [jnp.exp(x - mx) for x in xs]
        tot = functools.reduce(lambda a, b_: a + b_, es)
        lb = jnp.zeros_like(tot)
        for ll in range(1, layer + 1):
            lb = lb + es[ll] / tot
        log_lb.append(jnp.log(lb))
        log1m_lb.append(jnp.log1p(-lb))

    def gate(fz, dd, cols):
        log_sig = jnp.minimum(fz, 0.0) - jnp.log(1.0 + jnp.exp(-jnp.abs(fz)))
        a = jnp.broadcast_to(log_lb[dd][:, cols], fz.shape)
        bv = log1m_lb[dd][:, cols] + log_sig
        log_f = jnp.maximum(a, bv) + jnp.log(1.0 + jnp.exp(-jnp.abs(a - bv)))
        return log_f, 1.0 - jnp.exp(log_f)

    def block_body(bi, carry):
        s0 = pl.multiple_of(bi * n, n)
        heads = []
        for hh in range(nh):
            cols = slice(hh * HEAD_DIM, (hh + 1) * HEAD_DIM)
            lf_f, k_f = gate(ff_refs[hh][0, pl.ds(s0, n), :], 0, cols)
            lf_b, k_b = gate(fb_refs[hh][0, pl.ds(s0, n), :], 1, cols)
            heads.append((q_refs[hh][0, pl.ds(s0, n), :], k_f, k_b, i_refs[hh][0, pl.ds(s0, n), :], lf_f, lf_b))
        _decay_attn_blocks(s0, heads, HGRN_CHUNK, sets)
        return carry

    lax.fori_loop(0, l // n, block_body, 0)
    _decay_attn_state_pass(HGRN_CHUNK, l, sets)

    def out_body(bi, carry):
        s0 = pl.multiple_of(bi * n, n)
        for hh in range(nh):
            cols = slice(hh * HEAD_DIM, (hh + 1) * HEAD_DIM)
            o = sets[hh][-1][pl.ds(s0, n), :]
            o = o * lax.rsqrt(jnp.mean(o * o, axis=-1, keepdims=True) + EPS) * on_ref[:, cols]
            y_ref[0, pl.ds(s0, n), cols] = (_silu(g_refs[hh][0, pl.ds(s0, n), :]) * o).astype(BF16)
        return carry

    lax.fori_loop(0, l // n, out_body, 0)


def _hgrn2(p3, lb_logits, on_g, layer):
    l = p3.shape[1]
    b = p3.shape[0] // (N_IN_SPLITS * N_HEADS)
    gw = N_HEADS * HEAD_DIM
    depth = lb_logits.shape[0]
    nh = HEADS_PER_STEP
    slabs = [s for split in (9, 10, 11, 12, 13) for s in _pair_slab_specs(b, l, split)]
    return pl.pallas_call(
        functools.partial(_hgrn_kernel, seq_len=l, layer=layer, depth=depth),
        out_shape=jax.ShapeDtypeStruct((b, l, gw), BF16),
        grid=(b, N_HEADS // nh),
        in_specs=slabs + [pl.BlockSpec((depth * 2, 1, nh * HEAD_DIM), lambda bb, hp: (0, 0, hp)),
                          pl.BlockSpec((1, nh * HEAD_DIM), lambda bb, hp: (0, hp))],
        out_specs=pl.BlockSpec((1, l, nh * HEAD_DIM), lambda bb, hp: (bb, 0, hp)),
        scratch_shapes=_decay_scratch(l),
        compiler_params=_cparams(("arbitrary", "arbitrary"), VMEM_LIMIT),
        name="hgrn2",
    )(*([p3] * (5 * nh)), lb_logits.astype(F32).reshape(depth * 2, 1, gw), on_g.reshape(1, gw))


def _topk_rows(x, k):
    r_n = x.shape[0]
    idx = lax.broadcasted_iota(jnp.int32, x.shape, 0).astype(F32)
    rank = jnp.full(x.shape, float(k), F32)
    vals = []
    for r in range(k):
        m = jnp.max(x, axis=0, keepdims=True)
        first = jnp.min(jnp.where(x == m, idx, float(r_n)), axis=0, keepdims=True)
        sel = idx == first
        rank = jnp.where(sel, float(r), rank)
        x = jnp.where(sel, -jnp.inf, x)
        vals.append(m)
    return rank, jnp.concatenate(vals, axis=0)


def _bf16_pair_words(x):
    bits = pltpu.bitcast(x.astype(BF16).astype(F32), jnp.uint32)
    return bits | (bits >> 16)


def _peer_route_kernel(zt_ref, wq_ref, sk_ref, rank2_ref, lim1_ref, e1_ref, e2_ref, *, tb):
    k = PEER_TOPK
    n_kc, _, kc = wq_ref.shape
    q = jnp.dot(wq_ref[0], zt_ref[0, 0:kc, :], preferred_element_type=F32)
    for ci in range(1, n_kc):
        q += jnp.dot(wq_ref[ci], zt_ref[0, ci * kc:(ci + 1) * kc, :], preferred_element_type=F32)
    half = q.shape[0] // 2
    for c0 in range(0, tb, LANES):
        s1 = _bdot(sk_ref[0, 0], q[:half, c0:c0 + LANES])
        s2 = _bdot(sk_ref[0, 1], q[half:, c0:c0 + LANES])
        rank1, top1 = _topk_rows(s1, k)
        rank2, top2 = _topk_rows(s2, k)
        e1 = jnp.exp(s1 - top1[0:1])
        e2 = jnp.exp(s2 - top2[0:1])
        e1t = jnp.exp(top1 - top1[0:1])
        e2t = jnp.exp(top2 - top2[0:1])
        n_multi = sum(1 for r in range(k) if k // (r + 1) > 1)
        assert k - n_multi == SUBLANES
        pieces, epieces, widths = [], [], []
        for r in range(n_multi):
            n_c = k // (r + 1)
            n_pad = -(-n_c // SUBLANES) * SUBLANES
            valid = lax.broadcasted_iota(jnp.int32, (n_pad, LANES), 0) < n_c
            pieces.append(jnp.where(valid, top1[r:r + 1] + top2[0:n_pad], -jnp.inf))
            epieces.append(e1t[r:r + 1] * e2t[0:n_pad])
            widths.append(n_pad)
        pieces.append(top1[n_multi:k] + top2[0:1])
        epieces.append(e1t[n_multi:k] * e2t[0:1])
        cand = jnp.concatenate(pieces, axis=0)
        ecand = jnp.concatenate(epieces, axis=0)
        crank, _ = _topk_rows(cand, k)
        sel = crank < k
        z = jnp.sum(jnp.where(sel, ecand, 0.0), axis=0, keepdims=True)
        lim1 = jnp.zeros(rank1.shape, F32)
        self32 = sel.astype(F32)
        off = 0
        for r in range(n_multi):
            cnt = jnp.sum(self32[off:off + widths[r]], axis=0, keepdims=True)
            lim1 = jnp.where(rank1 == r, cnt, lim1)
            off += widths[r]
        for r in range(n_multi, k):
            lim1 = jnp.where(rank1 == r, self32[off + r - n_multi:off + r - n_multi + 1], lim1)
        rank2_ref[0, 0, :, c0:c0 + LANES] = rank2.astype(BF16)
        lim1_ref[0, 0, :, c0:c0 + LANES] = _bf16_pair_words(lim1)
        e1_ref[0, 0, :, c0:c0 + LANES] = _bf16_pair_words(e1)
        e2_ref[0, 0, :, c0:c0 + LANES] = (e2 / z).astype(BF16)


def _peer_route(zt, wq_t, sub_keys):
    nb, d, tb = zt.shape
    t = nb * tb
    n_kc, _, kc = wq_t.shape
    qd = wq_t.shape[1] // PEER_HEADS
    tabs = [jax.ShapeDtypeStruct((nb, PEER_HEADS, PEER_NKEYS, tb), dt)
            for dt in (BF16, jnp.uint32, jnp.uint32, BF16)]
    tab_spec = pl.BlockSpec((1, 1, PEER_NKEYS, tb), lambda i, h: (i, h, 0, 0))
    return pl.pallas_call(
        functools.partial(_peer_route_kernel, tb=tb),
        out_shape=tabs,
        grid=(t // tb, PEER_HEADS),
        in_specs=[pl.BlockSpec((1, d, tb), lambda i, h: (i, 0, 0)),
                  pl.BlockSpec((n_kc, qd, kc), lambda i, h: (0, h, 0)),
                  pl.BlockSpec((1, 2, PEER_NKEYS, qd // 2), lambda i, h: (h, 0, 0, 0))],
        out_specs=[tab_spec] * 4,
        compiler_params=_cparams(("arbitrary", "arbitrary"), VMEM_LIMIT),
        name="peer_route",
    )(zt, wq_t, sub_keys)


def _peer_expert_kernel(zt_ref, u_ref, vt_ref, rank2_ref, lim1_ref, e1_ref, e2_ref, o_ref, w_ref, *, ek):
    e = pl.program_id(1)
    tb = zt_ref.shape[2]
    n_i1 = ek // PEER_NKEYS

    @pl.when(e == 0)
    def _():
        o_ref[...] = jnp.zeros(o_ref.shape, F32)

    half = (PEER_NKEYS // 2, PEER_LANE_CHUNK)

    def gated(j, act):
        i1 = e * n_i1 + j
        for c0 in range(0, tb, PEER_LANE_CHUNK):
            cs = slice(c0, c0 + PEER_LANE_CHUNK)
            g = jnp.zeros((PEER_NKEYS, PEER_LANE_CHUNK), BF16)
            for h in range(PEER_HEADS):
                lim = pltpu.bitcast(jnp.broadcast_to(lim1_ref[0, h, pl.ds(i1, 1), cs], half), BF16)
                w1 = pltpu.bitcast(jnp.broadcast_to(e1_ref[0, h, pl.ds(i1, 1), cs], half), BF16)
                g = g + jnp.where(rank2_ref[0, h, :, cs] < lim, e2_ref[0, h, :, cs], jnp.zeros_like(g)) * w1
            w_ref[j * PEER_NKEYS:(j + 1) * PEER_NKEYS, cs] = (g.astype(F32) * _gelu(act[:, cs])).astype(BF16)

    per_batch = PEER_UP_BATCH // PEER_NKEYS
    per_group = PEER_DOWN_GROUP // PEER_NKEYS
    for b0 in range(0, n_i1, per_batch):
        acts = {j: jnp.dot(u_ref[j * PEER_NKEYS:(j + 1) * PEER_NKEYS, :], zt_ref[0], preferred_element_type=F32)
                for j in range(b0, b0 + per_batch)}
        for g0 in range(b0, b0 + per_batch, per_group):
            for j in range(g0, g0 + per_group):
                gated(j, acts[j])
            ks = slice(g0 * PEER_NKEYS, (g0 + per_group) * PEER_NKEYS)
            o_ref[0] += jnp.dot(vt_ref[0, :, ks], w_ref[ks, :], preferred_element_type=F32)


def _peer_experts(zt, u, vt, tabs, ek):
    nb, d, tb = zt.shape
    ne = u.shape[0]
    once = pl.Buffered(1)
    tab_spec = pl.BlockSpec((1, PEER_HEADS, PEER_NKEYS, tb), lambda i, e: (i, 0, 0, 0), pipeline_mode=once)
    return pl.pallas_call(
        functools.partial(_peer_expert_kernel, ek=ek),
        out_shape=jax.ShapeDtypeStruct((nb, d, tb), F32),
        grid=(nb, ne // ek),
        in_specs=[pl.BlockSpec((1, d, tb), lambda i, e: (i, 0, 0), pipeline_mode=once),
                  pl.BlockSpec((ek, d), lambda i, e: (e, 0)),
                  pl.BlockSpec((1, d, ek), lambda i, e: (e, 0, 0)),
                  tab_spec, tab_spec, tab_spec, tab_spec],
        out_specs=pl.BlockSpec((1, d, tb), lambda i, e: (i, 0, 0), pipeline_mode=once),
        scratch_shapes=[pltpu.VMEM((ek, tb), BF16)],
        compiler_params=_cparams(("arbitrary", "arbitrary"), PEER_VMEM_LIMIT),
        name="peer_experts",
    )(zt, u, vt, *tabs)


def kernel(x, c, ctx, c_ctx, norm1_g, norm2_g, ada_w, ada_b, w_in, w_out, lru_conv_w, lru_conv_b, lru_wa, lru_ba,
           lru_wx, lru_bx, lru_lam, na_qn_g, na_kn_g, na_rpb, ret_gn_g, hgrn_lb_logits, hgrn_on_g, peer_wq,
           peer_subkeys, peer_u, peer_v):
    bsz, n_lat, d = x.shape
    depth = w_in.shape[0]
    l = CTX_LEN + n_lat
    gw = d // N_GROUPS
    assert bsz < MOD_ROWS and n_lat % ROW_TILE == 0 and CTX_LEN % ROW_TILE == 0
    ctx_row = bsz

    c16 = jnp.zeros((MOD_ROWS, d), F32).at[:bsz].set(c).at[ctx_row].set(c_ctx)
    cos, sin = _rope_tables(l)
    rows = n_lat // GRID_W

    h3 = jnp.concatenate([ctx, x], axis=1)
    tiles_full = l // ROW_TILE
    tiles_lat = n_lat // ROW_TILE
    ctx_tiles = CTX_LEN // ROW_TILE

    def row_full(i):
        return jnp.where(i % tiles_full < ctx_tiles, ctx_row, i // tiles_full)

    def row_lat(i):
        return i // tiles_lat

    h = h3.reshape(bsz * l, d)
    pending = None
    for layer in range(depth):
        last = layer == depth - 1
        mod3 = _ada_modulation(c16, ada_w, ada_b, layer).reshape(MOD_ROWS * N_MOD, 1, d)
        if pending is None:
            (u,) = _resid_norm(h, mod3, row_full, norm_g=norm1_g[layer], shift_k=0, scale_k=1)
        else:
            ft, mod3_prev = pending
            h, u = _resid_norm(h, mod3, row_full, ft=ft, gate_mod3=mod3_prev, gate_k=5,
                               norm_g=norm1_g[layer], shift_k=0, scale_k=1)
        p = _matmul(u, w_in, layer, 1024, 1024)
        p3 = p.reshape(N_IN_SPLITS * N_HEADS * bsz, l, HEAD_DIM)
        y_a = _rglru(p3, lru_conv_w[layer], lru_conv_b[layer], lru_wa[layer], lru_ba[layer],
                     lru_wx[layer], lru_bx[layer], lru_lam[layer])
        y_b = _natten(p3, na_qn_g[layer], na_kn_g[layer], _na_bias_table(na_rpb[layer], rows))
        y_c = _retention(p3, ret_gn_g[layer], cos, sin)
        y_d = _hgrn2(p3, hgrn_lb_logits, hgrn_on_g[layer], layer)
        if last:
            hm3 = _out_proj((y_a, y_b, y_c, y_d), w_out, layer, h.reshape(bsz, l, d), mod3, row_off=ctx_tiles,
                            row_fn=lambda bb, i: bb, gate_k=2)
            row_fn = row_lat
        else:
            hm3 = _out_proj((y_a, y_b, y_c, y_d), w_out, layer, h.reshape(bsz, l, d), mod3, row_off=0,
                            row_fn=lambda bb, i: jnp.where(i < ctx_tiles, ctx_row, bb), gate_k=2)
            row_fn = row_full
        h = hm3.reshape(-1, d)
        (zt,) = _resid_norm(h, mod3, row_fn, norm_g=norm2_g[layer], shift_k=3, scale_k=4, transpose_out=True)
        wq_t = _cast_weight(peer_wq, layer, transpose=True, rows_per_step=512)
        u_b = _cast_weight(peer_u, layer, transpose=False, rows_per_step=512)
        v_t = _cast_weight(peer_v, layer, transpose=True, rows_per_step=PEER_EXPERT_BLOCK)
        tabs = _peer_route(zt, wq_t, peer_subkeys[layer])
        ft = _peer_experts(zt, u_b, v_t, tabs, PEER_EXPERT_BLOCK)
        if last:
            (h,) = _resid_norm(h, mod3, row_fn, ft=ft, gate_k=5)
        else:
            pending = (ft, mod3)
    return h.reshape(bsz, n_lat, d)
```

```python
import functools

import jax
import jax.numpy as jnp
from jax import lax
from jax.experimental import pallas as pl
from jax.experimental.pallas import tpu as pltpu

F32 = jnp.float32
BF16 = jnp.bfloat16

N_GROUPS = 4
N_HEADS = 8
HEAD_DIM = 128
N_IN_SPLITS = 14
N_MOD = 6
CTX_LEN = 256
GRID_W = 64
LRU_CONV_W = 4
LRU_C = 8.0
NA_WIN_ROWS = 8
NA_WIN_COLS = 16
RET_CHUNK = 128
HGRN_CHUNK = 32
PEER_HEADS = 8
PEER_NKEYS = 128
PEER_TOPK = 16
ROPE_BASE = 10000.0
EPS = 1e-6
NEG_INF = -1e30

SUBLANES = 8
LANES = 128
ROW_TILE = 256
ATTN_BLOCK = 128
MOD_ROWS = 16
PEER_TOKEN_BLOCK = 512
PEER_EXPERT_BLOCK = 1024
PEER_LANE_CHUNK = 256
PEER_DOWN_GROUP = 512
PEER_UP_BATCH = 1024
VMEM_LIMIT = 56 * 1024 * 1024
PEER_VMEM_LIMIT = 62 * 1024 * 1024


def _cparams(sem, vmem=None):
    return pltpu.CompilerParams(dimension_semantics=sem, vmem_limit_bytes=vmem)


def _bdot(a, b):
    return jnp.dot(a.astype(BF16), b.astype(BF16), preferred_element_type=F32)


def _bdot_nt(a, b):
    return lax.dot_general(a.astype(BF16), b.astype(BF16), (((1,), (1,)), ((), ())),
                           preferred_element_type=F32)


def _bdot_tn(a, b):
    return lax.dot_general(a.astype(BF16), b.astype(BF16), (((0,), (0,)), ((), ())),
                           preferred_element_type=F32)


def _gelu(x):
    c = 0.7978845608028654 * LOG2_E
    return x / (1.0 + jnp.exp2(x * (-2.0 * c - (2.0 * c * 0.044715) * (x * x))))


LOG2_E = 1.4426950408889634


def _sigmoid(x):
    return 1.0 / (1.0 + jnp.exp2(x * (-LOG2_E)))


def _silu(x):
    return x * _sigmoid(x)


def _ada_kernel(c_ref, w_ref, b_ref, o_ref):
    x = c_ref[...]
    x = _silu(x)
    hi = x.astype(BF16)
    lo = (x - hi.astype(F32)).astype(BF16)
    w = w_ref[0]
    whi = w.astype(BF16)
    wlo = (w - whi.astype(F32)).astype(BF16)
    acc = jnp.dot(hi, whi, preferred_element_type=F32)
    acc += jnp.dot(hi, wlo, preferred_element_type=F32)
    acc += jnp.dot(lo, whi, preferred_element_type=F32)
    o_ref[...] = acc + b_ref[0]


def _ada_modulation(c16, w_all, b_all, layer):
    depth, d, n = w_all.shape
    tn = 512
    return pl.pallas_call(
        _ada_kernel,
        out_shape=jax.ShapeDtypeStruct((MOD_ROWS, n), F32),
        grid=(n // tn,),
        in_specs=[pl.BlockSpec((MOD_ROWS, d), lambda j: (0, 0)),
                  pl.BlockSpec((1, d, tn), lambda j: (layer, 0, j)),
                  pl.BlockSpec((1, 1, tn), lambda j: (layer, 0, j))],
        out_specs=pl.BlockSpec((MOD_ROWS, tn), lambda j: (0, j)),
        compiler_params=_cparams(("arbitrary",), VMEM_LIMIT),
        name="ada_modulation",
    )(c16, w_all, b_all.reshape(depth, 1, n))


def _resid_norm_kernel(*refs, has_f, has_norm, transpose_out):
    it = iter(refs)
    h_ref = next(it)
    if has_f:
        ft_ref = next(it)
        gate_ref = next(it)
    if has_norm:
        g_ref = next(it)
        sh_ref = next(it)
        sc_ref = next(it)
    if has_f:
        hn_ref = next(it)
    if has_norm:
        u_ref = next(it)
    x = h_ref[...]
    if has_f:
        x = x + gate_ref[0] * ft_ref[0].T
        hn_ref[...] = x
    if has_norm:
        ms = jnp.mean(x * x, axis=-1, keepdims=True)
        y = x * lax.rsqrt(ms + EPS) * g_ref[...]
        y = y * (1.0 + sc_ref[0]) + sh_ref[0]
        if transpose_out:
            u_ref[0] = y.T.astype(BF16)
        else:
            u_ref[...] = y.astype(BF16)


def _resid_norm(h, mod3, row_fn, *, ft=None, gate_mod3=None, gate_k=None, norm_g=None, shift_k=None,
                scale_k=None, transpose_out=False):
    t, d = h.shape
    nt = t // ROW_TILE
    tiles_per_chunk = PEER_TOKEN_BLOCK // ROW_TILE
    chunk_spec = pl.BlockSpec((1, d, ROW_TILE), lambda i: (i // tiles_per_chunk, 0, i % tiles_per_chunk))
    has_f = ft is not None
    has_norm = norm_g is not None
    in_specs = [pl.BlockSpec((ROW_TILE, d), lambda i: (i, 0))]
    args = [h]
    if has_f:
        in_specs += [chunk_spec,
                     pl.BlockSpec((1, 1, d), lambda i: (row_fn(i) * N_MOD + gate_k, 0, 0))]
        args += [ft, mod3 if gate_mod3 is None else gate_mod3]
    if has_norm:
        in_specs += [pl.BlockSpec((1, d), lambda i: (0, 0)),
                     pl.BlockSpec((1, 1, d), lambda i: (row_fn(i) * N_MOD + shift_k, 0, 0)),
                     pl.BlockSpec((1, 1, d), lambda i: (row_fn(i) * N_MOD + scale_k, 0, 0))]
        args += [norm_g.reshape(1, d), mod3, mod3]
    out_shape, out_specs = [], []
    if has_f:
        out_shape.append(jax.ShapeDtypeStruct((t, d), F32))
        out_specs.append(pl.BlockSpec((ROW_TILE, d), lambda i: (i, 0)))
    if has_norm:
        if transpose_out:
            out_shape.append(jax.ShapeDtypeStruct((t // PEER_TOKEN_BLOCK, d, PEER_TOKEN_BLOCK), BF16))
            out_specs.append(chunk_spec)
        else:
            out_shape.append(jax.ShapeDtypeStruct((t, d), BF16))
            out_specs.append(pl.BlockSpec((ROW_TILE, d), lambda i: (i, 0)))
    outs = pl.pallas_call(
        functools.partial(_resid_norm_kernel, has_f=has_f, has_norm=has_norm, transpose_out=transpose_out),
        out_shape=out_shape, grid=(nt,), in_specs=in_specs, out_specs=out_specs,
        compiler_params=_cparams(("arbitrary",), VMEM_LIMIT),
        name="resid_norm",
    )(*args)
    return outs


def _mm_kernel(x_ref, w_ref, o_ref, wb_ref):
    @pl.when(pl.program_id(1) == 0)
    def _():
        wb_ref[...] = w_ref[0].astype(BF16)

    r = jnp.dot(x_ref[...], wb_ref[...], preferred_element_type=F32)
    for hh in range(o_ref.shape[0]):
        o_ref[hh] = r[:, hh * HEAD_DIM:(hh + 1) * HEAD_DIM]


def _matmul(x, w_all, layer, tm, tn):
    m, k = x.shape
    n = w_all.shape[2]
    return pl.pallas_call(
        _mm_kernel,
        out_shape=jax.ShapeDtypeStruct((n // HEAD_DIM, m, HEAD_DIM), F32),
        grid=(n // tn, m // tm),
        in_specs=[pl.BlockSpec((tm, k), lambda j, i: (i, 0)),
                  pl.BlockSpec((1, k, tn), lambda j, i: (layer, 0, j), pipeline_mode=pl.Buffered(1))],
        out_specs=pl.BlockSpec((tn // HEAD_DIM, tm, HEAD_DIM), lambda j, i: (j, i, 0)),
        scratch_shapes=[pltpu.VMEM((k, tn), BF16)],
        compiler_params=_cparams(("arbitrary", "arbitrary"), VMEM_LIMIT),
        name="in_proj",
    )(x, w_all)


def _cast_kernel(w_ref, o_ref, *, transpose):
    w = w_ref[0]
    if transpose:
        o_ref[0] = w.T.astype(BF16)
    else:
        o_ref[...] = w.astype(BF16)


def _cast_weight(w_all, layer, *, transpose, rows_per_step):
    _, r, c = w_all.shape
    tr = rows_per_step
    if transpose:
        out_shape, out_spec = (r // tr, c, tr), pl.BlockSpec((1, c, tr), lambda i: (i, 0, 0))
    else:
        out_shape, out_spec = (r, c), pl.BlockSpec((tr, c), lambda i: (i, 0))
    return pl.pallas_call(
        functools.partial(_cast_kernel, transpose=transpose),
        out_shape=jax.ShapeDtypeStruct(out_shape, BF16),
        grid=(r // tr,),
        in_specs=[pl.BlockSpec((1, tr, c), lambda i: (layer, i, 0))],
        out_specs=out_spec,
        compiler_params=_cparams(("arbitrary",), VMEM_LIMIT),
        name="cast_weight",
    )(w_all)


def _head_slab_spec(bsz, l, split):
    return pl.BlockSpec((1, l, HEAD_DIM), lambda bb, h: ((split * N_HEADS + h) * bsz + bb, 0, 0))


def _wout_kernel(ya_ref, yb_ref, yc_ref, yd_ref, wb_ref, h_ref, gate_ref, o_ref):
    gw = ya_ref.shape[2]
    acc = jnp.dot(ya_ref[0], wb_ref[0 * gw:1 * gw, :], preferred_element_type=F32)
    acc += jnp.dot(yb_ref[0], wb_ref[1 * gw:2 * gw, :], preferred_element_type=F32)
    acc += jnp.dot(yc_ref[0], wb_ref[2 * gw:3 * gw, :], preferred_element_type=F32)
    acc += jnp.dot(yd_ref[0], wb_ref[3 * gw:4 * gw, :], preferred_element_type=F32)
    o_ref[0] = h_ref[0] + gate_ref[0] * acc


def _out_proj(ys, w_all, layer, h3, mod3, *, row_off, row_fn, gate_k):
    b, l, gw = ys[0].shape
    d = w_all.shape[2]
    nrt = l // ROW_TILE - row_off
    tn = 2048
    w_b = _cast_weight(w_all, layer, transpose=False, rows_per_step=512)
    y_spec = pl.BlockSpec((1, ROW_TILE, gw), lambda j, bb, i: (bb, i + row_off, 0))
    return pl.pallas_call(
        _wout_kernel,
        out_shape=jax.ShapeDtypeStruct((b, nrt * ROW_TILE, d), F32),
        grid=(d // tn, b, nrt),
        in_specs=[y_spec, y_spec, y_spec, y_spec,
                  pl.BlockSpec((N_GROUPS * gw, tn), lambda j, bb, i: (0, j), pipeline_mode=pl.Buffered(1)),
                  pl.BlockSpec((1, ROW_TILE, tn), lambda j, bb, i: (bb, i + row_off, j)),
                  pl.BlockSpec((1, 1, tn), lambda j, bb, i: (row_fn(bb, i) * N_MOD + gate_k, 0, j))],
        out_specs=pl.BlockSpec((1, ROW_TILE, tn), lambda j, bb, i: (bb, i, j)),
        compiler_params=_cparams(("arbitrary", "arbitrary", "arbitrary"), VMEM_LIMIT),
        name="out_proj",
    )(*ys, w_b, h3, mod3)


def _group_cumsum(x, rowmod, c, reverse):
    n = x.shape[0]
    k = 1
    while k < c:
        if reverse:
            x = x + jnp.where(rowmod < c - k, pltpu.roll(x, n - k, 0), 0.0)
        else:
            x = x + jnp.where(rowmod >= k, pltpu.roll(x, k, 0), 0.0)
        k *= 2
    return x


def _group_linear_scan(a, u, rowmod, c, reverse):
    n, w = a.shape

    def shifted(x, s):
        return pltpu.roll(x.reshape(n // c, c, w), s, 1).reshape(n, w)

    k = 1
    while k < c:
        if reverse:
            ok = rowmod < c - k
            a_s = shifted(a, c - k)
            u_s = shifted(u, c - k)
        else:
            ok = rowmod >= k
            a_s = shifted(a, k)
            u_s = shifted(u, k)
        u = jnp.where(ok, a * u_s + u, u)
        a = jnp.where(ok, a * a_s, a)
        k *= 2
    return a, u


def _rglru_kernel(x_ref, g_ref, cw_ref, cb_ref, wa_ref, ba_ref, wx_ref, bx_ref, lam_ref, y_ref,
                  xp_ref, af_ref, uf_ref, ab_ref, ub_ref, *, seq_len):
    l = seq_len
    pad = SUBLANES
    nblk = l // ROW_TILE
    zeros_pad = jnp.zeros((pad, LANES), F32)
    xp_ref[0:pad, :] = zeros_pad
    xp_ref[pad + l:pad + l + pad, :] = zeros_pad
    xp_ref[pad:pad + l, :] = x_ref[0]

    row = lax.broadcasted_iota(jnp.int32, (ROW_TILE, LANES), 0)
    rowmod = row % SUBLANES
    cw = cw_ref[...]
    cb = cb_ref[...]
    lam = lam_ref[...]
    log2_decay = [(jnp.maximum(-lam[dd], 0.0) + jnp.log1p(jnp.exp(-jnp.abs(lam[dd])))) * (-LRU_C * LOG2_E)
                  for dd in range(2)]

    for blk in range(nblk):
        s0 = blk * ROW_TILE
        taps = []
        for tap in range(LRU_CONV_W):
            off = tap - LRU_CONV_W // 2
            v = xp_ref[pad + s0 + off:pad + s0 + off + ROW_TILE, :]
            if blk == 0 and off > 0:
                v = jnp.where(row + off < CTX_LEN, v, 0.0)
            if s0 == CTX_LEN and off < 0:
                v = jnp.where(row + off >= 0, v, 0.0)
            taps.append(v)
        xc = cb + taps[0] * cw[0:1]
        for tap in range(1, LRU_CONV_W):
            xc = xc + taps[tap] * cw[tap:tap + 1]
        xcb = xc.astype(BF16)
        for dd, (a_ref, u_ref) in enumerate(((af_ref, uf_ref), (ab_ref, ub_ref))):
            r = _sigmoid(jnp.dot(xcb, wa_ref[dd, 0].astype(BF16), preferred_element_type=F32) + ba_ref[dd])
            i = _sigmoid(jnp.dot(xcb, wx_ref[dd, 0].astype(BF16), preferred_element_type=F32) + bx_ref[dd])
            a = jnp.exp2(r * log2_decay[dd])
            y = 1.0 - a * a
            u = jnp.where(y > 0.0, y * lax.rsqrt(y), 0.0) * (i * xc)
            a_loc, u_loc = _group_linear_scan(a, u, rowmod, SUBLANES, reverse=(dd == 1))
            a_ref[s0:s0 + ROW_TILE, :] = a_loc
            u_ref[s0:s0 + ROW_TILE, :] = u_loc

    n_chunks = l // SUBLANES
    n_ctx_chunks = CTX_LEN // SUBLANES

    def fwd_body(c, carry):
        r0 = pl.multiple_of(c * SUBLANES, SUBLANES)
        hh = af_ref[pl.ds(r0, SUBLANES), :] * carry + uf_ref[pl.ds(r0, SUBLANES), :]
        uf_ref[pl.ds(r0, SUBLANES), :] = hh
        return jnp.broadcast_to(hh[SUBLANES - 1:SUBLANES, :], (SUBLANES, LANES))

    lax.fori_loop(0, n_chunks, fwd_body, jnp.zeros((SUBLANES, LANES), F32))

    def bwd_body(j, carry):
        c = jnp.where(j < n_ctx_chunks, n_ctx_chunks - 1 - j, n_chunks - 1 - (j - n_ctx_chunks))
        r0 = pl.multiple_of(c * SUBLANES, SUBLANES)
        hh = ab_ref[pl.ds(r0, SUBLANES), :] * carry + ub_ref[pl.ds(r0, SUBLANES), :]
        ub_ref[pl.ds(r0, SUBLANES), :] = hh
        return jnp.broadcast_to(hh[0:1, :], (SUBLANES, LANES))

    lax.fori_loop(0, n_chunks, bwd_body, jnp.zeros((SUBLANES, LANES), F32))

    for blk in range(nblk):
        s0 = blk * ROW_TILE
        hsum = uf_ref[s0:s0 + ROW_TILE, :] + ub_ref[s0:s0 + ROW_TILE, :]
        y_ref[0, s0:s0 + ROW_TILE, :] = (_gelu(g_ref[0, s0:s0 + ROW_TILE, :]) * hsum).astype(BF16)


def _rglru(p3, conv_w, conv_b, wa, ba, wx, bx, lam):
    l = p3.shape[1]
    b = p3.shape[0] // (N_IN_SPLITS * N_HEADS)
    gw = N_HEADS * HEAD_DIM
    vec = lambda a: a.reshape(2, 1, gw)
    spec_vec = pl.BlockSpec((2, 1, HEAD_DIM), lambda bb, h: (0, 0, h))
    spec_w = pl.BlockSpec((2, 1, HEAD_DIM, HEAD_DIM), lambda bb, h: (0, h, 0, 0))
    return pl.pallas_call(
        functools.partial(_rglru_kernel, seq_len=l),
        out_shape=jax.ShapeDtypeStruct((b, l, gw), BF16),
        grid=(b, N_HEADS),
        in_specs=[_head_slab_spec(b, l, 0), _head_slab_spec(b, l, 1),
                  pl.BlockSpec((LRU_CONV_W, HEAD_DIM), lambda bb, h: (0, h)),
                  pl.BlockSpec((1, HEAD_DIM), lambda bb, h: (0, h)),
                  spec_w, spec_vec, spec_w, spec_vec, spec_vec],
        out_specs=pl.BlockSpec((1, l, HEAD_DIM), lambda bb, h: (bb, 0, h)),
        scratch_shapes=[pltpu.VMEM((l + 2 * SUBLANES, LANES), F32)] + [pltpu.VMEM((l, LANES), F32)] * 4,
        compiler_params=_cparams(("arbitrary", "arbitrary"), VMEM_LIMIT),
        name="rglru",
    )(p3, p3, conv_w, conv_b.reshape(1, gw), wa, vec(ba), wx, vec(bx), vec(lam))


def _na_kernel(*refs, seq_len):
    nh = HEADS_PER_STEP
    q_refs, k_refs, v_refs = (refs[i * nh:(i + 1) * nh] for i in range(3))
    qg_ref, kg_ref, bias_ref, y_ref = refs[3 * nh:3 * nh + 4]
    scratch = refs[3 * nh + 4:]
    qb_refs, kb_refs, vb_refs = (scratch[i * nh:(i + 1) * nh] for i in range(3))
    l = seq_len
    scale = HEAD_DIM ** -0.5
    rows = (l - CTX_LEN) // GRID_W
    kr = min(NA_WIN_ROWS, rows)
    n_loc = kr * GRID_W

    def headnorm(z, g):
        return z * lax.rsqrt(jnp.mean(z * z, axis=-1, keepdims=True) + EPS) * g

    for hh in range(nh):
        cols = slice(hh * HEAD_DIM, (hh + 1) * HEAD_DIM)
        for s0 in range(0, l, ROW_TILE):
            rs = slice(s0, s0 + ROW_TILE)
            qb_refs[hh][rs, :] = headnorm(q_refs[hh][0, rs, :], qg_ref[...]).astype(BF16)
            kb_refs[hh][rs, :] = headnorm(k_refs[hh][0, rs, :], kg_ref[...]).astype(BF16)
            vb_refs[hh][rs, :] = v_refs[hh][0, rs, :].astype(BF16)
        k_ctx = kb_refs[hh][0:CTX_LEN, :]
        s_c = _bdot_nt(qb_refs[hh][0:CTX_LEN, :], k_ctx) * scale
        e_c = jnp.exp(s_c - jnp.max(s_c, axis=-1, keepdims=True))
        p_c = e_c / jnp.sum(e_c, axis=-1, keepdims=True)
        y_ref[0, 0:CTX_LEN, cols] = _bdot(p_c, vb_refs[hh][0:CTX_LEN, :]).astype(BF16)

    def scores(r, hh):
        r0 = jnp.clip(r - kr // 2, 0, rows - kr)
        qs = pl.multiple_of(CTX_LEN + r * GRID_W, GRID_W)
        ks = pl.multiple_of(CTX_LEN + r0 * GRID_W, GRID_W)
        q_r = qb_refs[hh][pl.ds(qs, GRID_W), :]
        bias = bias_ref[hh, r0 - r + NA_WIN_ROWS - 1]
        s_loc = _bdot_nt(q_r, kb_refs[hh][pl.ds(ks, n_loc), :]) * scale + bias
        s_ctx = _bdot_nt(q_r, kb_refs[hh][0:CTX_LEN, :]) * scale
        return hh, qs, ks, s_loc, s_ctx

    def softmax(hh, qs, ks, s_loc, s_ctx):
        m = jnp.maximum(jnp.max(s_loc, axis=-1, keepdims=True), jnp.max(s_ctx, axis=-1, keepdims=True))
        e_loc = jnp.exp(s_loc - m)
        e_ctx = jnp.exp(s_ctx - m)
        inv = 1.0 / (jnp.sum(e_loc, axis=-1, keepdims=True) + jnp.sum(e_ctx, axis=-1, keepdims=True))
        return hh, qs, ks, (e_loc * inv).astype(BF16), (e_ctx * inv).astype(BF16)

    def readout(hh, qs, ks, p_loc, p_ctx):
        o = _bdot(p_loc, vb_refs[hh][pl.ds(ks, n_loc), :]) + _bdot(p_ctx, vb_refs[hh][0:CTX_LEN, :])
        y_ref[0, pl.ds(qs, GRID_W), hh * HEAD_DIM:(hh + 1) * HEAD_DIM] = o.astype(BF16)

    group = 4 if rows % 4 == 0 else 1

    def rows_body(gi, carry):
        chains = [scores(gi * group + t, hh) for t in range(group) for hh in range(nh)]
        chains = [softmax(*ch) for ch in chains]
        for ch in chains:
            readout(*ch)
        return carry

    lax.fori_loop(0, rows // group, rows_body, 0)


def _na_bias_table(rpb, rows):
    kr = min(NA_WIN_ROWS, rows)
    qc = jnp.arange(GRID_W)[:, None]
    kc = jnp.arange(GRID_W)[None, :]
    win0 = jnp.clip(qc - NA_WIN_COLS // 2, 0, GRID_W - NA_WIN_COLS)
    col_ok = (kc >= win0) & (kc < win0 + NA_WIN_COLS)
    rel_c = jnp.clip(kc - qc + NA_WIN_COLS - 1, 0, 2 * NA_WIN_COLS - 2)
    t = jnp.where(col_ok[None, None], rpb.astype(F32)[:, :, rel_c], NEG_INF)
    d = jnp.arange(NA_WIN_ROWS)[:, None] + jnp.arange(kr)[None, :]
    d = jnp.clip(d, 0, 2 * NA_WIN_ROWS - 2)
    tb = t[:, d]
    return tb.transpose(0, 1, 3, 2, 4).reshape(rpb.shape[0], NA_WIN_ROWS, GRID_W, kr * GRID_W)


def _natten(p3, qn_g, kn_g, bias_tab):
    l = p3.shape[1]
    b = p3.shape[0] // (N_IN_SPLITS * N_HEADS)
    gw = N_HEADS * HEAD_DIM
    n_loc = bias_tab.shape[-1]
    nh = HEADS_PER_STEP
    slabs = [s for split in (2, 3, 4) for s in _pair_slab_specs(b, l, split)]
    return pl.pallas_call(
        functools.partial(_na_kernel, seq_len=l),
        out_shape=jax.ShapeDtypeStruct((b, l, gw), BF16),
        grid=(b, N_HEADS // nh),
        in_specs=slabs + [pl.BlockSpec((1, HEAD_DIM), lambda bb, hp: (0, 0)),
                          pl.BlockSpec((1, HEAD_DIM), lambda bb, hp: (0, 0)),
                          pl.BlockSpec((nh, NA_WIN_ROWS, GRID_W, n_loc), lambda bb, hp: (hp, 0, 0, 0))],
        out_specs=pl.BlockSpec((1, l, nh * HEAD_DIM), lambda bb, hp: (bb, 0, hp)),
        scratch_shapes=[pltpu.VMEM((l, LANES), BF16)] * (3 * nh),
        compiler_params=_cparams(("arbitrary", "arbitrary"), VMEM_LIMIT),
        name="natten",
    )(*([p3] * (3 * nh)), qn_g.reshape(1, HEAD_DIM), kn_g.reshape(1, HEAD_DIM), bias_tab)


def _dir_cums(lf, rowmod, c, reverse):
    if lf.shape[0] == 1:
        n_inc = (c - rowmod) if reverse else (rowmod + 1)
        n_oth = rowmod if reverse else (c - 1 - rowmod)
        inc = n_inc.astype(F32) * lf
        oth = n_oth.astype(F32) * lf
    else:
        n = lf.shape[0]
        row = lax.broadcasted_iota(jnp.int32, (n, n), 0)
        col = lax.broadcasted_iota(jnp.int32, (n, n), 1)
        same = (row // c) == (col // c)
        own = same & ((col >= row) if reverse else (col <= row))
        other = same & ((col < row) if reverse else (col > row))
        tri = jnp.concatenate([own, other], axis=0).astype(F32).astype(BF16)
        hi = lf.astype(BF16)
        r1 = lf - hi.astype(F32)
        mid = r1.astype(BF16)
        lo = (r1 - mid.astype(F32)).astype(BF16)
        acc = jnp.dot(tri, hi, preferred_element_type=F32)
        acc += jnp.dot(tri, mid, preferred_element_type=F32)
        acc += jnp.dot(tri, lo, preferred_element_type=F32)
        inc, oth = acc[:n], acc[n:]
    return inc, oth


def _decay_attn_blocks(s0, heads, c, ref_sets):
    n = ATTN_BLOCK
    row = lax.broadcasted_iota(jnp.int32, (n, n), 0)
    col = lax.broadcasted_iota(jnp.int32, (n, n), 1)
    rowmod = row % c
    same = (row // c) == (col // c)
    cums = [[_dir_cums(lf, rowmod, c, reverse) for reverse, lf in ((False, lf_f), (True, lf_b))]
            for (_, _, _, _, lf_f, lf_b) in heads]
    pending = []
    for hd, ((q, k_f, k_b, v, _, _), refs) in enumerate(zip(heads, ref_sets)):
        (qdf_ref, ksf_ref, ktf_ref, decf_ref, qdb_ref, ksb_ref, ktb_ref, decb_ref, vb_ref, o_ref) = refs
        vb = v.astype(BF16)
        vb_ref[pl.ds(s0, n), :] = vb
        scores = []
        for reverse, kk, (qd_ref, ks_ref, kt_ref, dec_ref) in (
                (False, k_f, (qdf_ref, ksf_ref, ktf_ref, decf_ref)),
                (True, k_b, (qdb_ref, ksb_ref, ktb_ref, decb_ref))):
            inc, oth = cums[hd][int(reverse)]
            kt = kk * jnp.exp(oth)
            dec = jnp.exp(inc + oth)
            qd = (q * jnp.exp(inc)).astype(BF16)
            ks = (kk * jnp.exp(-inc)).astype(BF16)
            qd_ref[pl.ds(s0, n), :] = qd
            ks_ref[pl.ds(s0, n), :] = ks
            kt_ref[pl.ds(s0, n), :] = kt.astype(BF16)
            dec_ref[pl.ds(s0, n), :] = dec
            scores.append((reverse, _bdot_nt(qd, ks)))
        pending.append((scores, vb, o_ref))
    for scores, vb, o_ref in pending:
        o = jnp.zeros((n, HEAD_DIM), F32)
        for reverse, s in scores:
            keep = same & ((col >= row) if reverse else (col <= row))
            o = o + jnp.dot(jnp.where(keep, s, 0.0).astype(BF16), vb, preferred_element_type=F32)
        o_ref[pl.ds(s0, n), :] = o


def _decay_attn_state_pass(c, seq_len, ref_sets):
    n_chunks = seq_len // c
    n_ctx_chunks = CTX_LEN // c

    steps = 4 if n_chunks % 4 == 0 else 3 if n_chunks % 3 == 0 else 1

    def body(jj, carry):
        chains = []
        for hh, refs in enumerate(ref_sets):
            (qdf_ref, _, ktf_ref, decf_ref, qdb_ref, _, ktb_ref, decb_ref, vb_ref, o_ref) = refs
            chains.append((qdf_ref, ktf_ref, decf_ref, vb_ref, o_ref, False))
            chains.append((qdb_ref, ktb_ref, decb_ref, vb_ref, o_ref, True))
        starts = []
        for t in range(steps):
            j = jj * steps + t
            jb = jnp.where(j < n_ctx_chunks, n_ctx_chunks - 1 - j, n_chunks - 1 - (j - n_ctx_chunks))
            starts.append((pl.multiple_of(j * c, c), pl.multiple_of(jb * c, c)))
        upds = [[_bdot_tn(vb_ref[pl.ds(starts[t][rev], c), :], kt_ref[pl.ds(starts[t][rev], c), :])
                 for t in range(steps)]
                for (_, kt_ref, _, vb_ref, _, rev) in chains]
        states = list(carry)
        for t in range(steps):
            for ci, (qd_ref, _, dec_ref, _, o_ref, rev) in enumerate(chains):
                r0 = starts[t][rev]
                o_ref[pl.ds(r0, c), :] += _bdot_nt(qd_ref[pl.ds(r0, c), :], states[ci])
                states[ci] = states[ci] * dec_ref[pl.ds(r0, 1), :] + upds[ci][t]
        return tuple(states)

    zero = jnp.zeros((HEAD_DIM, HEAD_DIM), F32)
    lax.fori_loop(0, n_chunks // steps, body, (zero,) * (2 * len(ref_sets)))


DECAY_SCRATCH_PER_HEAD = 10
HEADS_PER_STEP = 2


def _decay_scratch(l):
    one_dir = [pltpu.VMEM((l, LANES), BF16)] * 3 + [pltpu.VMEM((l, LANES), F32)]
    one_head = one_dir + one_dir + [pltpu.VMEM((l, LANES), BF16), pltpu.VMEM((l, LANES), F32)]
    assert len(one_head) == DECAY_SCRATCH_PER_HEAD
    return one_head * HEADS_PER_STEP


def _head_sets(scratch):
    n = DECAY_SCRATCH_PER_HEAD
    return [tuple(scratch[hh * n:(hh + 1) * n]) for hh in range(HEADS_PER_STEP)]


def _pair_slab_specs(bsz, l, split):
    return [pl.BlockSpec((1, l, HEAD_DIM),
                         lambda bb, hp, hh=hh: ((split * N_HEADS + hp * HEADS_PER_STEP + hh) * bsz + bb, 0, 0))
            for hh in range(HEADS_PER_STEP)]


def _retention_kernel(*refs, seq_len):
    nh = HEADS_PER_STEP
    q_refs, k_refs, v_refs, g_refs = (refs[i * nh:(i + 1) * nh] for i in range(4))
    cos_ref, sin_ref, lg_ref, gn_ref, y_ref = refs[4 * nh:4 * nh + 5]
    sets = _head_sets(refs[4 * nh + 5:])
    l = seq_len
    n = ATTN_BLOCK
    lane = lax.broadcasted_iota(jnp.int32, (n, HEAD_DIM), 1)
    first = (lane % (HEAD_DIM // 2)) < (HEAD_DIM // 4)

    def rope(z, cos, sin):
        swapped = jnp.where(first, pltpu.roll(z, HEAD_DIM - HEAD_DIM // 4, 1), pltpu.roll(z, HEAD_DIM // 4, 1))
        return z * cos + swapped * sin

    def block_body(bi, carry):
        s0 = pl.multiple_of(bi * n, n)
        cos = cos_ref[pl.ds(s0, n), :]
        sin = sin_ref[pl.ds(s0, n), :]
        heads = []
        for hh in range(nh):
            lg = lg_ref[hh]
            q = rope(q_refs[hh][0, pl.ds(s0, n), :], cos, sin)
            k = rope(k_refs[hh][0, pl.ds(s0, n), :], cos, sin) * (HEAD_DIM ** -0.5)
            heads.append((q, k, k, v_refs[hh][0, pl.ds(s0, n), :], lg[0:1], lg[1:2]))
        _decay_attn_blocks(s0, heads, RET_CHUNK, sets)
        return carry

    lax.fori_loop(0, l // n, block_body, 0)
    _decay_attn_state_pass(RET_CHUNK, l, sets)

    def out_body(bi, carry):
        s0 = pl.multiple_of(bi * n, n)
        for hh in range(nh):
            cols = slice(hh * HEAD_DIM, (hh + 1) * HEAD_DIM)
            o = sets[hh][-1][pl.ds(s0, n), :]
            o = o - jnp.mean(o, axis=-1, keepdims=True)
            o = o * lax.rsqrt(jnp.mean(o * o, axis=-1, keepdims=True) + EPS) * gn_ref[:, cols]
            y_ref[0, pl.ds(s0, n), cols] = (_silu(g_refs[hh][0, pl.ds(s0, n), :]) * o).astype(BF16)
        return carry

    lax.fori_loop(0, l // n, out_body, 0)


def _rope_tables(l):
    n_lat = l - CTX_LEN
    quarter = HEAD_DIM // 4
    t = jnp.arange(n_lat)
    inv_freq = ROPE_BASE ** (-jnp.arange(quarter, dtype=F32) / quarter)
    ang_r = (t // GRID_W).astype(F32)[:, None] * inv_freq[None, :]
    ang_c = (t % GRID_W).astype(F32)[:, None] * inv_freq[None, :]
    cos = jnp.concatenate([jnp.cos(ang_r), jnp.cos(ang_r), jnp.cos(ang_c), jnp.cos(ang_c)], axis=-1)
    sin = jnp.concatenate([-jnp.sin(ang_r), jnp.sin(ang_r), -jnp.sin(ang_c), jnp.sin(ang_c)], axis=-1)
    cos = jnp.concatenate([jnp.ones((CTX_LEN, HEAD_DIM), F32), cos], axis=0)
    sin = jnp.concatenate([jnp.zeros((CTX_LEN, HEAD_DIM), F32), sin], axis=0)
    return cos, sin


def _retention(p3, gn_g, cos, sin):
    l = p3.shape[1]
    b = p3.shape[0] // (N_IN_SPLITS * N_HEADS)
    gw = N_HEADS * HEAD_DIM
    log_gamma = jnp.log(1.0 - 2.0 ** (-5.0 - jnp.arange(N_HEADS, dtype=F32)))
    lg = jnp.stack([log_gamma, log_gamma[::-1]], axis=1)
    lg = jnp.broadcast_to(lg[:, :, None], (N_HEADS, 2, HEAD_DIM))
    nh = HEADS_PER_STEP
    full = pl.BlockSpec((l, HEAD_DIM), lambda bb, hp: (0, 0))
    slabs = [s for split in (5, 6, 7, 8) for s in _pair_slab_specs(b, l, split)]
    return pl.pallas_call(
        functools.partial(_retention_kernel, seq_len=l),
        out_shape=jax.ShapeDtypeStruct((b, l, gw), BF16),
        grid=(b, N_HEADS // nh),
        in_specs=slabs + [full, full,
                          pl.BlockSpec((nh, 2, HEAD_DIM), lambda bb, hp: (hp, 0, 0)),
                          pl.BlockSpec((1, nh * HEAD_DIM), lambda bb, hp: (0, hp))],
        out_specs=pl.BlockSpec((1, l, nh * HEAD_DIM), lambda bb, hp: (bb, 0, hp)),
        scratch_shapes=_decay_scratch(l),
        compiler_params=_cparams(("arbitrary", "arbitrary"), VMEM_LIMIT),
        name="retention",
    )(*([p3] * (4 * nh)), cos, sin, lg, gn_g.reshape(1, gw))


def _hgrn_kernel(*refs, seq_len, layer, depth):
    nh = HEADS_PER_STEP
    ff_refs, fb_refs, i_refs, q_refs, g_refs = (refs[i * nh:(i + 1) * nh] for i in range(5))
    lbl_ref, on_ref, y_ref = refs[5 * nh:5 * nh + 3]
    sets = _head_sets(refs[5 * nh + 3:])
    l = seq_len
    n = ATTN_BLOCK
    logits = lbl_ref[...]
    log_lb, log1m_lb = [], []
    for dd in range(2):
        xs = [logits[ll * 2 + dd] for ll in range(depth)]
        mx = functools.reduce(jnp.maximum, xs)
        es = [jnp.exp(x - mx) for x in xs]
        tot = functools.reduce(lambda a, b_: a + b_, es)
        lb = jnp.zeros_like(tot)
        for ll in range(1, layer + 1):
            lb = lb + es[ll] / tot
        log_lb.append(jnp.log(lb))
        log1m_lb.append(jnp.log1p(-lb))

    def gate(fz, dd, cols):
        log_sig = jnp.minimum(fz, 0.0) - jnp.log(1.0 + jnp.exp(-jnp.abs(fz)))
        a = jnp.broadcast_to(log_lb[dd][:, cols], fz.shape)
        bv = log1m_lb[dd][:, cols] + log_sig
        log_f = jnp.maximum(a, bv) + jnp.log(1.0 + jnp.exp(-jnp.abs(a - bv)))
        return log_f, 1.0 - jnp.exp(log_f)

    def block_body(bi, carry):
        s0 = pl.multiple_of(bi * n, n)
        heads = []
        for hh in range(nh):
            cols = slice(hh * HEAD_DIM, (hh + 1) * HEAD_DIM)
            lf_f, k_f = gate(ff_refs[hh][0, pl.ds(s0, n), :], 0, cols)
            lf_b, k_b = gate(fb_refs[hh][0, pl.ds(s0, n), :], 1, cols)
            heads.append((q_refs[hh][0, pl.ds(s0, n), :], k_f, k_b, i_refs[hh][0, pl.ds(s0, n), :], lf_f, lf_b))
        _decay_attn_blocks(s0, heads, HGRN_CHUNK, sets)
        return carry

    lax.fori_loop(0, l // n, block_body, 0)
    _decay_attn_state_pass(HGRN_CHUNK, l, sets)

    def out_body(bi, carry):
        s0 = pl.multiple_of(bi * n, n)
        for hh in range(nh):
            cols = slice(hh * HEAD_DIM, (hh + 1) * HEAD_DIM)
            o = sets[hh][-1][pl.ds(s0, n), :]
            o = o * lax.rsqrt(jnp.mean(o * o, axis=-1, keepdims=True) + EPS) * on_ref[:, cols]
            y_ref[0, pl.ds(s0, n), cols] = (_silu(g_refs[hh][0, pl.ds(s0, n), :]) * o).astype(BF16)
        return carry

    lax.fori_loop(0, l // n, out_body, 0)


def _hgrn2(p3, lb_logits, on_g, layer):
    l = p3.shape[1]
    b = p3.shape[0] // (N_IN_SPLITS * N_HEADS)
    gw = N_HEADS * HEAD_DIM
    depth = lb_logits.shape[0]
    nh = HEADS_PER_STEP
    slabs = [s for split in (9, 10, 11, 12, 13) for s in _pair_slab_specs(b, l, split)]
    return pl.pallas_call(
        functools.partial(_hgrn_kernel, seq_len=l, layer=layer, depth=depth),
        out_shape=jax.ShapeDtypeStruct((b, l, gw), BF16),
        grid=(b, N_HEADS // nh),
        in_specs=slabs + [pl.BlockSpec((depth * 2, 1, nh * HEAD_DIM), lambda bb, hp: (0, 0, hp)),
                          pl.BlockSpec((1, nh * HEAD_DIM), lambda bb, hp: (0, hp))],
        out_specs=pl.BlockSpec((1, l, nh * HEAD_DIM), lambda bb, hp: (bb, 0, hp)),
        scratch_shapes=_decay_scratch(l),
        compiler_params=_cparams(("arbitrary", "arbitrary"), VMEM_LIMIT),
        name="hgrn2",
    )(*([p3] * (5 * nh)), lb_logits.astype(F32).reshape(depth * 2, 1, gw), on_g.reshape(1, gw))


def _topk_rows(x, k):
    r_n = x.shape[0]
    idx = lax.broadcasted_iota(jnp.int32, x.shape, 0).astype(F32)
    rank = jnp.full(x.shape, float(k), F32)
    vals = []
    for r in range(k):
        m = jnp.max(x, axis=0, keepdims=True)
        first = jnp.min(jnp.where(x == m, idx, float(r_n)), axis=0, keepdims=True)
        sel = idx == first
        rank = jnp.where(sel, float(r), rank)
        x = jnp.where(sel, -jnp.inf, x)
        vals.append(m)
    return rank, jnp.concatenate(vals, axis=0)


def _bf16_pair_words(x):
    bits = pltpu.bitcast(x.astype(BF16).astype(F32), jnp.uint32)
    return bits | (bits >> 16)


def _peer_route_kernel(zt_ref, wq_ref, sk_ref, rank2_ref, lim1_ref, e1_ref, e2_ref, *, tb):
    k = PEER_TOPK
    n_kc, _, kc = wq_ref.shape
    q = jnp.dot(wq_ref[0], zt_ref[0, 0:kc, :], preferred_element_type=F32)
    for ci in range(1, n_kc):
        q += jnp.dot(wq_ref[ci], zt_ref[0, ci * kc:(ci + 1) * kc, :], preferred_element_type=F32)
    half = q.shape[0] // 2
    for c0 in range(0, tb, LANES):
        s1 = _bdot(sk_ref[0, 0], q[:half, c0:c0 + LANES])
        s2 = _bdot(sk_ref[0, 1], q[half:, c0:c0 + LANES])
        rank1, top1 = _topk_rows(s1, k)
        rank2, top2 = _topk_rows(s2, k)
        e1 = jnp.exp(s1 - top1[0:1])
        e2 = jnp.exp(s2 - top2[0:1])
        e1t = jnp.exp(top1 - top1[0:1])
        e2t = jnp.exp(top2 - top2[0:1])
        n_multi = sum(1 for r in range(k) if k // (r + 1) > 1)
        assert k - n_multi == SUBLANES
        pieces, epieces, widths = [], [], []
        for r in range(n_multi):
            n_c = k // (r + 1)
            n_pad = -(-n_c // SUBLANES) * SUBLANES
            valid = lax.broadcasted_iota(jnp.int32, (n_pad, LANES), 0) < n_c
            pieces.append(jnp.where(valid, top1[r:r + 1] + top2[0:n_pad], -jnp.inf))
            epieces.append(e1t[r:r + 1] * e2t[0:n_pad])
            widths.append(n_pad)
        pieces.append(top1[n_multi:k] + top2[0:1])
        epieces.append(e1t[n_multi:k] * e2t[0:1])
        cand = jnp.concatenate(pieces, axis=0)
        ecand = jnp.concatenate(epieces, axis=0)
        crank, _ = _topk_rows(cand, k)
        sel = crank < k
        z = jnp.sum(jnp.where(sel, ecand, 0.0), axis=0, keepdims=True)
        lim1 = jnp.zeros(rank1.shape, F32)
        self32 = sel.astype(F32)
        off = 0
        for r in range(n_multi):
            cnt = jnp.sum(self32[off:off + widths[r]], axis=0, keepdims=True)
            lim1 = jnp.where(rank1 == r, cnt, lim1)
            off += widths[r]
        for r in range(n_multi, k):
            lim1 = jnp.where(rank1 == r, self32[off + r - n_multi:off + r - n_multi + 1], lim1)
        rank2_ref[0, 0, :, c0:c0 + LANES] = rank2.astype(BF16)
        lim1_ref[0, 0, :, c0:c0 + LANES] = _bf16_pair_words(lim1)
        e1_ref[0, 0, :, c0:c0 + LANES] = _bf16_pair_words(e1)
        e2_ref[0, 0, :, c0:c0 + LANES] = (e2 / z).astype(BF16)


def _peer_route(zt, wq_t, sub_keys):
    nb, d, tb = zt.shape
    t = nb * tb
    n_kc, _, kc = wq_t.shape
    qd = wq_t.shape[1] // PEER_HEADS
    tabs = [jax.ShapeDtypeStruct((nb, PEER_HEADS, PEER_NKEYS, tb), dt)
            for dt in (BF16, jnp.uint32, jnp.uint32, BF16)]
    tab_spec = pl.BlockSpec((1, 1, PEER_NKEYS, tb), lambda i, h: (i, h, 0, 0))
    return pl.pallas_call(
        functools.partial(_peer_route_kernel, tb=tb),
        out_shape=tabs,
        grid=(t // tb, PEER_HEADS),
        in_specs=[pl.BlockSpec((1, d, tb), lambda i, h: (i, 0, 0)),
                  pl.BlockSpec((n_kc, qd, kc), lambda i, h: (0, h, 0)),
                  pl.BlockSpec((1, 2, PEER_NKEYS, qd // 2), lambda i, h: (h, 0, 0, 0))],
        out_specs=[tab_spec] * 4,
        compiler_params=_cparams(("arbitrary", "arbitrary"), VMEM_LIMIT),
        name="peer_route",
    )(zt, wq_t, sub_keys)


def _peer_expert_kernel(zt_ref, u_ref, vt_ref, rank2_ref, lim1_ref, e1_ref, e2_ref, o_ref, w_ref, *, ek):
    e = pl.program_id(1)
    tb = zt_ref.shape[2]
    n_i1 = ek // PEER_NKEYS

    @pl.when(e == 0)
    def _():
        o_ref[...] = jnp.zeros(o_ref.shape, F32)

    half = (PEER_NKEYS // 2, PEER_LANE_CHUNK)

    def gated(j, act):
        i1 = e * n_i1 + j
        for c0 in range(0, tb, PEER_LANE_CHUNK):
            cs = slice(c0, c0 + PEER_LANE_CHUNK)
            g = jnp.zeros((PEER_NKEYS, PEER_LANE_CHUNK), BF16)
            for h in range(PEER_HEADS):
                lim = pltpu.bitcast(jnp.broadcast_to(lim1_ref[0, h, pl.ds(i1, 1), cs], half), BF16)
                w1 = pltpu.bitcast(jnp.broadcast_to(e1_ref[0, h, pl.ds(i1, 1), cs], half), BF16)
                g = g + jnp.where(rank2_ref[0, h, :, cs] < lim, e2_ref[0, h, :, cs], jnp.zeros_like(g)) * w1
            w_ref[j * PEER_NKEYS:(j + 1) * PEER_NKEYS, cs] = (g.astype(F32) * _gelu(act[:, cs])).astype(BF16)

    per_batch = PEER_UP_BATCH // PEER_NKEYS
    per_group = PEER_DOWN_GROUP // PEER_NKEYS
    for b0 in range(0, n_i1, per_batch):
        acts = {j: jnp.dot(u_ref[j * PEER_NKEYS:(j + 1) * PEER_NKEYS, :], zt_ref[0], preferred_element_type=F32)
                for j in range(b0, b0 + per_batch)}
        for g0 in range(b0, b0 + per_batch, per_group):
            for j in range(g0, g0 + per_group):
                gated(j, acts[j])
            ks = slice(g0 * PEER_NKEYS, (g0 + per_group) * PEER_NKEYS)
            o_ref[0] += jnp.dot(vt_ref[0, :, ks], w_ref[ks, :], preferred_element_type=F32)


def _peer_experts(zt, u, vt, tabs, ek):
    nb, d, tb = zt.shape
    ne = u.shape[0]
    once = pl.Buffered(1)
    tab_spec = pl.BlockSpec((1, PEER_HEADS, PEER_NKEYS, tb), lambda i, e: (i, 0, 0, 0), pipeline_mode=once)
    return pl.pallas_call(
        functools.partial(_peer_expert_kernel, ek=ek),
        out_shape=jax.ShapeDtypeStruct((nb, d, tb), F32),
        grid=(nb, ne // ek),
        in_specs=[pl.BlockSpec((1, d, tb), lambda i, e: (i, 0, 0), pipeline_mode=once),
                  pl.BlockSpec((ek, d), lambda i, e: (e, 0)),
                  pl.BlockSpec((1, d, ek), lambda i, e: (e, 0, 0)),
                  tab_spec, tab_spec, tab_spec, tab_spec],
        out_specs=pl.BlockSpec((1, d, tb), lambda i, e: (i, 0, 0), pipeline_mode=once),
        scratch_shapes=[pltpu.VMEM((ek, tb), BF16)],
        compiler_params=_cparams(("arbitrary", "arbitrary"), PEER_VMEM_LIMIT),
        name="peer_experts",
    )(zt, u, vt, *tabs)


def kernel(x, c, ctx, c_ctx, norm1_g, norm2_g, ada_w, ada_b, w_in, w_out, lru_conv_w, lru_conv_b, lru_wa, lru_ba,
           lru_wx, lru_bx, lru_lam, na_qn_g, na_kn_g, na_rpb, ret_gn_g, hgrn_lb_logits, hgrn_on_g, peer_wq,
           peer_subkeys, peer_u, peer_v):
    bsz, n_lat, d = x.shape
    depth = w_in.shape[0]
    l = CTX_LEN + n_lat
    gw = d // N_GROUPS
    assert bsz < MOD_ROWS and n_lat % ROW_TILE == 0 and CTX_LEN % ROW_TILE == 0
    ctx_row = bsz

    c16 = jnp.zeros((MOD_ROWS, d), F32).at[:bsz].set(c).at[ctx_row].set(c_ctx)
    cos, sin = _rope_tables(l)
    rows = n_lat // GRID_W

    h3 = jnp.concatenate([ctx, x], axis=1)
    tiles_full = l // ROW_TILE
    tiles_lat = n_lat // ROW_TILE
    ctx_tiles = CTX_LEN // ROW_TILE

    def row_full(i):
        return jnp.where(i % tiles_full < ctx_tiles, ctx_row, i // tiles_full)

    def row_lat(i):
        return i // tiles_lat

    h = h3.reshape(bsz * l, d)
    pending = None
    for layer in range(depth):
        last = layer == depth - 1
        mod3 = _ada_modulation(c16, ada_w, ada_b, layer).reshape(MOD_ROWS * N_MOD, 1, d)
        if pending is None:
            (u,) = _resid_norm(h, mod3, row_full, norm_g=norm1_g[layer], shift_k=0, scale_k=1)
        else:
            ft, mod3_prev = pending
            h, u = _resid_norm(h, mod3, row_full, ft=ft, gate_mod3=mod3_prev, gate_k=5,
                               norm_g=norm1_g[layer], shift_k=0, scale_k=1)
        p = _matmul(u, w_in, layer, 1024, 1024)
        p3 = p.reshape(N_IN_SPLITS * N_HEADS * bsz, l, HEAD_DIM)
        y_a = _rglru(p3, lru_conv_w[layer], lru_conv_b[layer], lru_wa[layer], lru_ba[layer],
                     lru_wx[layer], lru_bx[layer], lru_lam[layer])
        y_b = _natten(p3, na_qn_g[layer], na_kn_g[layer], _na_bias_table(na_rpb[layer], rows))
        y_c = _retention(p3, ret_gn_g[layer], cos, sin)
        y_d = _hgrn2(p3, hgrn_lb_logits, hgrn_on_g[layer], layer)
        if last:
            hm3 = _out_proj((y_a, y_b, y_c, y_d), w_out, layer, h.reshape(bsz, l, d), mod3, row_off=ctx_tiles,
                            row_fn=lambda bb, i: bb, gate_k=2)
            row_fn = row_lat
        else:
            hm3 = _out_proj((y_a, y_b, y_c, y_d), w_out, layer, h.reshape(bsz, l, d), mod3, row_off=0,
                            row_fn=lambda bb, i: jnp.where(i < ctx_tiles, ctx_row, bb), gate_k=2)
            row_fn = row_full
        h = hm3.reshape(-1, d)
        (zt,) = _resid_norm(h, mod3, row_fn, norm_g=norm2_g[layer], shift_k=3, scale_k=4, transpose_out=True)
        wq_t = _cast_weight(peer_wq, layer, transpose=True, rows_per_step=512)
        u_b = _cast_weight(peer_u, layer, transpose=False, rows_per_step=512)
        v_t = _cast_weight(peer_v, layer, transpose=True, rows_per_step=PEER_EXPERT_BLOCK)
        tabs = _peer_route(zt, wq_t, peer_subkeys[layer])
        ft = _peer_experts(zt, u_b, v_t, tabs, PEER_EXPERT_BLOCK)
        if last:
            (h,) = _resid_norm(h, mod3, row_fn, ft=ft, gate_k=5)
        else:
            pending = (ft, mod3)
    return h.reshape(bsz, n_lat, d)
```
